```python
import numpy as np
import jax
import jax.numpy as jnp
from jax import lax

D_MODEL = 1024
BATCH = 2
SEQ = 8192
DEPTH = 4
DEC_BATCH = 128
DEC_SEQ = 4
PAST_LEN = 2048
PAGE_SIZE = 128

N_A_LAYERS = DEPTH // 2
N_B_LAYERS = DEPTH - N_A_LAYERS
RW_HEAD_DIM = 64
RW_HEADS = D_MODEL // RW_HEAD_DIM
LORA_W = 64
LORA_A = 64
LORA_V = 32
LORA_G = 128
RW_GN_EPS = 64e-5
NSA_HEAD_DIM = 64
NSA_HEADS = D_MODEL // NSA_HEAD_DIM
NSA_GROUPS = 4
HPG = NSA_HEADS // NSA_GROUPS
CMP_LEN = 32
CMP_STRIDE = 16
CMP_HID = 128
SEL_BLOCK = 64
N_SELECT = 16
WINDOW = 512
Q_BLOCK = 128
N_KV_PARTS = 6
N_PAGED_PARTS = 4
D_FF = 2816
CONV_W = 3
NORM_EPS = 1e-6
NEG_INF = -1e30
FORCE_SCORE = 1e9
MASK_SCORE = -1e9

kernel_name = "rwkv7_yoco_nsa_convffn_step"


def rmsnorm(x, g):
    xf = x.astype(jnp.float32)
    y = xf * lax.rsqrt(jnp.mean(xf * xf, -1, keepdims=True) + NORM_EPS)
    return (y * g.astype(jnp.float32)).astype(x.dtype)


def masked_softmax(s, mask):
    s = jnp.where(mask, s.astype(jnp.float32), NEG_INF)
    e = jnp.where(mask, jnp.exp(s - jnp.max(s, -1, keepdims=True)), 0.0)
    return e / jnp.maximum(jnp.sum(e, -1, keepdims=True), 1e-30)


def to_heads(x):
    return x.reshape(x.shape[0], x.shape[1], RW_HEADS, RW_HEAD_DIM)


def conv_ffn(xn, conv_state, w_in, conv_w, conv_b, w_out):
    t_len = xn.shape[1]
    u, gate = jnp.split(xn @ w_in, 2, axis=-1)
    u_pad = jnp.concatenate([conv_state.astype(u.dtype), u], axis=1)
    uc = conv_b + conv_w[0] * u_pad[:, 0:t_len]
    for j in range(1, CONV_W):
        uc = uc + conv_w[j] * u_pad[:, j:j + t_len]
    return (jax.nn.gelu(uc) * gate) @ w_out, u_pad[:, t_len:]


def wkv7_scan(s0, r, w, k, v, a, b):
    def step(s, inp):
        r_t, w_t, k_t, v_t, a_t, b_t = inp
        sa = jnp.einsum('bhij,bhj->bhi', s, a_t)
        s = s * w_t[:, :, None, :] + sa[..., None] * b_t[:, :, None, :] + v_t[..., None] * k_t[:, :, None, :]
        return s, jnp.einsum('bhij,bhj->bhi', s, r_t)
    xs = tuple(jnp.swapaxes(t, 0, 1) for t in (r, w, k, v, a, b))
    s_final, y = lax.scan(step, s0, xs)
    return s_final, jnp.swapaxes(y, 0, 1)


def rwkv7_mix(xn, shift_prev, s0, v_first, li, p):
    bsz, t_len, _ = xn.shape
    f32 = jnp.float32
    xx = jnp.concatenate([shift_prev[:, None].astype(xn.dtype), xn[:, :-1]], axis=1) - xn
    mu = p['rw_mu'][li]
    xr, xw, xk, xv, xa, xg = (xn + xx * mu[i] for i in range(6))
    r = xr @ p['rw_w_r'][li]
    k = xk @ p['rw_w_k'][li]
    v = xv @ p['rw_w_v'][li]
    w_raw = -jax.nn.softplus(-(p['rw_w0'][li] + jnp.tanh(xw @ p['rw_w1'][li]) @ p['rw_w2'][li])) - 0.5
    decay = jnp.exp(-jnp.exp(w_raw.astype(f32)))
    a = jax.nn.sigmoid(p['rw_a0'][li] + (xa @ p['rw_a1'][li]) @ p['rw_a2'][li])
    if li == 0:
        v_first = v
    else:
        v = v + (v_first - v) * jax.nn.sigmoid(p['rw_v0'][li - 1] + (xv @ p['rw_v1'][li - 1]) @ p['rw_v2'][li - 1])
    g = jax.nn.sigmoid(xg @ p['rw_g1'][li]) @ p['rw_g2'][li]
    kk = to_heads(k * p['rw_k_k'][li]).astype(f32)
    kk = kk / jnp.maximum(jnp.sqrt(jnp.sum(kk * kk, -1, keepdims=True)), 1e-12)
    k = k * (1 + (a - 1) * p['rw_k_a'][li])
    rh, kh, vh, ah = (to_heads(t).astype(f32) for t in (r, k, v, a))
    s_new, y = wkv7_scan(s0.astype(f32), rh, to_heads(decay), kh, vh, -kk, kk * ah)
    mean = jnp.mean(y, -1, keepdims=True)
    var = jnp.mean(jnp.square(y - mean), -1, keepdims=True)
    yn = ((y - mean) * lax.rsqrt(var + RW_GN_EPS)).reshape(bsz, t_len, D_MODEL)
    yn = yn * p['rw_ln_w'][li] + p['rw_ln_b'][li]
    bonus = jnp.sum(rh * kh * p['rw_r_k'][li], -1, keepdims=True) * vh
    y = (yn + bonus.reshape(bsz, t_len, D_MODEL)).astype(xn.dtype)
    return (y * g) @ p['rw_w_o'][li], s_new.astype(s0.dtype), xn[:, -1], v_first


def kv_rows(h, p):
    bsz, t_len, _ = h.shape
    kv = (rmsnorm(h, p['kv_norm_g']) @ p['w_kv']).reshape(bsz, t_len, N_KV_PARTS, NSA_GROUPS, NSA_HEAD_DIM)
    k_sel = rmsnorm(kv[:, :, 2], p['k_norm_g'][1])
    k_win = rmsnorm(kv[:, :, 4], p['k_norm_g'][2])
    paged = jnp.stack([kv[:, :, 0], kv[:, :, 1], k_sel, kv[:, :, 3]], axis=2)
    win = jnp.stack([k_win, kv[:, :, 5]], axis=2)
    return paged, win


def compress_rows(rows, pe, w1, b1, w2, b2):
    bsz, t_len, grp, dh = rows.shape
    n_sub = t_len // CMP_STRIDE
    sub = rows[:, :n_sub * CMP_STRIDE].reshape(bsz, n_sub, CMP_STRIDE, grp, dh)
    w1 = w1.reshape(CMP_LEN, dh, CMP_HID)
    p0 = jnp.einsum('bnsgd,sdh->bngh', sub + pe[None, None, :CMP_STRIDE, None, :], w1[:CMP_STRIDE])
    p1 = jnp.einsum('bnsgd,sdh->bngh', sub + pe[None, None, CMP_STRIDE:, None, :], w1[CMP_STRIDE:])
    hid = jax.nn.gelu(p0[:, :-1] + p1[:, 1:] + b1)
    return hid @ w2 + b2


def build_context(paged, p):
    bsz, t_len = paged.shape[:2]
    ck = rmsnorm(compress_rows(paged[:, :, 0], p['cmp_pe'][0], p['cmp_w1'][0], p['cmp_b1'][0], p['cmp_w2'][0], p['cmp_b2'][0]), p['k_norm_g'][0])
    cv = compress_rows(paged[:, :, 1], p['cmp_pe'][1], p['cmp_w1'][1], p['cmp_b1'][1], p['cmp_w2'][1], p['cmp_b2'][1])
    n_cmp = ck.shape[1]
    c_end = jnp.arange(n_cmp) * CMP_STRIDE + (CMP_LEN - 1)
    n_sel_blocks = -(-t_len // SEL_BLOCK)
    sel = jnp.pad(paged[:, :, 2:4], ((0, 0), (0, n_sel_blocks * SEL_BLOCK - t_len), (0, 0), (0, 0), (0, 0)))
    sel = sel.reshape(bsz, n_sel_blocks, SEL_BLOCK, 2, NSA_GROUPS, NSA_HEAD_DIM).transpose(3, 0, 4, 1, 2, 5)
    ci = np.arange(n_cmp)[:, None] * CMP_STRIDE
    sj = np.arange(n_sel_blocks)[None, :] * SEL_BLOCK
    ov = np.clip(np.minimum(ci + CMP_LEN, sj + SEL_BLOCK) - np.maximum(ci, sj), 0, None) / CMP_STRIDE
    return (ck, cv, c_end, sel[0], sel[1], jnp.asarray(ov, jnp.float32))


def nsa_core(q, gates, q_pos, core, kw, vw, w_pos):
    ck, cv, c_end, ks_blk, vs_blk, imp_map = core
    bsz, n_q = q.shape[:2]
    s_c = jnp.einsum('bqghd,bngd->bqghn', q, ck)
    m_c = (c_end[None, :] <= q_pos[:, None])[None, :, None, None, :]
    p_c = masked_softmax(s_c, m_c)
    o_c = jnp.einsum('bqghn,bngd->bqghd', p_c.astype(cv.dtype), cv)
    imp = jnp.einsum('bqgn,ns->bqgs', jnp.sum(p_c, axis=3), imp_map)
    n_blocks = imp_map.shape[1]
    blk = jnp.arange(n_blocks)[None, :]
    qb = (q_pos // SEL_BLOCK)[:, None]
    forced = (blk == 0) | (blk == qb) | (blk == qb - 1)
    causal = blk * SEL_BLOCK <= q_pos[:, None]
    score = jnp.where(causal[None, :, None], jnp.where(forced[None, :, None], FORCE_SCORE, imp), MASK_SCORE)
    n_sel = min(N_SELECT, n_blocks)
    top_s, top_i = lax.top_k(score, n_sel)
    b_ix = jnp.arange(bsz)[:, None, None, None]
    g_ix = jnp.arange(NSA_GROUPS)[None, None, :, None]
    ks = ks_blk[b_ix, g_ix, top_i]
    vs = vs_blk[b_ix, g_ix, top_i]
    key_pos = top_i[..., None] * SEL_BLOCK + jnp.arange(SEL_BLOCK)
    m_s = (top_s > MASK_SCORE / 2)[..., None] & (key_pos <= q_pos[None, :, None, None, None])
    s_s = jnp.einsum('bqghd,bqgnkd->bqghnk', q, ks).reshape(bsz, n_q, NSA_GROUPS, HPG, n_sel * SEL_BLOCK)
    p_s = masked_softmax(s_s, m_s.reshape(bsz, n_q, NSA_GROUPS, 1, n_sel * SEL_BLOCK))
    p_s = p_s.reshape(bsz, n_q, NSA_GROUPS, HPG, n_sel, SEL_BLOCK)
    o_s = jnp.einsum('bqghnk,bqgnkd->bqghd', p_s.astype(vs.dtype), vs)
    s_w = jnp.einsum('bqghd,bkgd->bqghk', q, kw)
    dist = q_pos[:, None] - w_pos[None, :]
    m_w = ((dist >= 0) & (dist < WINDOW) & (w_pos[None, :] >= 0))[None, :, None, None, :]
    p_w = masked_softmax(s_w, m_w)
    o_w = jnp.einsum('bqghk,bkgd->bqghd', p_w.astype(vw.dtype), vw)
    return gates[..., 0:1] * o_c + gates[..., 1:2] * o_s + gates[..., 2:3] * o_w


def nsa_query(xn, lb, p):
    bsz, t_len, _ = xn.shape
    proj = xn @ p['nsa_w_q'][lb]
    q = proj[..., :NSA_HEADS * NSA_HEAD_DIM].reshape(bsz, t_len, NSA_GROUPS, HPG, NSA_HEAD_DIM)
    q = rmsnorm(q, p['q_norm_g'][lb]) * (NSA_HEAD_DIM ** -0.5)
    gates = jax.nn.sigmoid(proj[..., NSA_HEADS * NSA_HEAD_DIM:]).reshape(bsz, t_len, NSA_GROUPS, HPG, 3)
    return q, gates


def attend_prompt(q, gates, ctx):
    core, win_pad = ctx
    bsz, t_len = q.shape[:2]
    kw_pad, vw_pad = win_pad[:, :, 0], win_pad[:, :, 1]

    def chunk(c):
        start = c * Q_BLOCK
        qc = lax.dynamic_slice_in_dim(q, start, Q_BLOCK, axis=1)
        gc = lax.dynamic_slice_in_dim(gates, start, Q_BLOCK, axis=1)
        kw = lax.dynamic_slice_in_dim(kw_pad, start, WINDOW + Q_BLOCK, axis=1)
        vw = lax.dynamic_slice_in_dim(vw_pad, start, WINDOW + Q_BLOCK, axis=1)
        q_pos = start + jnp.arange(Q_BLOCK)
        w_pos = start - WINDOW + jnp.arange(WINDOW + Q_BLOCK)
        return nsa_core(qc, gc, q_pos, core, kw, vw, w_pos)

    o = lax.map(chunk, jnp.arange(t_len // Q_BLOCK))
    return jnp.moveaxis(o, 0, 1).reshape(bsz, t_len, NSA_HEADS * NSA_HEAD_DIM)


def attend_sample(q, gates, ctx):
    core, win_full = ctx
    bsz, t_new = q.shape[:2]
    n_buf = win_full.shape[1] - t_new
    q_pos = PAST_LEN + jnp.arange(t_new)
    w_pos = PAST_LEN - n_buf + jnp.arange(win_full.shape[1])
    o = nsa_core(q, gates, q_pos, core, win_full[:, :, 0], win_full[:, :, 1], w_pos)
    return o.reshape(bsz, t_new, NSA_HEADS * NSA_HEAD_DIM)


def forward(x, wkv_in, shift_in, conv_in, make_ctx, attend, p):
    wkv_out, shift_out, conv_out = [], [], []
    v_first = None
    ctx = None
    kv_state = None
    for layer in range(DEPTH):
        if layer == N_A_LAYERS:
            ctx, kv_state = make_ctx(x)
        xn = rmsnorm(x, p['norm_g'][layer, 0])
        if layer < N_A_LAYERS:
            mix, s_new, sh_new, v_first = rwkv7_mix(xn, shift_in[layer], wkv_in[layer], v_first, layer, p)
            wkv_out.append(s_new)
            shift_out.append(sh_new)
        else:
            lb = layer - N_A_LAYERS
            q, gates = nsa_query(xn, lb, p)
            mix = attend(q, gates, ctx) @ p['nsa_w_o'][lb]
        x = x + mix
        xn = rmsnorm(x, p['norm_g'][layer, 1])
        f, conv_new = conv_ffn(xn, conv_in[layer], p['ffn_w_in'][layer], p['ffn_conv_w'][layer], p['ffn_conv_b'][layer], p['ffn_w_out'][layer])
        conv_out.append(conv_new)
        x = x + f
    return x, kv_state, jnp.stack(wkv_out), jnp.stack(shift_out), jnp.stack(conv_out)


def setup_inputs(seed: int = 0) -> dict:
    key = jax.random.key(seed)
    ks = iter(jax.random.split(key, 64))
    f32 = jnp.float32

    def nrm(shape, scale):
        return jax.random.normal(next(ks), shape, f32) * scale

    def gain(shape):
        return 1.0 + nrm(shape, 0.02)

    n_pages = PAST_LEN // PAGE_SIZE
    n_used = DEC_BATCH * n_pages
    n_phys = (5 * n_used + 3) // 4
    page_table = jax.random.permutation(next(ks), n_phys)[:n_used].reshape(DEC_BATCH, n_pages).astype(jnp.int32)
    nq_cols = NSA_HEADS * NSA_HEAD_DIM + 3 * NSA_HEADS
    return {
        'x_prompt': nrm((BATCH, SEQ, D_MODEL), 1.0),
        'x_sample': nrm((DEC_BATCH, DEC_SEQ, D_MODEL), 1.0),
        'cache_kv': nrm((n_phys, PAGE_SIZE, N_PAGED_PARTS, NSA_GROUPS, NSA_HEAD_DIM), 1.0),
        'cache_kv_win': nrm((DEC_BATCH, min(WINDOW, PAST_LEN), 2, NSA_GROUPS, NSA_HEAD_DIM), 1.0),
        'state_wkv': nrm((N_A_LAYERS, DEC_BATCH, RW_HEADS, RW_HEAD_DIM, RW_HEAD_DIM), 0.1),
        'state_shift': nrm((N_A_LAYERS, DEC_BATCH, D_MODEL), 1.0),
        'state_ffn_conv': nrm((DEPTH, DEC_BATCH, CONV_W - 1, D_FF), 1.0),
        'page_table': page_table,
        'norm_g': gain((DEPTH, 2, D_MODEL)),
        'rw_mu': jax.random.uniform(next(ks), (N_A_LAYERS, 6, D_MODEL), f32),
        'rw_w_r': nrm((N_A_LAYERS, D_MODEL, D_MODEL), D_MODEL ** -0.5),
        'rw_w_k': nrm((N_A_LAYERS, D_MODEL, D_MODEL), D_MODEL ** -0.5),
        'rw_w_v': nrm((N_A_LAYERS, D_MODEL, D_MODEL), D_MODEL ** -0.5),
        'rw_w_o': nrm((N_A_LAYERS, D_MODEL, D_MODEL), D_MODEL ** -0.5),
        'rw_w0': jax.random.uniform(next(ks), (N_A_LAYERS, D_MODEL), f32, minval=-6.0, maxval=-1.0),
        'rw_w1': nrm((N_A_LAYERS, D_MODEL, LORA_W), 0.5 * D_MODEL ** -0.5),
        'rw_w2': nrm((N_A_LAYERS, LORA_W, D_MODEL), 0.5 * LORA_W ** -0.5),
        'rw_a0': nrm((N_A_LAYERS, D_MODEL), 0.1),
        'rw_a1': nrm((N_A_LAYERS, D_MODEL, LORA_A), D_MODEL ** -0.5),
        'rw_a2': nrm((N_A_LAYERS, LORA_A, D_MODEL), 0.5 * LORA_A ** -0.5),
        'rw_v0': nrm((N_A_LAYERS - 1, D_MODEL), 0.1),
        'rw_v1': nrm((N_A_LAYERS - 1, D_MODEL, LORA_V), D_MODEL ** -0.5),
        'rw_v2': nrm((N_A_LAYERS - 1, LORA_V, D_MODEL), 0.5 * LORA_V ** -0.5),
        'rw_g1': nrm((N_A_LAYERS, D_MODEL, LORA_G), D_MODEL ** -0.5),
        'rw_g2': nrm((N_A_LAYERS, LORA_G, D_MODEL), LORA_G ** -0.5),
        'rw_k_k': 0.85 + nrm((N_A_LAYERS, D_MODEL), 0.05),
        'rw_k_a': 1.0 + nrm((N_A_LAYERS, D_MODEL), 0.05),
        'rw_r_k': nrm((N_A_LAYERS, RW_HEADS, RW_HEAD_DIM), 0.1),
        'rw_ln_w': gain((N_A_LAYERS, D_MODEL)),
        'rw_ln_b': nrm((N_A_LAYERS, D_MODEL), 0.02),
        'kv_norm_g': gain((D_MODEL,)),
        'w_kv': nrm((D_MODEL, N_KV_PARTS * NSA_GROUPS * NSA_HEAD_DIM), D_MODEL ** -0.5),
        'k_norm_g': gain((3, NSA_HEAD_DIM)),
        'cmp_pe': nrm((2, CMP_LEN, NSA_HEAD_DIM), 0.1),
        'cmp_w1': nrm((2, CMP_LEN * NSA_HEAD_DIM, CMP_HID), (CMP_LEN * NSA_HEAD_DIM) ** -0.5),
        'cmp_b1': nrm((2, CMP_HID), 0.02),
        'cmp_w2': nrm((2, CMP_HID, NSA_HEAD_DIM), CMP_HID ** -0.5),
        'cmp_b2': nrm((2, NSA_HEAD_DIM), 0.02),
        'nsa_w_q': nrm((N_B_LAYERS, D_MODEL, nq_cols), D_MODEL ** -0.5),
        'q_norm_g': gain((N_B_LAYERS, NSA_HEAD_DIM)),
        'nsa_w_o': nrm((N_B_LAYERS, NSA_HEADS * NSA_HEAD_DIM, D_MODEL), D_MODEL ** -0.5),
        'ffn_w_in': nrm((DEPTH, D_MODEL, 2 * D_FF), D_MODEL ** -0.5),
        'ffn_conv_w': nrm((DEPTH, CONV_W, D_FF), CONV_W ** -0.5),
        'ffn_conv_b': nrm((DEPTH, D_FF), 0.02),
        'ffn_w_out': nrm((DEPTH, D_FF, D_MODEL), D_FF ** -0.5),
    }


def reference(x_prompt, x_sample, cache_kv, cache_kv_win, state_wkv, state_shift, state_ffn_conv, page_table,
              norm_g, rw_mu, rw_w_r, rw_w_k, rw_w_v, rw_w_o, rw_w0, rw_w1, rw_w2, rw_a0, rw_a1, rw_a2,
              rw_v0, rw_v1, rw_v2, rw_g1, rw_g2, rw_k_k, rw_k_a, rw_r_k, rw_ln_w, rw_ln_b,
              kv_norm_g, w_kv, k_norm_g, cmp_pe, cmp_w1, cmp_b1, cmp_w2, cmp_b2,
              nsa_w_q, q_norm_g, nsa_w_o, ffn_w_in, ffn_conv_w, ffn_conv_b, ffn_w_out):
    p = dict(norm_g=norm_g, rw_mu=rw_mu, rw_w_r=rw_w_r, rw_w_k=rw_w_k, rw_w_v=rw_w_v, rw_w_o=rw_w_o,
             rw_w0=rw_w0, rw_w1=rw_w1, rw_w2=rw_w2, rw_a0=rw_a0, rw_a1=rw_a1, rw_a2=rw_a2,
             rw_v0=rw_v0, rw_v1=rw_v1, rw_v2=rw_v2, rw_g1=rw_g1, rw_g2=rw_g2, rw_k_k=rw_k_k, rw_k_a=rw_k_a,
             rw_r_k=rw_r_k, rw_ln_w=rw_ln_w, rw_ln_b=rw_ln_b, kv_norm_g=kv_norm_g, w_kv=w_kv,
             k_norm_g=k_norm_g, cmp_pe=cmp_pe, cmp_w1=cmp_w1, cmp_b1=cmp_b1, cmp_w2=cmp_w2, cmp_b2=cmp_b2,
             nsa_w_q=nsa_w_q, q_norm_g=q_norm_g, nsa_w_o=nsa_w_o, ffn_w_in=ffn_w_in,
             ffn_conv_w=ffn_conv_w, ffn_conv_b=ffn_conv_b, ffn_w_out=ffn_w_out)

    dt = x_prompt.dtype
    wkv0 = jnp.zeros((N_A_LAYERS, x_prompt.shape[0], RW_HEADS, RW_HEAD_DIM, RW_HEAD_DIM), dt)
    shift0 = jnp.zeros((N_A_LAYERS, x_prompt.shape[0], D_MODEL), dt)
    conv0 = jnp.zeros((DEPTH, x_prompt.shape[0], CONV_W - 1, D_FF), dt)

    def make_ctx_prompt(h):
        paged, win = kv_rows(h, p)
        t_len = h.shape[1]
        win_pad = jnp.pad(win, ((0, 0), (WINDOW, 0), (0, 0), (0, 0), (0, 0)))
        keep = min(WINDOW, t_len)
        return (build_context(paged, p), win_pad), (paged, win[:, t_len - keep:])

    y_prompt, (kv_prompt, win_prompt), wkv_prompt, shift_prompt, conv_prompt = forward(
        x_prompt, wkv0, shift0, conv0, make_ctx_prompt, attend_prompt, p)

    n_pages = PAST_LEN // PAGE_SIZE
    past = cache_kv[page_table].reshape(page_table.shape[0], n_pages * PAGE_SIZE, N_PAGED_PARTS, NSA_GROUPS, NSA_HEAD_DIM)
    n_buf = cache_kv_win.shape[1]

    def make_ctx_sample(h):
        paged, win = kv_rows(h, p)
        full = jnp.concatenate([past.astype(paged.dtype), paged], axis=1)
        win_full = jnp.concatenate([cache_kv_win.astype(win.dtype), win], axis=1)
        return (build_context(full, p), win_full), (paged, win_full[:, win_full.shape[1] - n_buf:])

    y_sample, (kv_sample, win_sample), wkv_sample, shift_sample, conv_sample = forward(
        x_sample, state_wkv, state_shift, state_ffn_conv, make_ctx_sample, attend_sample, p)

    return (y_prompt, y_sample, kv_prompt, kv_sample, win_prompt, win_sample,
            wkv_prompt, wkv_sample, shift_prompt, shift_sample, conv_prompt, conv_sample)
```

```python
import functools

import numpy as np
import jax
import jax.numpy as jnp
from jax import lax
from jax.experimental import pallas as pl
from jax.experimental.pallas import tpu as pltpu

f32 = jnp.float32
bf16 = jnp.bfloat16

D_MODEL = 1024
DEPTH = 4
N_A_LAYERS = 2
PAST_LEN = 2048
PAGE_SIZE = 128
N_PAGES = PAST_LEN // PAGE_SIZE
RW_HEAD_DIM = 64
RW_HEADS = 16
RW_GN_EPS = 64e-5
DH = 64
NSA_HEADS = 16
G = 4
HPG = 4
CMP_LEN = 32
CMP_STRIDE = 16
CMP_HID = 128
SEL_BLOCK = 64
N_SELECT = 16
WINDOW = 512
Q_BLOCK = 128
D_FF = 2816
CONV_W = 3
NORM_EPS = 1e-6
NEG_INF = -1e30
FORCE_SCORE = 1e9
MASK_SCORE = -1e9
PICKED = -3e38

LANES = 128
SUBLANES = 8
VMEM_LIMIT = 48 * 1024 * 1024


def _nt(a, b):
    return lax.dot_general(a, b, (((1,), (1,)), ((), ())), preferred_element_type=f32)


def _split_bf16(x):
    hi = x.astype(bf16)
    lo = (x - hi.astype(f32)).astype(bf16)
    return hi, lo


def _mm_kernel(x_ref, w_ref, o_ref):
    o_ref[...] = jnp.dot(x_ref[...].astype(bf16), w_ref[...], preferred_element_type=f32)


MM_TM = 512


def _pick_tile(n, cands):
    for c in cands:
        if n % c == 0:
            return c
    return n


def mm(x, w):
    m, k = x.shape
    n = w.shape[1]
    w = w.astype(bf16)
    tm = MM_TM if m >= MM_TM else -(-m // SUBLANES) * SUBLANES
    mp = -(-m // tm) * tm
    if mp != m:
        x = jnp.pad(x, ((0, mp - m), (0, 0)))
    tn = n if n <= 1536 else _pick_tile(n, (1408, 1024, 512, 256, 128))
    out = pl.pallas_call(
        _mm_kernel,
        grid=(mp // tm, n // tn),
        in_specs=[pl.BlockSpec((tm, k), lambda i, j: (i, 0)),
                  pl.BlockSpec((k, tn), lambda i, j: (0, j))],
        out_specs=pl.BlockSpec((tm, tn), lambda i, j: (i, j)),
        out_shape=jax.ShapeDtypeStruct((mp, n), f32),
        compiler_params=pltpu.CompilerParams(dimension_semantics=("parallel", "parallel"),
                                             vmem_limit_bytes=VMEM_LIMIT),
        name="mm",
    )(x, w)
    return out[:m] if mp != m else out


def mm3(x, w):
    b, t, k = x.shape
    return mm(x.reshape(b * t, k), w).reshape(b, t, w.shape[1])


PAIR_W = 2 * RW_HEAD_DIM
N_PAIRS = RW_HEADS // 2


def _scan_kernel(r_ref, w_ref, k_ref, v_ref, a_ref, b_ref, s0_ref, y_ref, s_ref, *, bb, tc):
    n = RW_HEAD_DIM
    npairs = bb * N_PAIRS

    @pl.when(pl.program_id(1) == 0)
    def _():
        s_ref[...] = s0_ref[...]

    row = lax.broadcasted_iota(jnp.int32, (n, PAIR_W), 0)
    lane = lax.broadcasted_iota(jnp.int32, (n, PAIR_W), 1)
    diag = (lane % n) == row
    kk = lax.broadcasted_iota(jnp.int32, (PAIR_W, PAIR_W), 0) // n
    ll = lax.broadcasted_iota(jnp.int32, (PAIR_W, PAIR_W), 1) // n
    bones = (kk == ll).astype(bf16)
    group = min(tc, SUBLANES)

    def seg_sum_bcast(x):
        return jnp.dot(x.astype(bf16), bones, preferred_element_type=f32)

    def token(t0, s):
        def rowvec(ref, bi, hp):
            tile = ref[bi, pl.ds(t0, group), pl.ds(hp * PAIR_W, PAIR_W)]
            return jnp.broadcast_to(tile[s:s + 1], (n, PAIR_W))

        pairs = [(bi, hp) for bi in range(bb) for hp in range(N_PAIRS)]
        st = [s_ref[p] for p in range(npairs)]
        sa_in = jnp.concatenate([st[p] * rowvec(a_ref, *pairs[p]) for p in range(npairs)], axis=0)
        v_in = jnp.concatenate([jnp.where(diag, rowvec(v_ref, *pairs[p]), 0.0) for p in range(npairs)], axis=0)
        sa_col = seg_sum_bcast(sa_in)
        v_col = seg_sum_bcast(v_in)
        new = []
        for p in range(npairs):
            sl = slice(p * n, (p + 1) * n)
            sn = (st[p] * rowvec(w_ref, *pairs[p]) + sa_col[sl] * rowvec(b_ref, *pairs[p])
                  + v_col[sl] * rowvec(k_ref, *pairs[p]))
            s_ref[p] = sn
            new.append(sn)
        y_in = jnp.concatenate([new[p] * rowvec(r_ref, *pairs[p]) for p in range(npairs)], axis=0)
        y_col = seg_sum_bcast(y_in)
        rows = []
        for p in range(npairs):
            sl = slice(p * n, (p + 1) * n)
            rows.append(jnp.sum(jnp.where(diag, y_col[sl], 0.0), axis=0, keepdims=True))
        return rows

    def block(g, carry):
        t0 = pl.multiple_of(g * group, group)
        rows = [token(t0, s) for s in range(group)]
        for p in range(npairs):
            bi, hp = divmod(p, N_PAIRS)
            y_ref[bi, pl.ds(t0, group), pl.ds(hp * PAIR_W, PAIR_W)] = jnp.concatenate(
                [rows[s][p] for s in range(group)], axis=0)
        return carry

    lax.fori_loop(0, tc // group, block, 0)


def wkv_scan(r, w, k, v, a, b, s0_pairs, *, bb, tc):
    bsz, t_len, _ = r.shape
    assert bsz % bb == 0 and t_len % tc == 0
    vec_spec = pl.BlockSpec((bb, tc, D_MODEL), lambda i, j: (i, j, 0))
    st_spec = pl.BlockSpec((bb * N_PAIRS, RW_HEAD_DIM, PAIR_W), lambda i, j: (i, 0, 0))
    return pl.pallas_call(
        functools.partial(_scan_kernel, bb=bb, tc=tc),
        grid=(bsz // bb, t_len // tc),
        in_specs=[vec_spec] * 6 + [st_spec],
        out_specs=[vec_spec, st_spec],
        out_shape=[jax.ShapeDtypeStruct((bsz, t_len, D_MODEL), f32),
                   jax.ShapeDtypeStruct((bsz * N_PAIRS, RW_HEAD_DIM, PAIR_W), f32)],
        compiler_params=pltpu.CompilerParams(dimension_semantics=("arbitrary", "arbitrary"),
                                             vmem_limit_bytes=VMEM_LIMIT),
        name="wkv_scan",
    )(r, w, k, v, a, b, s0_pairs)


def to_pairs(s):
    bsz = s.shape[0]
    return s.reshape(bsz, N_PAIRS, 2, RW_HEAD_DIM, RW_HEAD_DIM).transpose(0, 1, 3, 2, 4).reshape(
        bsz * N_PAIRS, RW_HEAD_DIM, PAIR_W)


def from_pairs(s, bsz):
    return s.reshape(bsz, N_PAIRS, RW_HEAD_DIM, 2, RW_HEAD_DIM).transpose(0, 1, 3, 2, 4).reshape(
        bsz, RW_HEADS, RW_HEAD_DIM, RW_HEAD_DIM)


def _masked_softmax(s, mask):
    s = jnp.where(mask, s, NEG_INF)
    e = jnp.where(mask, jnp.exp(s - jnp.max(s, axis=-1, keepdims=True)), 0.0)
    return e / jnp.maximum(jnp.sum(e, axis=-1, keepdims=True), 1e-30)


def _topk_rounds(score, idx, axis):
    sel = jnp.zeros(score.shape, f32)
    idx = idx.astype(f32)
    for _ in range(N_SELECT):
        m = jnp.max(score, axis=axis, keepdims=True)
        first = jnp.min(jnp.where(score == m, idx, 1e6), axis=axis, keepdims=True)
        pick = idx == first
        sel = jnp.where(pick, 1.0, sel)
        score = jnp.where(pick, PICKED, score)
    return sel > 0.5


def _imp_map_t(n_cmp_pad, n_blk_pad, n_cmp, n_blk):
    ci = np.arange(n_cmp_pad)[None, :] * CMP_STRIDE
    sj = np.arange(n_blk_pad)[:, None] * SEL_BLOCK
    ov = np.clip(np.minimum(ci + CMP_LEN, sj + SEL_BLOCK) - np.maximum(ci, sj), 0, None) / CMP_STRIDE
    ov = ov * (np.arange(n_cmp_pad)[None, :] < n_cmp) * (np.arange(n_blk_pad)[:, None] < n_blk)
    return jnp.asarray(ov, bf16)


NC_PAD = 512
NB_PROMPT = 128


def _cmp_select_kernel(q_ref, ck_ref, cv_ref, impt_ref, oc_ref, sb_ref, *, n_cmp):
    qi = pl.program_id(2)
    q = q_ref[0, 0, 0]
    s = _nt(q, ck_ref[0, 0])
    rows = HPG * Q_BLOCK
    t_loc = lax.broadcasted_iota(jnp.int32, (rows, NC_PAD), 0) & (Q_BLOCK - 1)
    q_pos = qi * Q_BLOCK + t_loc
    nn = lax.broadcasted_iota(jnp.int32, (rows, NC_PAD), 1)
    valid = (nn * CMP_STRIDE + (CMP_LEN - 1) <= q_pos) & (nn < n_cmp)
    p = _masked_softmax(s, valid)
    oc_ref[0, 0, 0] = jnp.dot(p.astype(bf16), cv_ref[0, 0], preferred_element_type=f32)
    psum = p[0:Q_BLOCK]
    for h in range(1, HPG):
        psum = psum + p[h * Q_BLOCK:(h + 1) * Q_BLOCK]
    hi, lo = _split_bf16(psum)
    imp_t = _nt(impt_ref[...], hi) + _nt(impt_ref[...], lo)
    blk = lax.broadcasted_iota(jnp.int32, (NB_PROMPT, Q_BLOCK), 0)
    qp = qi * Q_BLOCK + lax.broadcasted_iota(jnp.int32, (NB_PROMPT, Q_BLOCK), 1)
    qb = qp >> 6
    forced = (blk == 0) | (blk == qb) | (blk == qb - 1)
    causal = blk * SEL_BLOCK <= qp
    score = jnp.where(causal, jnp.where(forced, FORCE_SCORE, imp_t), MASK_SCORE)
    sel = _topk_rounds(score, blk, 0) & causal
    bias_t = jnp.where(sel, 0.0, MASK_SCORE)
    sb_ref[0, 0, 0] = bias_t.T.astype(bf16)


def nsa_cmp_select(qh, ck, cv, n_cmp):
    bsz, _, n_q = qh.shape[:3]
    rows = HPG * Q_BLOCK
    impt = _imp_map_t(NC_PAD, NB_PROMPT, n_cmp, NB_PROMPT)
    return pl.pallas_call(
        functools.partial(_cmp_select_kernel, n_cmp=n_cmp),
        grid=(bsz, G, n_q),
        in_specs=[pl.BlockSpec((1, 1, 1, rows, DH), lambda b, g, i: (b, g, i, 0, 0)),
                  pl.BlockSpec((1, 1, NC_PAD, DH), lambda b, g, i: (b, g, 0, 0)),
                  pl.BlockSpec((1, 1, NC_PAD, DH), lambda b, g, i: (b, g, 0, 0)),
                  pl.BlockSpec((NB_PROMPT, NC_PAD), lambda b, g, i: (0, 0))],
        out_specs=[pl.BlockSpec((1, 1, 1, rows, DH), lambda b, g, i: (b, g, i, 0, 0)),
                   pl.BlockSpec((1, 1, 1, Q_BLOCK, NB_PROMPT), lambda b, g, i: (b, g, i, 0, 0))],
        out_shape=[jax.ShapeDtypeStruct((bsz, G, n_q, rows, DH), f32),
                   jax.ShapeDtypeStruct((bsz, G, n_q, Q_BLOCK, NB_PROMPT), bf16)],
        compiler_params=pltpu.CompilerParams(dimension_semantics=("parallel", "parallel", "parallel"),
                                             vmem_limit_bytes=VMEM_LIMIT),
        name="nsa_cmp_select",
    )(qh, ck, cv, impt)


KEXT = NB_PROMPT + DH


def _sel_win_kernel(qe_ref, ke_ref, vs_ref, kw_ref, vw_ref, oc_ref, gt_ref, o_ref):
    qi = pl.program_id(2)
    rows = HPG * Q_BLOCK
    qe = qe_ref[0, 0, 0]
    q = qe[:, NB_PROMPT:]
    t_loc = lax.broadcasted_iota(jnp.int32, (rows, Q_BLOCK), 0) & (Q_BLOCK - 1)
    q_pos = qi * Q_BLOCK + t_loc
    kcol = lax.broadcasted_iota(jnp.int32, (rows, Q_BLOCK), 1)

    def flash_step(s, v, carry):
        m, l, acc = carry
        m_new = jnp.maximum(m, jnp.max(s, axis=1, keepdims=True))
        alpha = jnp.exp(m - m_new)
        p = jnp.exp(s - m_new)
        l = alpha * l + jnp.sum(p, axis=1, keepdims=True)
        acc = alpha * acc + jnp.dot(p.astype(bf16), v, preferred_element_type=f32)
        return m_new, l, acc

    init = (jnp.full((rows, 1), NEG_INF, f32), jnp.zeros((rows, 1), f32), jnp.zeros((rows, DH), f32))

    def sel_body(kt, carry):
        off = pl.multiple_of(kt * Q_BLOCK, Q_BLOCK)
        s = _nt(qe, ke_ref[0, 0, pl.ds(off, Q_BLOCK), :])
        return flash_step(s, vs_ref[0, 0, pl.ds(off, Q_BLOCK), :], carry)

    carry = lax.fori_loop(0, qi, sel_body, init)
    off = pl.multiple_of(qi * Q_BLOCK, Q_BLOCK)
    s = _nt(qe, ke_ref[0, 0, pl.ds(off, Q_BLOCK), :])
    s = jnp.where(kcol <= t_loc, s, 2.0 * MASK_SCORE)
    _, l_s, acc_s = flash_step(s, vs_ref[0, 0, pl.ds(off, Q_BLOCK), :], carry)
    o_s = acc_s / jnp.maximum(l_s, 1e-30)

    carry = init
    n_wt = WINDOW // Q_BLOCK + 1
    for j in range(n_wt):
        kt = qi - (n_wt - 1) + j
        ktc = jnp.maximum(kt, 0)
        off = pl.multiple_of(ktc * Q_BLOCK, Q_BLOCK)
        dist = q_pos - (ktc * Q_BLOCK + kcol)
        ok = (dist >= 0) & (dist < WINDOW) & (kt >= 0)
        s = jnp.where(ok, _nt(q, kw_ref[0, 0, pl.ds(off, Q_BLOCK), :]), NEG_INF)
        m, l, acc = carry
        m_new = jnp.maximum(m, jnp.max(s, axis=1, keepdims=True))
        alpha = jnp.exp(m - m_new)
        p = jnp.where(ok, jnp.exp(s - m_new), 0.0)
        l = alpha * l + jnp.sum(p, axis=1, keepdims=True)
        acc = alpha * acc + jnp.dot(p.astype(bf16), vw_ref[0, 0, pl.ds(off, Q_BLOCK), :],
                                    preferred_element_type=f32)
        carry = (m_new, l, acc)
    _, l_w, acc_w = carry
    o_w = acc_w / jnp.maximum(l_w, 1e-30)

    gt = gt_ref[0, 0, 0]
    o_ref[0, 0, 0] = gt[:, 0:1] * oc_ref[0, 0, 0] + gt[:, 1:2] * o_s + gt[:, 2:3] * o_w


def nsa_sel_win(qe, ke, vs, kw, vw, oc, gt):
    bsz, _, n_q = qe.shape[:3]
    t_len = ke.shape[2]
    rows = HPG * Q_BLOCK
    tile = lambda w: pl.BlockSpec((1, 1, 1, rows, w), lambda b, g, i: (b, g, i, 0, 0))
    full = lambda w: pl.BlockSpec((1, 1, t_len, w), lambda b, g, i: (b, g, 0, 0))
    return pl.pallas_call(
        _sel_win_kernel,
        grid=(bsz, G, n_q),
        in_specs=[tile(KEXT), full(KEXT), full(DH), full(DH), full(DH), tile(DH), tile(3)],
        out_specs=tile(DH),
        out_shape=jax.ShapeDtypeStruct((bsz, G, n_q, rows, DH), f32),
        compiler_params=pltpu.CompilerParams(dimension_semantics=("parallel", "parallel", "arbitrary"),
                                             vmem_limit_bytes=VMEM_LIMIT),
        name="nsa_sel_win",
    )(qe, ke, vs, kw, vw, oc, gt)


N_SUB = PAST_LEN // CMP_STRIDE
HALF = 2 * G * DH


def _gelu(x):
    return 0.5 * x * (1.0 + jnp.tanh(0.7978845608028654 * (x + 0.044715 * (x * x * x))))


def _sample_cmp_kernel(pt_ref, *refs):
    pages = refs[:N_PAGES]
    pe_ref, w1_ref, b1_ref, w2_ref, b2_ref, kg_ref, ck_ref, cv_ref, x_scr = refs[N_PAGES:]
    n_chunk = HALF // LANES
    for c in range(n_chunk):
        for j in range(N_PAGES):
            x_scr[c, pl.ds(j * PAGE_SIZE, PAGE_SIZE), :] = pages[j][0, :, pl.ds(c * LANES, LANES)]
        x_scr[c, pl.ds(PAST_LEN, CMP_STRIDE), :] = jnp.zeros((CMP_STRIDE, LANES), f32)
    for part in range(2):
        acc = jnp.zeros((G * N_SUB, CMP_HID), f32)
        for s in range(CMP_LEN):
            two = [x_scr[part * (G // 2) + c, pl.ds(s, N_SUB, stride=CMP_STRIDE), :] for c in range(G // 2)]
            x = jnp.concatenate([two[g // 2][:, (g % 2) * DH:(g % 2 + 1) * DH] for g in range(G)],
                                axis=0)
            x = x + pe_ref[part, s:s + 1, :]
            acc = acc + jnp.dot(x.astype(bf16), w1_ref[part, s], preferred_element_type=f32)
        hid = _gelu(acc + b1_ref[part:part + 1, :])
        out = jnp.dot(hid.astype(bf16), w2_ref[part], preferred_element_type=f32) + b2_ref[part:part + 1, :]
        if part == 0:
            out = out * lax.rsqrt(jnp.mean(out * out, axis=-1, keepdims=True) + NORM_EPS) * kg_ref[...]
            ck_ref[0] = out.reshape(G, N_SUB, DH).astype(bf16)
        else:
            cv_ref[0] = out.reshape(G, N_SUB, DH).astype(bf16)


def sample_compress(cache3, page_table, pe, w1, b1, w2, b2, kg):
    bsz = page_table.shape[0]

    def page_spec(j):
        return pl.BlockSpec((1, PAGE_SIZE, HALF), lambda b, pt: (pt[b, j], 0, 0))

    const = lambda shape: pl.BlockSpec(shape, lambda b, pt: (0,) * len(shape))
    out_spec = pl.BlockSpec((1, G, N_SUB, DH), lambda b, pt: (b, 0, 0, 0))
    grid_spec = pltpu.PrefetchScalarGridSpec(
        num_scalar_prefetch=1, grid=(bsz,),
        in_specs=[page_spec(j) for j in range(N_PAGES)] + [
            const((2, CMP_LEN, DH)), const((2, CMP_LEN, DH, CMP_HID)), const((2, CMP_HID)),
            const((2, CMP_HID, DH)), const((2, DH)), const((1, DH))],
        out_specs=[out_spec, out_spec],
        scratch_shapes=[pltpu.VMEM((HALF // LANES, PAST_LEN + CMP_STRIDE, LANES), f32)])
    return pl.pallas_call(
        _sample_cmp_kernel, grid_spec=grid_spec,
        out_shape=[jax.ShapeDtypeStruct((bsz, G, N_SUB, DH), bf16)] * 2,
        compiler_params=pltpu.CompilerParams(dimension_semantics=("arbitrary",), vmem_limit_bytes=VMEM_LIMIT),
        name="sample_compress",
    )(page_table, *([cache3] * N_PAGES), pe, w1.reshape(2, CMP_LEN, DH, CMP_HID).astype(bf16), b1,
      w2.astype(bf16), b2, kg)


T_NEW = 4
NEW_PAD = 128
NB_SAMPLE = -(-(PAST_LEN + T_NEW) // SEL_BLOCK)
N_CMP_SAMPLE = (PAST_LEN + T_NEW) // CMP_STRIDE - 1
ROWS_S = HPG * T_NEW


def _sample_attn_kernel(pt_ref, *refs):
    pages = refs[:N_PAGES]
    (q_ref, gt_ref, ck_ref, cv_ref, new_ref, win_ref, wnew_ref, impt_ref, oh_ref, ohn_ref, o_ref) = refs[N_PAGES:]
    t_of_row = lax.broadcasted_iota(jnp.int32, (ROWS_S, LANES), 0) & (T_NEW - 1)
    lane = lax.broadcasted_iota(jnp.int32, (ROWS_S, LANES), 1)
    q_pos = PAST_LEN + t_of_row
    hs_t = lax.broadcasted_iota(jnp.int32, (SUBLANES, ROWS_S), 0)
    hs_r = lax.broadcasted_iota(jnp.int32, (SUBLANES, ROWS_S), 1) & (T_NEW - 1)
    hsum = (hs_t == hs_r).astype(bf16)
    sp_r = lax.broadcasted_iota(jnp.int32, (ROWS_S, SUBLANES), 0) & (T_NEW - 1)
    sp_t = lax.broadcasted_iota(jnp.int32, (ROWS_S, SUBLANES), 1)
    spread = (sp_r == sp_t).astype(bf16)
    blk8 = lax.broadcasted_iota(jnp.int32, (SUBLANES, LANES), 1)
    qp8 = PAST_LEN + lax.broadcasted_iota(jnp.int32, (SUBLANES, LANES), 0)
    qb8 = qp8 >> 6
    forced = (blk8 == 0) | (blk8 == qb8) | (blk8 == qb8 - 1)
    causal = (blk8 * SEL_BLOCK <= qp8) & (blk8 < NB_SAMPLE)

    for g in range(G):
        q = q_ref[0, g]
        gt = gt_ref[0, g]
        s_c = _nt(q, ck_ref[0, g])
        ok_c = (lane * CMP_STRIDE + (CMP_LEN - 1) <= q_pos) & (lane < N_CMP_SAMPLE)
        p_c = _masked_softmax(s_c, ok_c)
        o_c = jnp.dot(p_c.astype(bf16), cv_ref[0, g], preferred_element_type=f32)
        hi, lo = _split_bf16(p_c)
        psum = jnp.dot(hsum, hi, preferred_element_type=f32) + jnp.dot(hsum, lo, preferred_element_type=f32)
        hi, lo = _split_bf16(psum)
        imp = _nt(hi, impt_ref[...]) + _nt(lo, impt_ref[...])
        score = jnp.where(blk8 < NB_SAMPLE,
                          jnp.where(causal, jnp.where(forced, FORCE_SCORE, imp), MASK_SCORE), PICKED)
        sel = _topk_rounds(score, blk8, 1) & causal
        bias8 = jnp.where(sel, 0.0, MASK_SCORE).astype(bf16)
        bias = jnp.dot(spread, bias8, preferred_element_type=f32).astype(bf16)
        sc = []
        for j in range(N_PAGES):
            kj = pages[j][0, :, pl.ds(g * DH, DH)].astype(bf16)
            sc.append(_nt(q, kj) + jnp.dot(bias, oh_ref[:, pl.ds(j * PAGE_SIZE, PAGE_SIZE)],
                                           preferred_element_type=f32))
        k_new = new_ref[0, :, pl.ds(g * DH, DH)].astype(bf16)
        s_new = _nt(q, k_new) + jnp.dot(bias, ohn_ref[...], preferred_element_type=f32)
        s_new = jnp.where(lane <= t_of_row, s_new, 2.0 * MASK_SCORE)
        sc.append(s_new)
        s_all = jnp.concatenate(sc, axis=1)
        m = jnp.max(s_all, axis=1, keepdims=True)
        e = jnp.exp(s_all - m)
        den = jnp.maximum(jnp.sum(e, axis=1, keepdims=True), 1e-30)
        eb = e.astype(bf16)
        acc = jnp.zeros((ROWS_S, DH), f32)
        for j in range(N_PAGES):
            vj = pages[j][0, :, pl.ds(G * DH + g * DH, DH)].astype(bf16)
            acc = acc + jnp.dot(eb[:, j * PAGE_SIZE:(j + 1) * PAGE_SIZE], vj, preferred_element_type=f32)
        v_new = new_ref[0, :, pl.ds(G * DH + g * DH, DH)].astype(bf16)
        acc = acc + jnp.dot(eb[:, PAST_LEN:], v_new, preferred_element_type=f32)
        o_s = acc / den
        n_buf = win_ref.shape[1]
        sw = []
        okw = []
        for c in range(n_buf // LANES):
            kc = win_ref[0, pl.ds(c * LANES, LANES), pl.ds(g * DH, DH)].astype(bf16)
            w_pos = PAST_LEN - n_buf + c * LANES + lane
            dist = q_pos - w_pos
            okw.append((dist >= 0) & (dist < WINDOW) & (w_pos >= 0))
            sw.append(_nt(q, kc))
        kn = wnew_ref[0, :, pl.ds(g * DH, DH)].astype(bf16)
        okw.append(lane <= t_of_row)
        sw.append(_nt(q, kn))
        ok_w = jnp.concatenate(okw, axis=1)
        p_w = _masked_softmax(jnp.concatenate(sw, axis=1), ok_w).astype(bf16)
        acc = jnp.zeros((ROWS_S, DH), f32)
        for c in range(n_buf // LANES):
            vc = win_ref[0, pl.ds(c * LANES, LANES), pl.ds(G * DH + g * DH, DH)].astype(bf16)
            acc = acc + jnp.dot(p_w[:, c * LANES:(c + 1) * LANES], vc, preferred_element_type=f32)
        vn = wnew_ref[0, :, pl.ds(G * DH + g * DH, DH)].astype(bf16)
        o_w = acc + jnp.dot(p_w[:, n_buf:], vn, preferred_element_type=f32)
        o_ref[0, g] = gt[:, 0:1] * o_c + gt[:, 1:2] * o_s + gt[:, 2:3] * o_w


def sample_attention(cache3, page_table, qh, gt, ck, cv, new_sel, win_cache, new_win):
    bsz = page_table.shape[0]
    n_buf = win_cache.shape[1]
    impt = _imp_map_t(N_SUB, LANES, N_CMP_SAMPLE, NB_SAMPLE)
    key_blk = np.arange(PAST_LEN) // SEL_BLOCK
    onehot = jnp.asarray(key_blk[None, :] == np.arange(LANES)[:, None], bf16)
    onehot_new = jnp.asarray((np.arange(LANES)[:, None] == PAST_LEN // SEL_BLOCK) &
                             (np.arange(NEW_PAD)[None, :] >= 0), bf16)

    def page_spec(j):
        return pl.BlockSpec((1, PAGE_SIZE, HALF), lambda b, pt: (pt[b, j], 0, 1))

    per_seq = lambda shape: pl.BlockSpec((1,) + shape, lambda b, pt: (b,) + (0,) * len(shape))
    const = lambda shape: pl.BlockSpec(shape, lambda b, pt: (0,) * len(shape))
    grid_spec = pltpu.PrefetchScalarGridSpec(
        num_scalar_prefetch=1, grid=(bsz,),
        in_specs=[page_spec(j) for j in range(N_PAGES)] + [
            per_seq((G, ROWS_S, DH)), per_seq((G, ROWS_S, 3)), per_seq((G, N_SUB, DH)), per_seq((G, N_SUB, DH)),
            per_seq((NEW_PAD, HALF)), per_seq((n_buf, HALF)), per_seq((NEW_PAD, HALF)),
            const((LANES, N_SUB)), const((LANES, PAST_LEN)), const((LANES, NEW_PAD))],
        out_specs=per_seq((G, ROWS_S, DH)))
    return pl.pallas_call(
        _sample_attn_kernel, grid_spec=grid_spec,
        out_shape=jax.ShapeDtypeStruct((bsz, G, ROWS_S, DH), f32),
        compiler_params=pltpu.CompilerParams(dimension_semantics=("arbitrary",), vmem_limit_bytes=VMEM_LIMIT),
        name="sample_attention",
    )(page_table, *([cache3] * N_PAGES), qh, gt, ck, cv, new_sel, win_cache, new_win, impt, onehot, onehot_new)


def rmsnorm(x, g):
    y = x * lax.rsqrt(jnp.mean(x * x, -1, keepdims=True) + NORM_EPS)
    return y * g


def to_heads(x):
    return x.reshape(x.shape[0], x.shape[1], RW_HEADS, RW_HEAD_DIM)


def conv_ffn(xn, conv_state, w_in, conv_w, conv_b, w_out):
    t_len = xn.shape[1]
    u, gate = jnp.split(mm3(xn, w_in), 2, axis=-1)
    u_pad = jnp.concatenate([conv_state, u], axis=1)
    uc = conv_b + conv_w[0] * u_pad[:, 0:t_len]
    for j in range(1, CONV_W):
        uc = uc + conv_w[j] * u_pad[:, j:j + t_len]
    return mm3(jax.nn.gelu(uc) * gate, w_out), u_pad[:, t_len:]


def rwkv7_mix(xn, shift_prev, s0, v_first, li, p, scan_cfg):
    bsz, t_len, _ = xn.shape
    xx = jnp.concatenate([shift_prev[:, None], xn[:, :-1]], axis=1) - xn
    mu = p['rw_mu'][li]
    xr, xw, xk, xv, xa, xg = (xn + xx * mu[i] for i in range(6))
    r = mm3(xr, p['rw_w_r'][li])
    k = mm3(xk, p['rw_w_k'][li])
    v = mm3(xv, p['rw_w_v'][li])
    w_raw = -jax.nn.softplus(-(p['rw_w0'][li] + mm3(jnp.tanh(mm3(xw, p['rw_w1'][li])), p['rw_w2'][li]))) - 0.5
    decay = jnp.exp(-jnp.exp(w_raw))
    a = jax.nn.sigmoid(p['rw_a0'][li] + mm3(mm3(xa, p['rw_a1'][li]), p['rw_a2'][li]))
    if li == 0:
        v_first = v
    else:
        v = v + (v_first - v) * jax.nn.sigmoid(
            p['rw_v0'][li - 1] + mm3(mm3(xv, p['rw_v1'][li - 1]), p['rw_v2'][li - 1]))
    g = mm3(jax.nn.sigmoid(mm3(xg, p['rw_g1'][li])), p['rw_g2'][li])
    kk = to_heads(k * p['rw_k_k'][li])
    kk = kk / jnp.maximum(jnp.sqrt(jnp.sum(kk * kk, -1, keepdims=True)), 1e-12)
    k = k * (1 + (a - 1) * p['rw_k_a'][li])
    kk = kk.reshape(bsz, t_len, D_MODEL)
    y, s_new = wkv_scan(r, decay, k, v, -kk, kk * a, to_pairs(s0), **scan_cfg)
    s_new = from_pairs(s_new, bsz)
    rh, kh, vh = to_heads(r), to_heads(k), to_heads(v)
    y = to_heads(y)
    mean = jnp.mean(y, -1, keepdims=True)
    var = jnp.mean(jnp.square(y - mean), -1, keepdims=True)
    yn = ((y - mean) * lax.rsqrt(var + RW_GN_EPS)).reshape(bsz, t_len, D_MODEL)
    yn = yn * p['rw_ln_w'][li] + p['rw_ln_b'][li]
    bonus = jnp.sum(rh * kh * p['rw_r_k'][li], -1, keepdims=True) * vh
    y = yn + bonus.reshape(bsz, t_len, D_MODEL)
    return mm3(y * g, p['rw_w_o'][li]), s_new, xn[:, -1], v_first


def kv_rows(h, p):
    bsz, t_len, _ = h.shape
    kv = mm3(rmsnorm(h, p['kv_norm_g']), p['w_kv']).reshape(bsz, t_len, 6, G, DH)
    k_sel = rmsnorm(kv[:, :, 2], p['k_norm_g'][1])
    k_win = rmsnorm(kv[:, :, 4], p['k_norm_g'][2])
    paged = jnp.stack([kv[:, :, 0], kv[:, :, 1], k_sel, kv[:, :, 3]], axis=2)
    win = jnp.stack([k_win, kv[:, :, 5]], axis=2)
    return paged, win


def nsa_query(xn, lb, p):
    bsz, t_len, _ = xn.shape
    n_q = NSA_HEADS * DH
    w = jnp.pad(p['nsa_w_q'][lb], ((0, 0), (0, LANES - 3 * NSA_HEADS)))
    proj = mm3(xn, w)
    q = proj[..., :n_q].reshape(bsz, t_len, G, HPG, DH)
    q = rmsnorm(q, p['q_norm_g'][lb]) * (DH ** -0.5)
    gates = jax.nn.sigmoid(proj[..., n_q:n_q + 3 * NSA_HEADS]).reshape(bsz, t_len, G, HPG, 3)
    return q, gates


def compress_rows_prompt(rows, pe, w1, b1, w2, b2):
    bsz, t_len = rows.shape[:2]
    n_sub = t_len // CMP_STRIDE
    sub = rows.reshape(bsz, n_sub, CMP_STRIDE, G, DH).transpose(0, 1, 3, 2, 4)
    half = CMP_STRIDE * DH
    x0 = (sub + pe[:CMP_STRIDE][None, None, None]).reshape(bsz * n_sub * G, half)
    x1 = (sub + pe[CMP_STRIDE:][None, None, None]).reshape(bsz * n_sub * G, half)
    p0 = mm(x0, w1[:half]).reshape(bsz, n_sub, G, CMP_HID)
    p1 = mm(x1, w1[half:]).reshape(bsz, n_sub, G, CMP_HID)
    hid = jax.nn.gelu(p0[:, :-1] + p1[:, 1:] + b1)
    out = mm(hid.reshape(bsz * (n_sub - 1) * G, CMP_HID), w2) + b2
    return out.reshape(bsz, n_sub - 1, G, DH)


def to_q_tiles(x, n_q):
    bsz, _, _, _, w = x.shape
    x = x.reshape(bsz, n_q, Q_BLOCK, G, HPG, w).transpose(0, 3, 1, 4, 2, 5)
    return x.reshape(bsz, G, n_q, HPG * Q_BLOCK, w)


def from_q_tiles(x, t_len):
    bsz, _, n_q, _, w = x.shape
    x = x.reshape(bsz, G, n_q, HPG, Q_BLOCK, w).transpose(0, 2, 4, 1, 3, 5)
    return x.reshape(bsz, t_len, G * HPG * w)


def prompt_context(h, p):
    paged, win = kv_rows(h, p)
    bsz, t_len = h.shape[:2]
    ck = rmsnorm(compress_rows_prompt(paged[:, :, 0], p['cmp_pe'][0], p['cmp_w1'][0], p['cmp_b1'][0],
                                      p['cmp_w2'][0], p['cmp_b2'][0]), p['k_norm_g'][0])
    cv = compress_rows_prompt(paged[:, :, 1], p['cmp_pe'][1], p['cmp_w1'][1], p['cmp_b1'][1],
                              p['cmp_w2'][1], p['cmp_b2'][1])
    n_cmp = ck.shape[1]
    pad = lambda c: jnp.pad(c, ((0, 0), (0, NC_PAD - n_cmp), (0, 0), (0, 0))).transpose(0, 2, 1, 3).astype(bf16)
    heads = lambda x: x.transpose(0, 2, 1, 3).astype(bf16)
    onehot = jnp.asarray(np.arange(t_len)[:, None] // SEL_BLOCK == np.arange(NB_PROMPT)[None, :], bf16)
    ke = jnp.concatenate([jnp.broadcast_to(onehot, (bsz, G, t_len, NB_PROMPT)), heads(paged[:, :, 2])], axis=-1)
    ctx = dict(ck=pad(ck), cv=pad(cv), n_cmp=n_cmp, ke=ke, vs=heads(paged[:, :, 3]),
               kw=heads(win[:, :, 0]), vw=heads(win[:, :, 1]))
    keep = min(WINDOW, t_len)
    return ctx, (paged, win[:, t_len - keep:])


def attend_prompt(q, gates, ctx):
    bsz, t_len = q.shape[:2]
    n_q = t_len // Q_BLOCK
    qh = to_q_tiles(q, n_q).astype(bf16)
    gt = to_q_tiles(gates, n_q)
    oc, sb = nsa_cmp_select(qh, ctx['ck'], ctx['cv'], ctx['n_cmp'])
    qe = jnp.concatenate([jnp.tile(sb, (1, 1, 1, HPG, 1)), qh], axis=-1)
    o = nsa_sel_win(qe, ctx['ke'], ctx['vs'], ctx['kw'], ctx['vw'], oc, gt)
    return from_q_tiles(o, t_len)


def sample_context(h, cache3, page_table, cache_kv_win, p):
    paged, win = kv_rows(h, p)
    bsz, t_new = h.shape[:2]
    ck, cv = sample_compress(cache3, page_table, p['cmp_pe'], p['cmp_w1'], p['cmp_b1'], p['cmp_w2'], p['cmp_b2'],
                             p['k_norm_g'][0:1])
    n_buf = cache_kv_win.shape[1]
    pad_rows = lambda x: jnp.pad(x.reshape(bsz, t_new, HALF), ((0, 0), (0, NEW_PAD - t_new), (0, 0)))
    ctx = dict(ck=ck, cv=cv, new_sel=pad_rows(paged[:, :, 2:4]), new_win=pad_rows(win),
               win_cache=cache_kv_win.reshape(bsz, n_buf, HALF))
    win_full = jnp.concatenate([cache_kv_win, win], axis=1)
    return ctx, (paged, win_full[:, win_full.shape[1] - n_buf:])


def attend_sample(q, gates, ctx, cache3, page_table):
    bsz, t_new = q.shape[:2]
    rows = lambda x: x.transpose(0, 2, 3, 1, 4).reshape(bsz, G, HPG * t_new, x.shape[-1])
    o = sample_attention(cache3, page_table, rows(q).astype(bf16), rows(gates), ctx['ck'], ctx['cv'],
                         ctx['new_sel'], ctx['win_cache'], ctx['new_win'])
    return o.reshape(bsz, G, HPG, t_new, DH).transpose(0, 3, 1, 2, 4).reshape(bsz, t_new, NSA_HEADS * DH)


def forward(x, wkv_in, shift_in, conv_in, make_ctx, attend, p, scan_cfg):
    wkv_out, shift_out, conv_out = [], [], []
    v_first = None
    ctx = None
    kv_state = None
    for layer in range(DEPTH):
        if layer == N_A_LAYERS:
            ctx, kv_state = make_ctx(x)
        xn = rmsnorm(x, p['norm_g'][layer, 0])
        if layer < N_A_LAYERS:
            mix, s_new, sh_new, v_first = rwkv7_mix(xn, shift_in[layer], wkv_in[layer], v_first, layer, p, scan_cfg)
            wkv_out.append(s_new)
            shift_out.append(sh_new)
        else:
            lb = layer - N_A_LAYERS
            q, gates = nsa_query(xn, lb, p)
            mix = mm3(attend(q, gates, ctx), p['nsa_w_o'][lb])
        x = x + mix
        xn = rmsnorm(x, p['norm_g'][layer, 1])
        f, conv_new = conv_ffn(xn, conv_in[layer], p['ffn_w_in'][layer], p['ffn_conv_w'][layer],
                               p['ffn_conv_b'][layer], p['ffn_w_out'][layer])
        conv_out.append(conv_new)
        x = x + f
    return x, kv_state, jnp.stack(wkv_out), jnp.stack(shift_out), jnp.stack(conv_out)


def kernel(x_prompt, x_sample, cache_kv, cache_kv_win, state_wkv, state_shift, state_ffn_conv, page_table, norm_g, rw_mu, rw_w_r, rw_w_k, rw_w_v, rw_w_o, rw_w0, rw_w1, rw_w2, rw_a0, rw_a1, rw_a2, rw_v0, rw_v1, rw_v2, rw_g1, rw_g2, rw_k_k, rw_k_a, rw_r_k, rw_ln_w, rw_ln_b, kv_norm_g, w_kv, k_norm_g, cmp_pe, cmp_w1, cmp_b1, cmp_w2, cmp_b2, nsa_w_q, q_norm_g, nsa_w_o, ffn_w_in, ffn_conv_w, ffn_conv_b, ffn_w_out):
    p = dict(norm_g=norm_g, rw_mu=rw_mu, rw_w_r=rw_w_r, rw_w_k=rw_w_k, rw_w_v=rw_w_v, rw_w_o=rw_w_o,
             rw_w0=rw_w0, rw_w1=rw_w1, rw_w2=rw_w2, rw_a0=rw_a0, rw_a1=rw_a1, rw_a2=rw_a2,
             rw_v0=rw_v0, rw_v1=rw_v1, rw_v2=rw_v2, rw_g1=rw_g1, rw_g2=rw_g2, rw_k_k=rw_k_k, rw_k_a=rw_k_a,
             rw_r_k=rw_r_k, rw_ln_w=rw_ln_w, rw_ln_b=rw_ln_b, kv_norm_g=kv_norm_g, w_kv=w_kv,
             k_norm_g=k_norm_g, cmp_pe=cmp_pe, cmp_w1=cmp_w1, cmp_b1=cmp_b1, cmp_w2=cmp_w2, cmp_b2=cmp_b2,
             nsa_w_q=nsa_w_q, q_norm_g=q_norm_g, nsa_w_o=nsa_w_o, ffn_w_in=ffn_w_in,
             ffn_conv_w=ffn_conv_w, ffn_conv_b=ffn_conv_b, ffn_w_out=ffn_w_out)
    bp, t_p = x_prompt.shape[:2]
    bs = x_sample.shape[0]

    wkv0 = jnp.zeros((N_A_LAYERS, bp, RW_HEADS, RW_HEAD_DIM, RW_HEAD_DIM), f32)
    shift0 = jnp.zeros((N_A_LAYERS, bp, D_MODEL), f32)
    conv0 = jnp.zeros((DEPTH, bp, CONV_W - 1, D_FF), f32)
    y_prompt, (kv_prompt, win_prompt), wkv_prompt, shift_prompt, conv_prompt = forward(
        x_prompt, wkv0, shift0, conv0, lambda h: prompt_context(h, p), attend_prompt, p,
        dict(bb=bp, tc=128))

    cache3 = cache_kv.reshape(cache_kv.shape[0], PAGE_SIZE, 4 * G * DH)
    y_sample, (kv_sample, win_sample), wkv_sample, shift_sample, conv_sample = forward(
        x_sample, state_wkv, state_shift, state_ffn_conv,
        lambda h: sample_context(h, cache3, page_table, cache_kv_win, p),
        lambda q, gates, ctx: attend_sample(q, gates, ctx, cache3, page_table), p,
        dict(bb=2, tc=x_sample.shape[1]))

    return (y_prompt, y_sample, kv_prompt, kv_sample, win_prompt, win_sample,
            wkv_prompt, wkv_sample, shift_prompt, shift_sample, conv_prompt, conv_sample)
```

```python
import functools

import numpy as np
import jax
import jax.numpy as jnp
from jax import lax
from jax.experimental import pallas as pl
from jax.experimental.pallas import tpu as pltpu

f32 = jnp.float32
bf16 = jnp.bfloat16

D_MODEL = 1024
DEPTH = 4
N_A_LAYERS = 2
PAST_LEN = 2048
PAGE_SIZE = 128
N_PAGES = PAST_LEN // PAGE_SIZE
RW_HEAD_DIM = 64
RW_HEADS = 16
RW_GN_EPS = 64e-5
DH = 64
NSA_HEADS = 16
G = 4
HPG = 4
CMP_LEN = 32
CMP_STRIDE = 16
CMP_HID = 128
SEL_BLOCK = 64
N_SELECT = 16
WINDOW = 512
Q_BLOCK = 128
D_FF = 2816
CONV_W = 3
NORM_EPS = 1e-6
NEG_INF = -1e30
FORCE_SCORE = 1e9
MASK_SCORE = -1e9
PICKED = -3e38

LANES = 128
SUBLANES = 8
VMEM_LIMIT = 48 * 1024 * 1024


def _nt(a, b):
    return lax.dot_general(a, b, (((1,), (1,)), ((), ())), preferred_element_type=f32)


def _split_bf16(x):
    hi = x.astype(bf16)
    lo = (x - hi.astype(f32)).astype(bf16)
    return hi, lo


def _mm_kernel(x_ref, w_ref, o_ref):
    o_ref[...] = jnp.dot(x_ref[...].astype(bf16), w_ref[...], preferred_element_type=f32)


MM_TM = 512


def _pick_tile(n, cands):
    for c in cands:
        if n % c == 0:
            return c
    return n


def mm(x, w):
    m, k = x.shape
    n = w.shape[1]
    w = w.astype(bf16)
    tm = MM_TM if m >= MM_TM else -(-m // SUBLANES) * SUBLANES
    mp = -(-m // tm) * tm
    if mp != m:
        x = jnp.pad(x, ((0, mp - m), (0, 0)))
    tn = n if n <= 1536 else _pick_tile(n, (1408, 1024, 512, 256, 128))
    out = pl.pallas_call(
        _mm_kernel,
        grid=(mp // tm, n // tn),
        in_specs=[pl.BlockSpec((tm, k), lambda i, j: (i, 0)),
                  pl.BlockSpec((k, tn), lambda i, j: (0, j))],
        out_specs=pl.BlockSpec((tm, tn), lambda i, j: (i, j)),
        out_shape=jax.ShapeDtypeStruct((mp, n), f32),
        compiler_params=pltpu.CompilerParams(dimension_semantics=("parallel", "parallel"),
                                             vmem_limit_bytes=VMEM_LIMIT),
        name="mm",
    )(x, w)
    return out[:m] if mp != m else out


def mm3(x, w):
    b, t, k = x.shape
    return mm(x.reshape(b * t, k), w).reshape(b, t, w.shape[1])


PAIR_W = 2 * RW_HEAD_DIM
N_PAIRS = RW_HEADS // 2


def _scan_kernel(r_ref, w_ref, k_ref, v_ref, a_ref, b_ref, s0_ref, y_ref, s_ref, *, bb, tc):
    n = RW_HEAD_DIM
    npairs = bb * N_PAIRS

    @pl.when(pl.program_id(1) == 0)
    def _():
        s_ref[...] = s0_ref[...]

    row = lax.broadcasted_iota(jnp.int32, (n, PAIR_W), 0)
    lane = lax.broadcasted_iota(jnp.int32, (n, PAIR_W), 1)
    diag = (lane % n) == row
    kk = lax.broadcasted_iota(jnp.int32, (PAIR_W, PAIR_W), 0) // n
    ll = lax.broadcasted_iota(jnp.int32, (PAIR_W, PAIR_W), 1) // n
    bones = (kk == ll).astype(bf16)
    group = min(tc, SUBLANES)

    def seg_sum_bcast(x):
        return jnp.dot(x.astype(bf16), bones, preferred_element_type=f32)

    def token(t0, s):
        def rowvec(ref, bi, hp):
            tile = ref[bi, pl.ds(t0, group), pl.ds(hp * PAIR_W, PAIR_W)]
            return jnp.broadcast_to(tile[s:s + 1], (n, PAIR_W))

        pairs = [(bi, hp) for bi in range(bb) for hp in range(N_PAIRS)]
        st = [s_ref[p] for p in range(npairs)]
        sa_in = jnp.concatenate([st[p] * rowvec(a_ref, *pairs[p]) for p in range(npairs)], axis=0)
        v_in = jnp.concatenate([jnp.where(diag, rowvec(v_ref, *pairs[p]), 0.0) for p in range(npairs)], axis=0)
        sa_col = seg_sum_bcast(sa_in)
        v_col = seg_sum_bcast(v_in)
        new = []
        for p in range(npairs):
            sl = slice(p * n, (p + 1) * n)
            sn = (st[p] * rowvec(w_ref, *pairs[p]) + sa_col[sl] * rowvec(b_ref, *pairs[p])
                  + v_col[sl] * rowvec(k_ref, *pairs[p]))
            s_ref[p] = sn
            new.append(sn)
        y_in = jnp.concatenate([new[p] * rowvec(r_ref, *pairs[p]) for p in range(npairs)], axis=0)
        y_col = seg_sum_bcast(y_in)
        rows = []
        for p in range(npairs):
            sl = slice(p * n, (p + 1) * n)
            rows.append(jnp.sum(jnp.where(diag, y_col[sl], 0.0), axis=0, keepdims=True))
        return rows

    def block(g, carry):
        t0 = pl.multiple_of(g * group, group)
        rows = [token(t0, s) for s in range(group)]
        for p in range(npairs):
            bi, hp = divmod(p, N_PAIRS)
            y_ref[bi, pl.ds(t0, group), pl.ds(hp * PAIR_W, PAIR_W)] = jnp.concatenate(
                [rows[s][p] for s in range(group)], axis=0)
        return carry

    lax.fori_loop(0, tc // group, block, 0)


def wkv_scan(r, w, k, v, a, b, s0_pairs, *, bb, tc):
    bsz, t_len, _ = r.shape
    assert bsz % bb == 0 and t_len % tc == 0
    vec_spec = pl.BlockSpec((bb, tc, D_MODEL), lambda i, j: (i, j, 0))
    st_spec = pl.BlockSpec((bb * N_PAIRS, RW_HEAD_DIM, PAIR_W), lambda i, j: (i, 0, 0))
    return pl.pallas_call(
        functools.partial(_scan_kernel, bb=bb, tc=tc),
        grid=(bsz // bb, t_len // tc),
        in_specs=[vec_spec] * 6 + [st_spec],
        out_specs=[vec_spec, st_spec],
        out_shape=[jax.ShapeDtypeStruct((bsz, t_len, D_MODEL), f32),
                   jax.ShapeDtypeStruct((bsz * N_PAIRS, RW_HEAD_DIM, PAIR_W), f32)],
        compiler_params=pltpu.CompilerParams(dimension_semantics=("arbitrary", "arbitrary"),
                                             vmem_limit_bytes=VMEM_LIMIT),
        name="wkv_scan",
    )(r, w, k, v, a, b, s0_pairs)


def to_pairs(s):
    bsz = s.shape[0]
    return s.reshape(bsz, N_PAIRS, 2, RW_HEAD_DIM, RW_HEAD_DIM).transpose(0, 1, 3, 2, 4).reshape(
        bsz * N_PAIRS, RW_HEAD_DIM, PAIR_W)


def from_pairs(s, bsz):
    return s.reshape(bsz, N_PAIRS, RW_HEAD_DIM, 2, RW_HEAD_DIM).transpose(0, 1, 3, 2, 4).reshape(
        bsz, RW_HEADS, RW_HEAD_DIM, RW_HEAD_DIM)


def _masked_softmax(s, mask):
    s = jnp.where(mask, s, NEG_INF)
    e = jnp.where(mask, jnp.exp(s - jnp.max(s, axis=-1, keepdims=True)), 0.0)
    return e / jnp.maximum(jnp.sum(e, axis=-1, keepdims=True), 1e-30)


def _topk_rounds(score, idx, axis):
    sel = jnp.zeros(score.shape, f32)
    idx = idx.astype(f32)
    for _ in range(N_SELECT):
        m = jnp.max(score, axis=axis, keepdims=True)
        first = jnp.min(jnp.where(score == m, idx, 1e6), axis=axis, keepdims=True)
        pick = idx == first
        sel = jnp.where(pick, 1.0, sel)
        score = jnp.where(pick, PICKED, score)
    return sel > 0.5


def _imp_map_t(n_cmp_pad, n_blk_pad, n_cmp, n_blk):
    ci = np.arange(n_cmp_pad)[None, :] * CMP_STRIDE
    sj = np.arange(n_blk_pad)[:, None] * SEL_BLOCK
    ov = np.clip(np.minimum(ci + CMP_LEN, sj + SEL_BLOCK) - np.maximum(ci, sj), 0, None) / CMP_STRIDE
    ov = ov * (np.arange(n_cmp_pad)[None, :] < n_cmp) * (np.arange(n_blk_pad)[:, None] < n_blk)
    return jnp.asarray(ov, bf16)


NC_PAD = 512
NB_PROMPT = 128


def _cmp_select_kernel(q_ref, ck_ref, cv_ref, impt_ref, oc_ref, sb_ref, *, n_cmp):
    qi = pl.program_id(2)
    q = q_ref[0, 0, 0]
    s = _nt(q, ck_ref[0, 0])
    rows = HPG * Q_BLOCK
    t_loc = lax.broadcasted_iota(jnp.int32, (rows, NC_PAD), 0) & (Q_BLOCK - 1)
    q_pos = qi * Q_BLOCK + t_loc
    nn = lax.broadcasted_iota(jnp.int32, (rows, NC_PAD), 1)
    valid = (nn * CMP_STRIDE + (CMP_LEN - 1) <= q_pos) & (nn < n_cmp)
    p = _masked_softmax(s, valid)
    oc_ref[0, 0, 0] = jnp.dot(p.astype(bf16), cv_ref[0, 0], preferred_element_type=f32)
    psum = p[0:Q_BLOCK]
    for h in range(1, HPG):
        psum = psum + p[h * Q_BLOCK:(h + 1) * Q_BLOCK]
    hi, lo = _split_bf16(psum)
    imp_t = _nt(impt_ref[...], hi) + _nt(impt_ref[...], lo)
    blk = lax.broadcasted_iota(jnp.int32, (NB_PROMPT, Q_BLOCK), 0)
    qp = qi * Q_BLOCK + lax.broadcasted_iota(jnp.int32, (NB_PROMPT, Q_BLOCK), 1)
    qb = qp >> 6
    forced = (blk == 0) | (blk == qb) | (blk == qb - 1)
    causal = blk * SEL_BLOCK <= qp
    score = jnp.where(causal, jnp.where(forced, FORCE_SCORE, imp_t), MASK_SCORE)
    sel = _topk_rounds(score, blk, 0) & causal
    bias_t = jnp.where(sel, 0.0, MASK_SCORE)
    sb_ref[0, 0, 0] = bias_t.T.astype(bf16)


def nsa_cmp_select(qh, ck, cv, n_cmp):
    bsz, _, n_q = qh.shape[:3]
    rows = HPG * Q_BLOCK
    impt = _imp_map_t(NC_PAD, NB_PROMPT, n_cmp, NB_PROMPT)
    return pl.pallas_call(
        functools.partial(_cmp_select_kernel, n_cmp=n_cmp),
        grid=(bsz, G, n_q),
        in_specs=[pl.BlockSpec((1, 1, 1, rows, DH), lambda b, g, i: (b, g, i, 0, 0)),
                  pl.BlockSpec((1, 1, NC_PAD, DH), lambda b, g, i: (b, g, 0, 0)),
                  pl.BlockSpec((1, 1, NC_PAD, DH), lambda b, g, i: (b, g, 0, 0)),
                  pl.BlockSpec((NB_PROMPT, NC_PAD), lambda b, g, i: (0, 0))],
        out_specs=[pl.BlockSpec((1, 1, 1, rows, DH), lambda b, g, i: (b, g, i, 0, 0)),
                   pl.BlockSpec((1, 1, 1, Q_BLOCK, NB_PROMPT), lambda b, g, i: (b, g, i, 0, 0))],
        out_shape=[jax.ShapeDtypeStruct((bsz, G, n_q, rows, DH), f32),
                   jax.ShapeDtypeStruct((bsz, G, n_q, Q_BLOCK, NB_PROMPT), bf16)],
        compiler_params=pltpu.CompilerParams(dimension_semantics=("parallel", "parallel", "parallel"),
                                             vmem_limit_bytes=VMEM_LIMIT),
        name="nsa_cmp_select",
    )(qh, ck, cv, impt)


KEXT = NB_PROMPT + DH
SEL_KT = 4 * Q_BLOCK


def _sel_win_kernel(qe_ref, ke_ref, vst_ref, kw_ref, vwt_ref, oc_ref, gt_ref, o_ref, s_scr, p_scr):
    qi = pl.program_id(2)
    rows = HPG * Q_BLOCK
    qe = qe_ref[0, 0, 0]
    q = qe[:, NB_PROMPT:]
    t_loc = lax.broadcasted_iota(jnp.int32, (Q_BLOCK, rows), 1) & (Q_BLOCK - 1)
    krow = lax.broadcasted_iota(jnp.int32, (Q_BLOCK, rows), 0)
    q_pos = qi * Q_BLOCK + t_loc

    def flash_step(s, ok, vt, carry):
        m, l, acc = carry
        m_new = jnp.maximum(m, jnp.max(s, axis=0, keepdims=True))
        alpha = jnp.exp(m - m_new)
        p = jnp.exp(s - m_new)
        if ok is not None:
            p = jnp.where(ok, p, 0.0)
        l = alpha * l + jnp.sum(p, axis=0, keepdims=True)
        acc = alpha * acc + jnp.dot(vt, p.astype(bf16), preferred_element_type=f32)
        return m_new, l, acc

    init = (jnp.full((1, rows), NEG_INF, f32), jnp.zeros((1, rows), f32), jnp.zeros((DH, rows), f32))

    def sel_scores(off):
        return _nt(ke_ref[0, 0, pl.ds(off, SEL_KT), :], qe)

    def pv(j):
        off = pl.multiple_of(jnp.maximum(j, 0) * SEL_KT, SEL_KT)
        return jnp.dot(vst_ref[0, 0, :, pl.ds(off, SEL_KT)], p_scr[...], preferred_element_type=f32)

    def softmax_step(s, m, l):
        m_new = jnp.maximum(m, jnp.max(s, axis=0, keepdims=True))
        alpha = jnp.exp(m - m_new)
        p = jnp.exp(s - m_new)
        return m_new, alpha * l + jnp.sum(p, axis=0, keepdims=True), alpha, p.astype(bf16)

    n_full = (qi * Q_BLOCK) // SEL_KT
    s_scr[...] = sel_scores(0)
    p_scr[...] = jnp.zeros((SEL_KT, rows), bf16)

    def sel_body(j, carry):
        m, l, acc, alpha_prev = carry
        s_next = sel_scores(pl.multiple_of((j + 1) * SEL_KT, SEL_KT))
        acc = alpha_prev * acc + pv(j - 1)
        m, l, alpha, p = softmax_step(s_scr[...], m, l)
        p_scr[...] = p
        s_scr[...] = s_next
        return m, l, acc, alpha

    m, l, acc, alpha_prev = lax.fori_loop(0, n_full, sel_body, init + (jnp.ones((1, rows), f32),))
    acc = alpha_prev * acc + pv(n_full - 1)
    off = pl.multiple_of(n_full * SEL_KT, SEL_KT)
    key_pos = off + lax.broadcasted_iota(jnp.int32, (SEL_KT, rows), 0)
    t_row = lax.broadcasted_iota(jnp.int32, (SEL_KT, rows), 1) & (Q_BLOCK - 1)
    s = jnp.where(key_pos <= qi * Q_BLOCK + t_row, s_scr[...], 2.0 * MASK_SCORE)
    _, l_s, alpha, p = softmax_step(s, m, l)
    acc_s = alpha * acc + jnp.dot(vst_ref[0, 0, :, pl.ds(off, SEL_KT)], p, preferred_element_type=f32)
    o_s = (acc_s / jnp.maximum(l_s, 1e-30)).T

    carry = init
    n_wt = WINDOW // Q_BLOCK + 1
    for j in range(n_wt):
        kt = qi - (n_wt - 1) + j
        ktc = jnp.maximum(kt, 0)
        off = pl.multiple_of(ktc * Q_BLOCK, Q_BLOCK)
        dist = q_pos - (ktc * Q_BLOCK + krow)
        ok = (dist >= 0) & (dist < WINDOW) & (kt >= 0)
        s = jnp.where(ok, _nt(kw_ref[0, 0, pl.ds(off, Q_BLOCK), :], q), NEG_INF)
        carry = flash_step(s, ok, vwt_ref[0, 0, :, pl.ds(off, Q_BLOCK)], carry)
    _, l_w, acc_w = carry
    o_w = (acc_w / jnp.maximum(l_w, 1e-30)).T

    gt = gt_ref[0, 0, 0]
    o_ref[0, 0, 0] = gt[:, 0:1] * oc_ref[0, 0, 0] + gt[:, 1:2] * o_s + gt[:, 2:3] * o_w


def nsa_sel_win(qe, ke, vst, kw, vwt, oc, gt):
    bsz, _, n_q = qe.shape[:3]
    t_len = ke.shape[2]
    rows = HPG * Q_BLOCK
    tile = lambda w: pl.BlockSpec((1, 1, 1, rows, w), lambda b, g, i: (b, g, i, 0, 0))
    full = lambda w: pl.BlockSpec((1, 1, t_len, w), lambda b, g, i: (b, g, 0, 0))
    full_t = pl.BlockSpec((1, 1, DH, t_len), lambda b, g, i: (b, g, 0, 0))
    return pl.pallas_call(
        _sel_win_kernel,
        grid=(bsz, G, n_q),
        in_specs=[tile(KEXT), full(KEXT), full_t, full(DH), full_t, tile(DH), tile(3)],
        out_specs=tile(DH),
        out_shape=jax.ShapeDtypeStruct((bsz, G, n_q, rows, DH), f32),
        scratch_shapes=[pltpu.VMEM((SEL_KT, rows), f32), pltpu.VMEM((SEL_KT, rows), bf16)],
        compiler_params=pltpu.CompilerParams(dimension_semantics=("parallel", "parallel", "arbitrary"),
                                             vmem_limit_bytes=VMEM_LIMIT),
        name="nsa_sel_win",
    )(qe, ke, vst, kw, vwt, oc, gt)


N_SUB = PAST_LEN // CMP_STRIDE
HALF = 2 * G * DH


def _gelu(x):
    return 0.5 * x * (1.0 + jnp.tanh(0.7978845608028654 * (x + 0.044715 * (x * x * x))))


def _sample_cmp_kernel(pt_ref, *refs):
    pages = refs[:N_PAGES]
    pe_ref, w1_ref, b1_ref, w2_ref, b2_ref, kg_ref, ck_ref, cv_ref, x_scr = refs[N_PAGES:]
    n_chunk = HALF // LANES
    for c in range(n_chunk):
        for j in range(N_PAGES):
            x_scr[c, pl.ds(j * PAGE_SIZE, PAGE_SIZE), :] = pages[j][0, :, pl.ds(c * LANES, LANES)]
        x_scr[c, pl.ds(PAST_LEN, CMP_STRIDE), :] = jnp.zeros((CMP_STRIDE, LANES), f32)
    for part in range(2):
        acc = jnp.zeros((G * N_SUB, CMP_HID), f32)
        for s in range(CMP_LEN):
            two = [x_scr[part * (G // 2) + c, pl.ds(s, N_SUB, stride=CMP_STRIDE), :] for c in range(G // 2)]
            x = jnp.concatenate([two[g // 2][:, (g % 2) * DH:(g % 2 + 1) * DH] for g in range(G)],
                                axis=0)
            x = x + pe_ref[part, s:s + 1, :]
            acc = acc + jnp.dot(x.astype(bf16), w1_ref[part, s], preferred_element_type=f32)
        hid = _gelu(acc + b1_ref[part:part + 1, :])
        out = jnp.dot(hid.astype(bf16), w2_ref[part], preferred_element_type=f32) + b2_ref[part:part + 1, :]
        if part == 0:
            out = out * lax.rsqrt(jnp.mean(out * out, axis=-1, keepdims=True) + NORM_EPS) * kg_ref[...]
            ck_ref[0] = out.reshape(G, N_SUB, DH).astype(bf16)
        else:
            cv_ref[0] = out.reshape(G, N_SUB, DH).astype(bf16)


def sample_compress(cache3, page_table, pe, w1, b1, w2, b2, kg):
    bsz = page_table.shape[0]

    def page_spec(j):
        return pl.BlockSpec((1, PAGE_SIZE, HALF), lambda b, pt: (pt[b, j], 0, 0))

    const = lambda shape: pl.BlockSpec(shape, lambda b, pt: (0,) * len(shape))
    out_spec = pl.BlockSpec((1, G, N_SUB, DH), lambda b, pt: (b, 0, 0, 0))
    grid_spec = pltpu.PrefetchScalarGridSpec(
        num_scalar_prefetch=1, grid=(bsz,),
        in_specs=[page_spec(j) for j in range(N_PAGES)] + [
            const((2, CMP_LEN, DH)), const((2, CMP_LEN, DH, CMP_HID)), const((2, CMP_HID)),
            const((2, CMP_HID, DH)), const((2, DH)), const((1, DH))],
        out_specs=[out_spec, out_spec],
        scratch_shapes=[pltpu.VMEM((HALF // LANES, PAST_LEN + CMP_STRIDE, LANES), f32)])
    return pl.pallas_call(
        _sample_cmp_kernel, grid_spec=grid_spec,
        out_shape=[jax.ShapeDtypeStruct((bsz, G, N_SUB, DH), bf16)] * 2,
        compiler_params=pltpu.CompilerParams(dimension_semantics=("arbitrary",), vmem_limit_bytes=VMEM_LIMIT),
        name="sample_compress",
    )(page_table, *([cache3] * N_PAGES), pe, w1.reshape(2, CMP_LEN, DH, CMP_HID).astype(bf16), b1,
      w2.astype(bf16), b2, kg)


T_NEW = 4
NEW_PAD = 128
NB_SAMPLE = -(-(PAST_LEN + T_NEW) // SEL_BLOCK)
N_CMP_SAMPLE = (PAST_LEN + T_NEW) // CMP_STRIDE - 1
ROWS_S = HPG * T_NEW


def _sample_attn_kernel(pt_ref, *refs):
    pages = refs[:N_PAGES]
    (q_ref, gt_ref, ck_ref, cv_ref, new_ref, win_ref, wnew_ref, impt_ref, oh_ref, ohn_ref, o_ref) = refs[N_PAGES:]
    t_of_row = lax.broadcasted_iota(jnp.int32, (ROWS_S, LANES), 0) & (T_NEW - 1)
    lane = lax.broadcasted_iota(jnp.int32, (ROWS_S, LANES), 1)
    q_pos = PAST_LEN + t_of_row
    hs_t = lax.broadcasted_iota(jnp.int32, (SUBLANES, ROWS_S), 0)
    hs_r = lax.broadcasted_iota(jnp.int32, (SUBLANES, ROWS_S), 1) & (T_NEW - 1)
    hsum = (hs_t == hs_r).astype(bf16)
    sp_r = lax.broadcasted_iota(jnp.int32, (ROWS_S, SUBLANES), 0) & (T_NEW - 1)
    sp_t = lax.broadcasted_iota(jnp.int32, (ROWS_S, SUBLANES), 1)
    spread = (sp_r == sp_t).astype(bf16)
    blk8 = lax.broadcasted_iota(jnp.int32, (SUBLANES, LANES), 1)
    qp8 = PAST_LEN + lax.broadcasted_iota(jnp.int32, (SUBLANES, LANES), 0)
    qb8 = qp8 >> 6
    forced = (blk8 == 0) | (blk8 == qb8) | (blk8 == qb8 - 1)
    causal = (blk8 * SEL_BLOCK <= qp8) & (blk8 < NB_SAMPLE)

    for g in range(G):
        q = q_ref[0, g]
        gt = gt_ref[0, g]
        s_c = _nt(q, ck_ref[0, g])
        ok_c = (lane * CMP_STRIDE + (CMP_LEN - 1) <= q_pos) & (lane < N_CMP_SAMPLE)
        p_c = _masked_softmax(s_c, ok_c)
        o_c = jnp.dot(p_c.astype(bf16), cv_ref[0, g], preferred_element_type=f32)
        hi, lo = _split_bf16(p_c)
        psum = jnp.dot(hsum, hi, preferred_element_type=f32) + jnp.dot(hsum, lo, preferred_element_type=f32)
        hi, lo = _split_bf16(psum)
        imp = _nt(hi, impt_ref[...]) + _nt(lo, impt_ref[...])
        score = jnp.where(blk8 < NB_SAMPLE,
                          jnp.where(causal, jnp.where(forced, FORCE_SCORE, imp), MASK_SCORE), PICKED)
        sel = _topk_rounds(score, blk8, 1) & causal
        bias8 = jnp.where(sel, 0.0, MASK_SCORE).astype(bf16)
        bias = jnp.dot(spread, bias8, preferred_element_type=f32).astype(bf16)
        sc = []
        for j in range(N_PAGES):
            kj = pages[j][0, :, pl.ds(g * DH, DH)].astype(bf16)
            sc.append(_nt(q, kj) + jnp.dot(bias, oh_ref[:, pl.ds(j * PAGE_SIZE, PAGE_SIZE)],
                                           preferred_element_type=f32))
        k_new = new_ref[0, :, pl.ds(g * DH, DH)].astype(bf16)
        s_new = _nt(q, k_new) + jnp.dot(bias, ohn_ref[...], preferred_element_type=f32)
        s_new = jnp.where(lane <= t_of_row, s_new, 2.0 * MASK_SCORE)
        sc.append(s_new)
        s_all = jnp.concatenate(sc, axis=1)
        m = jnp.max(s_all, axis=1, keepdims=True)
        e = jnp.exp(s_all - m)
        den = jnp.maximum(jnp.sum(e, axis=1, keepdims=True), 1e-30)
        eb = e.astype(bf16)
        acc = jnp.zeros((ROWS_S, DH), f32)
        for j in range(N_PAGES):
            vj = pages[j][0, :, pl.ds(G * DH + g * DH, DH)].astype(bf16)
            acc = acc + jnp.dot(eb[:, j * PAGE_SIZE:(j + 1) * PAGE_SIZE], vj, preferred_element_type=f32)
        v_new = new_ref[0, :, pl.ds(G * DH + g * DH, DH)].astype(bf16)
        acc = acc + jnp.dot(eb[:, PAST_LEN:], v_new, preferred_element_type=f32)
        o_s = acc / den
        n_buf = win_ref.shape[1]
        sw = []
        okw = []
        for c in range(n_buf // LANES):
            kc = win_ref[0, pl.ds(c * LANES, LANES), pl.ds(g * DH, DH)].astype(bf16)
            w_pos = PAST_LEN - n_buf + c * LANES + lane
            dist = q_pos - w_pos
            okw.append((dist >= 0) & (dist < WINDOW) & (w_pos >= 0))
            sw.append(_nt(q, kc))
        kn = wnew_ref[0, :, pl.ds(g * DH, DH)].astype(bf16)
        okw.append(lane <= t_of_row)
        sw.append(_nt(q, kn))
        ok_w = jnp.concatenate(okw, axis=1)
        p_w = _masked_softmax(jnp.concatenate(sw, axis=1), ok_w).astype(bf16)
        acc = jnp.zeros((ROWS_S, DH), f32)
        for c in range(n_buf // LANES):
            vc = win_ref[0, pl.ds(c * LANES, LANES), pl.ds(G * DH + g * DH, DH)].astype(bf16)
            acc = acc + jnp.dot(p_w[:, c * LANES:(c + 1) * LANES], vc, preferred_element_type=f32)
        vn = wnew_ref[0, :, pl.ds(G * DH + g * DH, DH)].astype(bf16)
        o_w = acc + jnp.dot(p_w[:, n_buf:], vn, preferred_element_type=f32)
        o_ref[0, g] = gt[:, 0:1] * o_c + gt[:, 1:2] * o_s + gt[:, 2:3] * o_w


def sample_attention(cache3, page_table, qh, gt, ck, cv, new_sel, win_cache, new_win):
    bsz = page_table.shape[0]
    n_buf = win_cache.shape[1]
    impt = _imp_map_t(N_SUB, LANES, N_CMP_SAMPLE, NB_SAMPLE)
    key_blk = np.arange(PAST_LEN) // SEL_BLOCK
    onehot = jnp.asarray(key_blk[None, :] == np.arange(LANES)[:, None], bf16)
    onehot_new = jnp.asarray((np.arange(LANES)[:, None] == PAST_LEN // SEL_BLOCK) &
                             (np.arange(NEW_PAD)[None, :] >= 0), bf16)

    def page_spec(j):
        return pl.BlockSpec((1, PAGE_SIZE, HALF), lambda b, pt: (pt[b, j], 0, 1))

    per_seq = lambda shape: pl.BlockSpec((1,) + shape, lambda b, pt: (b,) + (0,) * len(shape))
    const = lambda shape: pl.BlockSpec(shape, lambda b, pt: (0,) * len(shape))
    grid_spec = pltpu.PrefetchScalarGridSpec(
        num_scalar_prefetch=1, grid=(bsz,),
        in_specs=[page_spec(j) for j in range(N_PAGES)] + [
            per_seq((G, ROWS_S, DH)), per_seq((G, ROWS_S, 3)), per_seq((G, N_SUB, DH)), per_seq((G, N_SUB, DH)),
            per_seq((NEW_PAD, HALF)), per_seq((n_buf, HALF)), per_seq((NEW_PAD, HALF)),
            const((LANES, N_SUB)), const((LANES, PAST_LEN)), const((LANES, NEW_PAD))],
        out_specs=per_seq((G, ROWS_S, DH)))
    return pl.pallas_call(
        _sample_attn_kernel, grid_spec=grid_spec,
        out_shape=jax.ShapeDtypeStruct((bsz, G, ROWS_S, DH), f32),
        compiler_params=pltpu.CompilerParams(dimension_semantics=("arbitrary",), vmem_limit_bytes=VMEM_LIMIT),
        name="sample_attention",
    )(page_table, *([cache3] * N_PAGES), qh, gt, ck, cv, new_sel, win_cache, new_win, impt, onehot, onehot_new)


def rmsnorm(x, g):
    y = x * lax.rsqrt(jnp.mean(x * x, -1, keepdims=True) + NORM_EPS)
    return y * g


def to_heads(x):
    return x.reshape(x.shape[0], x.shape[1], RW_HEADS, RW_HEAD_DIM)


def conv_ffn(xn, conv_state, w_in, conv_w, conv_b, w_out):
    t_len = xn.shape[1]
    u, gate = jnp.split(mm3(xn, w_in), 2, axis=-1)
    u_pad = jnp.concatenate([conv_state, u], axis=1)
    uc = conv_b + conv_w[0] * u_pad[:, 0:t_len]
    for j in range(1, CONV_W):
        uc = uc + conv_w[j] * u_pad[:, j:j + t_len]
    return mm3(jax.nn.gelu(uc) * gate, w_out), u_pad[:, t_len:]


def rwkv7_mix(xn, shift_prev, s0, v_first, li, p, scan_cfg):
    bsz, t_len, _ = xn.shape
    xx = jnp.concatenate([shift_prev[:, None], xn[:, :-1]], axis=1) - xn
    mu = p['rw_mu'][li]
    xr, xw, xk, xv, xa, xg = (xn + xx * mu[i] for i in range(6))
    r = mm3(xr, p['rw_w_r'][li])
    k = mm3(xk, p['rw_w_k'][li])
    v = mm3(xv, p['rw_w_v'][li])
    w_raw = -jax.nn.softplus(-(p['rw_w0'][li] + mm3(jnp.tanh(mm3(xw, p['rw_w1'][li])), p['rw_w2'][li]))) - 0.5
    decay = jnp.exp(-jnp.exp(w_raw))
    a = jax.nn.sigmoid(p['rw_a0'][li] + mm3(mm3(xa, p['rw_a1'][li]), p['rw_a2'][li]))
    if li == 0:
        v_first = v
    else:
        v = v + (v_first - v) * jax.nn.sigmoid(
            p['rw_v0'][li - 1] + mm3(mm3(xv, p['rw_v1'][li - 1]), p['rw_v2'][li - 1]))
    g = mm3(jax.nn.sigmoid(mm3(xg, p['rw_g1'][li])), p['rw_g2'][li])
    kk = to_heads(k * p['rw_k_k'][li])
    kk = kk / jnp.maximum(jnp.sqrt(jnp.sum(kk * kk, -1, keepdims=True)), 1e-12)
    k = k * (1 + (a - 1) * p['rw_k_a'][li])
    kk = kk.reshape(bsz, t_len, D_MODEL)
    y, s_new = wkv_scan(r, decay, k, v, -kk, kk * a, to_pairs(s0), **scan_cfg)
    s_new = from_pairs(s_new, bsz)
    rh, kh, vh = to_heads(r), to_heads(k), to_heads(v)
    y = to_heads(y)
    mean = jnp.mean(y, -1, keepdims=True)
    var = jnp.mean(jnp.square(y - mean), -1, keepdims=True)
    yn = ((y - mean) * lax.rsqrt(var + RW_GN_EPS)).reshape(bsz, t_len, D_MODEL)
    yn = yn * p['rw_ln_w'][li] + p['rw_ln_b'][li]
    bonus = jnp.sum(rh * kh * p['rw_r_k'][li], -1, keepdims=True) * vh
    y = yn + bonus.reshape(bsz, t_len, D_MODEL)
    return mm3(y * g, p['rw_w_o'][li]), s_new, xn[:, -1], v_first


def kv_rows(h, p):
    bsz, t_len, _ = h.shape
    kv = mm3(rmsnorm(h, p['kv_norm_g']), p['w_kv']).reshape(bsz, t_len, 6, G, DH)
    k_sel = rmsnorm(kv[:, :, 2], p['k_norm_g'][1])
    k_win = rmsnorm(kv[:, :, 4], p['k_norm_g'][2])
    paged = jnp.stack([kv[:, :, 0], kv[:, :, 1], k_sel, kv[:, :, 3]], axis=2)
    win = jnp.stack([k_win, kv[:, :, 5]], axis=2)
    return paged, win


def nsa_query(xn, lb, p):
    bsz, t_len, _ = xn.shape
    n_q = NSA_HEADS * DH
    w = jnp.pad(p['nsa_w_q'][lb], ((0, 0), (0, LANES - 3 * NSA_HEADS)))
    proj = mm3(xn, w)
    q = proj[..., :n_q].reshape(bsz, t_len, G, HPG, DH)
    q = rmsnorm(q, p['q_norm_g'][lb]) * (DH ** -0.5)
    gates = jax.nn.sigmoid(proj[..., n_q:n_q + 3 * NSA_HEADS]).reshape(bsz, t_len, G, HPG, 3)
    return q, gates


def compress_rows_prompt(rows, pe, w1, b1, w2, b2):
    bsz, t_len = rows.shape[:2]
    n_sub = t_len // CMP_STRIDE
    sub = rows.reshape(bsz, n_sub, CMP_STRIDE, G, DH).transpose(0, 1, 3, 2, 4)
    half = CMP_STRIDE * DH
    x0 = (sub + pe[:CMP_STRIDE][None, None, None]).reshape(bsz * n_sub * G, half)
    x1 = (sub + pe[CMP_STRIDE:][None, None, None]).reshape(bsz * n_sub * G, half)
    p0 = mm(x0, w1[:half]).reshape(bsz, n_sub, G, CMP_HID)
    p1 = mm(x1, w1[half:]).reshape(bsz, n_sub, G, CMP_HID)
    hid = jax.nn.gelu(p0[:, :-1] + p1[:, 1:] + b1)
    out = mm(hid.reshape(bsz * (n_sub - 1) * G, CMP_HID), w2) + b2
    return out.reshape(bsz, n_sub - 1, G, DH)


def to_q_tiles(x, n_q):
    bsz, _, _, _, w = x.shape
    x = x.reshape(bsz, n_q, Q_BLOCK, G, HPG, w).transpose(0, 3, 1, 4, 2, 5)
    return x.reshape(bsz, G, n_q, HPG * Q_BLOCK, w)


def from_q_tiles(x, t_len):
    bsz, _, n_q, _, w = x.shape
    x = x.reshape(bsz, G, n_q, HPG, Q_BLOCK, w).transpose(0, 2, 4, 1, 3, 5)
    return x.reshape(bsz, t_len, G * HPG * w)


def prompt_context(h, p):
    paged, win = kv_rows(h, p)
    bsz, t_len = h.shape[:2]
    ck = rmsnorm(compress_rows_prompt(paged[:, :, 0], p['cmp_pe'][0], p['cmp_w1'][0], p['cmp_b1'][0],
                                      p['cmp_w2'][0], p['cmp_b2'][0]), p['k_norm_g'][0])
    cv = compress_rows_prompt(paged[:, :, 1], p['cmp_pe'][1], p['cmp_w1'][1], p['cmp_b1'][1],
                              p['cmp_w2'][1], p['cmp_b2'][1])
    n_cmp = ck.shape[1]
    pad = lambda c: jnp.pad(c, ((0, 0), (0, NC_PAD - n_cmp), (0, 0), (0, 0))).transpose(0, 2, 1, 3).astype(bf16)
    heads = lambda x: x.transpose(0, 2, 1, 3).astype(bf16)
    onehot = jnp.asarray(np.arange(t_len)[:, None] // SEL_BLOCK == np.arange(NB_PROMPT)[None, :], bf16)
    ke = jnp.concatenate([jnp.broadcast_to(onehot, (bsz, G, t_len, NB_PROMPT)), heads(paged[:, :, 2])], axis=-1)
    heads_t = lambda x: x.transpose(0, 2, 3, 1).astype(bf16)
    ctx = dict(ck=pad(ck), cv=pad(cv), n_cmp=n_cmp, ke=ke, vst=heads_t(paged[:, :, 3]),
               kw=heads(win[:, :, 0]), vwt=heads_t(win[:, :, 1]))
    keep = min(WINDOW, t_len)
    return ctx, (paged, win[:, t_len - keep:])


def attend_prompt(q, gates, ctx):
    bsz, t_len = q.shape[:2]
    n_q = t_len // Q_BLOCK
    qh = to_q_tiles(q, n_q).astype(bf16)
    gt = to_q_tiles(gates, n_q)
    oc, sb = nsa_cmp_select(qh, ctx['ck'], ctx['cv'], ctx['n_cmp'])
    qe = jnp.concatenate([jnp.tile(sb, (1, 1, 1, HPG, 1)), qh], axis=-1)
    o = nsa_sel_win(qe, ctx['ke'], ctx['vst'], ctx['kw'], ctx['vwt'], oc, gt)
    return from_q_tiles(o, t_len)


def sample_context(h, cache3, page_table, cache_kv_win, p):
    paged, win = kv_rows(h, p)
    bsz, t_new = h.shape[:2]
    ck, cv = sample_compress(cache3, page_table, p['cmp_pe'], p['cmp_w1'], p['cmp_b1'], p['cmp_w2'], p['cmp_b2'],
                             p['k_norm_g'][0:1])
    n_buf = cache_kv_win.shape[1]
    pad_rows = lambda x: jnp.pad(x.reshape(bsz, t_new, HALF), ((0, 0), (0, NEW_PAD - t_new), (0, 0)))
    ctx = dict(ck=ck, cv=cv, new_sel=pad_rows(paged[:, :, 2:4]), new_win=pad_rows(win),
               win_cache=cache_kv_win.reshape(bsz, n_buf, HALF))
    win_full = jnp.concatenate([cache_kv_win, win], axis=1)
    return ctx, (paged, win_full[:, win_full.shape[1] - n_buf:])


def attend_sample(q, gates, ctx, cache3, page_table):
    bsz, t_new = q.shape[:2]
    rows = lambda x: x.transpose(0, 2, 3, 1, 4).reshape(bsz, G, HPG * t_new, x.shape[-1])
    o = sample_attention(cache3, page_table, rows(q).astype(bf16), rows(gates), ctx['ck'], ctx['cv'],
                         ctx['new_sel'], ctx['win_cache'], ctx['new_win'])
    return o.reshape(bsz, G, HPG, t_new, DH).transpose(0, 3, 1, 2, 4).reshape(bsz, t_new, NSA_HEADS * DH)


def forward(x, wkv_in, shift_in, conv_in, make_ctx, attend, p, scan_cfg):
    wkv_out, shift_out, conv_out = [], [], []
    v_first = None
    ctx = None
    kv_state = None
    for layer in range(DEPTH):
        if layer == N_A_LAYERS:
            ctx, kv_state = make_ctx(x)
        xn = rmsnorm(x, p['norm_g'][layer, 0])
        if layer < N_A_LAYERS:
            mix, s_new, sh_new, v_first = rwkv7_mix(xn, shift_in[layer], wkv_in[layer], v_first, layer, p, scan_cfg)
            wkv_out.append(s_new)
            shift_out.append(sh_new)
        else:
            lb = layer - N_A_LAYERS
            q, gates = nsa_query(xn, lb, p)
            mix = mm3(attend(q, gates, ctx), p['nsa_w_o'][lb])
        x = x + mix
        xn = rmsnorm(x, p['norm_g'][layer, 1])
        f, conv_new = conv_ffn(xn, conv_in[layer], p['ffn_w_in'][layer], p['ffn_conv_w'][layer],
                               p['ffn_conv_b'][layer], p['ffn_w_out'][layer])
        conv_out.append(conv_new)
        x = x + f
    return x, kv_state, jnp.stack(wkv_out), jnp.stack(shift_out), jnp.stack(conv_out)


def kernel(x_prompt, x_sample, cache_kv, cache_kv_win, state_wkv, state_shift, state_ffn_conv, page_table, norm_g, rw_mu, rw_w_r, rw_w_k, rw_w_v, rw_w_o, rw_w0, rw_w1, rw_w2, rw_a0, rw_a1, rw_a2, rw_v0, rw_v1, rw_v2, rw_g1, rw_g2, rw_k_k, rw_k_a, rw_r_k, rw_ln_w, rw_ln_b, kv_norm_g, w_kv, k_norm_g, cmp_pe, cmp_w1, cmp_b1, cmp_w2, cmp_b2, nsa_w_q, q_norm_g, nsa_w_o, ffn_w_in, ffn_conv_w, ffn_conv_b, ffn_w_out):
    p = dict(norm_g=norm_g, rw_mu=rw_mu, rw_w_r=rw_w_r, rw_w_k=rw_w_k, rw_w_v=rw_w_v, rw_w_o=rw_w_o,
             rw_w0=rw_w0, rw_w1=rw_w1, rw_w2=rw_w2, rw_a0=rw_a0, rw_a1=rw_a1, rw_a2=rw_a2,
             rw_v0=rw_v0, rw_v1=rw_v1, rw_v2=rw_v2, rw_g1=rw_g1, rw_g2=rw_g2, rw_k_k=rw_k_k, rw_k_a=rw_k_a,
             rw_r_k=rw_r_k, rw_ln_w=rw_ln_w, rw_ln_b=rw_ln_b, kv_norm_g=kv_norm_g, w_kv=w_kv,
             k_norm_g=k_norm_g, cmp_pe=cmp_pe, cmp_w1=cmp_w1, cmp_b1=cmp_b1, cmp_w2=cmp_w2, cmp_b2=cmp_b2,
             nsa_w_q=nsa_w_q, q_norm_g=q_norm_g, nsa_w_o=nsa_w_o, ffn_w_in=ffn_w_in,
             ffn_conv_w=ffn_conv_w, ffn_conv_b=ffn_conv_b, ffn_w_out=ffn_w_out)
    bp, t_p = x_prompt.shape[:2]
    bs = x_sample.shape[0]

    wkv0 = jnp.zeros((N_A_LAYERS, bp, RW_HEADS, RW_HEAD_DIM, RW_HEAD_DIM), f32)
    shift0 = jnp.zeros((N_A_LAYERS, bp, D_MODEL), f32)
    conv0 = jnp.zeros((DEPTH, bp, CONV_W - 1, D_FF), f32)
    y_prompt, (kv_prompt, win_prompt), wkv_prompt, shift_prompt, conv_prompt = forward(
        x_prompt, wkv0, shift0, conv0, lambda h: prompt_context(h, p), attend_prompt, p,
        dict(bb=bp, tc=128))

    cache3 = cache_kv.reshape(cache_kv.shape[0], PAGE_SIZE, 4 * G * DH)
    y_sample, (kv_sample, win_sample), wkv_sample, shift_sample, conv_sample = forward(
        x_sample, state_wkv, state_shift, state_ffn_conv,
        lambda h: sample_context(h, cache3, page_table, cache_kv_win, p),
        lambda q, gates, ctx: attend_sample(q, gates, ctx, cache3, page_table), p,
        dict(bb=2, tc=x_sample.shape[1]))

    return (y_prompt, y_sample, kv_prompt, kv_sample, win_prompt, win_sample,
            wkv_prompt, wkv_sample, shift_prompt, shift_sample, conv_prompt, conv_sample)
```

```python
import functools

import numpy as np
import jax
import jax.numpy as jnp
from jax import lax
from jax.experimental import pallas as pl
from jax.experimental.pallas import tpu as pltpu

f32 = jnp.float32
bf16 = jnp.bfloat16

D_MODEL = 1024
DEPTH = 4
N_A_LAYERS = 2
PAST_LEN = 2048
PAGE_SIZE = 128
N_PAGES = PAST_LEN // PAGE_SIZE
RW_HEAD_DIM = 64
RW_HEADS = 16
RW_GN_EPS = 64e-5
DH = 64
NSA_HEADS = 16
G = 4
HPG = 4
CMP_LEN = 32
CMP_STRIDE = 16
CMP_HID = 128
SEL_BLOCK = 64
N_SELECT = 16
WINDOW = 512
Q_BLOCK = 128
D_FF = 2816
CONV_W = 3
NORM_EPS = 1e-6
NEG_INF = -1e30
FORCE_SCORE = 1e9
MASK_SCORE = -1e9
PICKED = -3e38

LANES = 128
SUBLANES = 8
VMEM_LIMIT = 48 * 1024 * 1024


def _nt(a, b):
    return lax.dot_general(a, b, (((1,), (1,)), ((), ())), preferred_element_type=f32)


def _split_bf16(x):
    hi = x.astype(bf16)
    lo = (x - hi.astype(f32)).astype(bf16)
    return hi, lo


def _mm_kernel(x_ref, w_ref, o_ref):
    o_ref[...] = jnp.dot(x_ref[...].astype(bf16), w_ref[...], preferred_element_type=f32)


MM_TM = 512


def _pick_tile(n, cands):
    for c in cands:
        if n % c == 0:
            return c
    return n


def mm(x, w):
    m, k = x.shape
    n = w.shape[1]
    w = w.astype(bf16)
    tm = MM_TM if m >= MM_TM else -(-m // SUBLANES) * SUBLANES
    mp = -(-m // tm) * tm
    if mp != m:
        x = jnp.pad(x, ((0, mp - m), (0, 0)))
    tn = n if n <= 1536 else _pick_tile(n, (1408, 1024, 512, 256, 128))
    out = pl.pallas_call(
        _mm_kernel,
        grid=(mp // tm, n // tn),
        in_specs=[pl.BlockSpec((tm, k), lambda i, j: (i, 0)),
                  pl.BlockSpec((k, tn), lambda i, j: (0, j))],
        out_specs=pl.BlockSpec((tm, tn), lambda i, j: (i, j)),
        out_shape=jax.ShapeDtypeStruct((mp, n), f32),
        compiler_params=pltpu.CompilerParams(dimension_semantics=("parallel", "parallel"),
                                             vmem_limit_bytes=VMEM_LIMIT),
        name="mm",
    )(x, w)
    return out[:m] if mp != m else out


def mm3(x, w):
    b, t, k = x.shape
    return mm(x.reshape(b * t, k), w).reshape(b, t, w.shape[1])


PAIR_W = 2 * RW_HEAD_DIM
N_PAIRS = RW_HEADS // 2


def _scan_kernel(r_ref, w_ref, k_ref, v_ref, a_ref, b_ref, s0_ref, y_ref, s_ref, *, bb, tc):
    n = RW_HEAD_DIM
    npairs = bb * N_PAIRS

    @pl.when(pl.program_id(1) == 0)
    def _():
        s_ref[...] = s0_ref[...]

    row = lax.broadcasted_iota(jnp.int32, (n, PAIR_W), 0)
    lane = lax.broadcasted_iota(jnp.int32, (n, PAIR_W), 1)
    diag = (lane % n) == row
    kk = lax.broadcasted_iota(jnp.int32, (PAIR_W, PAIR_W), 0) // n
    ll = lax.broadcasted_iota(jnp.int32, (PAIR_W, PAIR_W), 1) // n
    bones = (kk == ll).astype(bf16)
    group = min(tc, SUBLANES)

    def seg_sum_bcast(x):
        return jnp.dot(x.astype(bf16), bones, preferred_element_type=f32)

    def token(t0, s):
        def rowvec(ref, bi, hp):
            tile = ref[bi, pl.ds(t0, group), pl.ds(hp * PAIR_W, PAIR_W)]
            return jnp.broadcast_to(tile[s:s + 1], (n, PAIR_W))

        pairs = [(bi, hp) for bi in range(bb) for hp in range(N_PAIRS)]
        st = [s_ref[p] for p in range(npairs)]
        sa_in = jnp.concatenate([st[p] * rowvec(a_ref, *pairs[p]) for p in range(npairs)], axis=0)
        v_in = jnp.concatenate([jnp.where(diag, rowvec(v_ref, *pairs[p]), 0.0) for p in range(npairs)], axis=0)
        sa_col = seg_sum_bcast(sa_in)
        v_col = seg_sum_bcast(v_in)
        new = []
        for p in range(npairs):
            sl = slice(p * n, (p + 1) * n)
            sn = (st[p] * rowvec(w_ref, *pairs[p]) + sa_col[sl] * rowvec(b_ref, *pairs[p])
                  + v_col[sl] * rowvec(k_ref, *pairs[p]))
            s_ref[p] = sn
            new.append(sn)
        y_in = jnp.concatenate([new[p] * rowvec(r_ref, *pairs[p]) for p in range(npairs)], axis=0)
        y_col = seg_sum_bcast(y_in)
        rows = []
        for p in range(npairs):
            sl = slice(p * n, (p + 1) * n)
            rows.append(jnp.sum(jnp.where(diag, y_col[sl], 0.0), axis=0, keepdims=True))
        return rows

    def block(g, carry):
        t0 = pl.multiple_of(g * group, group)
        rows = [token(t0, s) for s in range(group)]
        for p in range(npairs):
            bi, hp = divmod(p, N_PAIRS)
            y_ref[bi, pl.ds(t0, group), pl.ds(hp * PAIR_W, PAIR_W)] = jnp.concatenate(
                [rows[s][p] for s in range(group)], axis=0)
        return carry

    lax.fori_loop(0, tc // group, block, 0)


def wkv_scan(r, w, k, v, a, b, s0_pairs, *, bb, tc):
    bsz, t_len, _ = r.shape
    assert bsz % bb == 0 and t_len % tc == 0
    vec_spec = pl.BlockSpec((bb, tc, D_MODEL), lambda i, j: (i, j, 0))
    st_spec = pl.BlockSpec((bb * N_PAIRS, RW_HEAD_DIM, PAIR_W), lambda i, j: (i, 0, 0))
    return pl.pallas_call(
        functools.partial(_scan_kernel, bb=bb, tc=tc),
        grid=(bsz // bb, t_len // tc),
        in_specs=[vec_spec] * 6 + [st_spec],
        out_specs=[vec_spec, st_spec],
        out_shape=[jax.ShapeDtypeStruct((bsz, t_len, D_MODEL), f32),
                   jax.ShapeDtypeStruct((bsz * N_PAIRS, RW_HEAD_DIM, PAIR_W), f32)],
        compiler_params=pltpu.CompilerParams(dimension_semantics=("arbitrary", "arbitrary"),
                                             vmem_limit_bytes=VMEM_LIMIT),
        name="wkv_scan",
    )(r, w, k, v, a, b, s0_pairs)


def to_pairs(s):
    bsz = s.shape[0]
    return s.reshape(bsz, N_PAIRS, 2, RW_HEAD_DIM, RW_HEAD_DIM).transpose(0, 1, 3, 2, 4).reshape(
        bsz * N_PAIRS, RW_HEAD_DIM, PAIR_W)


def from_pairs(s, bsz):
    return s.reshape(bsz, N_PAIRS, RW_HEAD_DIM, 2, RW_HEAD_DIM).transpose(0, 1, 3, 2, 4).reshape(
        bsz, RW_HEADS, RW_HEAD_DIM, RW_HEAD_DIM)


def _masked_softmax(s, mask):
    s = jnp.where(mask, s, NEG_INF)
    e = jnp.where(mask, jnp.exp(s - jnp.max(s, axis=-1, keepdims=True)), 0.0)
    return e / jnp.maximum(jnp.sum(e, axis=-1, keepdims=True), 1e-30)


def _topk_rounds(score, idx, axis):
    sel = jnp.zeros(score.shape, f32)
    idx = idx.astype(f32)
    for _ in range(N_SELECT):
        m = jnp.max(score, axis=axis, keepdims=True)
        first = jnp.min(jnp.where(score == m, idx, 1e6), axis=axis, keepdims=True)
        pick = idx == first
        sel = jnp.where(pick, 1.0, sel)
        score = jnp.where(pick, PICKED, score)
    return sel > 0.5


def _topk_by_rank(score, idx, n_valid):
    rank = jnp.zeros(score.shape, f32)
    for i in range(n_valid):
        ci = score[:, i:i + 1]
        beats = (ci > score) | ((ci == score) & (idx > i))
        rank = rank + jnp.where(beats, 1.0, 0.0)
    return rank < (N_SELECT - 0.5)


def _imp_map_t(n_cmp_pad, n_blk_pad, n_cmp, n_blk):
    ci = np.arange(n_cmp_pad)[None, :] * CMP_STRIDE
    sj = np.arange(n_blk_pad)[:, None] * SEL_BLOCK
    ov = np.clip(np.minimum(ci + CMP_LEN, sj + SEL_BLOCK) - np.maximum(ci, sj), 0, None) / CMP_STRIDE
    ov = ov * (np.arange(n_cmp_pad)[None, :] < n_cmp) * (np.arange(n_blk_pad)[:, None] < n_blk)
    return jnp.asarray(ov, bf16)


NC_PAD = 512
NB_PROMPT = 128


def _cmp_select_kernel(q_ref, ck_ref, cv_ref, impt_ref, oc_ref, sb_ref, *, n_cmp):
    qi = pl.program_id(2)
    q = q_ref[0, 0, 0]
    s = _nt(q, ck_ref[0, 0])
    rows = HPG * Q_BLOCK
    t_loc = lax.broadcasted_iota(jnp.int32, (rows, NC_PAD), 0) & (Q_BLOCK - 1)
    q_pos = qi * Q_BLOCK + t_loc
    nn = lax.broadcasted_iota(jnp.int32, (rows, NC_PAD), 1)
    valid = (nn * CMP_STRIDE + (CMP_LEN - 1) <= q_pos) & (nn < n_cmp)
    p = _masked_softmax(s, valid)
    oc_ref[0, 0, 0] = jnp.dot(p.astype(bf16), cv_ref[0, 0], preferred_element_type=f32)
    psum = p[0:Q_BLOCK]
    for h in range(1, HPG):
        psum = psum + p[h * Q_BLOCK:(h + 1) * Q_BLOCK]
    hi, lo = _split_bf16(psum)
    imp_t = _nt(impt_ref[...], hi) + _nt(impt_ref[...], lo)
    blk = lax.broadcasted_iota(jnp.int32, (NB_PROMPT, Q_BLOCK), 0)
    qp = qi * Q_BLOCK + lax.broadcasted_iota(jnp.int32, (NB_PROMPT, Q_BLOCK), 1)
    qb = qp >> 6
    forced = (blk == 0) | (blk == qb) | (blk == qb - 1)
    causal = blk * SEL_BLOCK <= qp
    score = jnp.where(causal, jnp.where(forced, FORCE_SCORE, imp_t), MASK_SCORE)
    sel = _topk_rounds(score, blk, 0) & causal
    bias_t = jnp.where(sel, 0.0, MASK_SCORE)
    sb_ref[0, 0, 0] = bias_t.T.astype(bf16)


def nsa_cmp_select(qh, ck, cv, n_cmp):
    bsz, _, n_q = qh.shape[:3]
    rows = HPG * Q_BLOCK
    impt = _imp_map_t(NC_PAD, NB_PROMPT, n_cmp, NB_PROMPT)
    return pl.pallas_call(
        functools.partial(_cmp_select_kernel, n_cmp=n_cmp),
        grid=(bsz, G, n_q),
        in_specs=[pl.BlockSpec((1, 1, 1, rows, DH), lambda b, g, i: (b, g, i, 0, 0)),
                  pl.BlockSpec((1, 1, NC_PAD, DH), lambda b, g, i: (b, g, 0, 0)),
                  pl.BlockSpec((1, 1, NC_PAD, DH), lambda b, g, i: (b, g, 0, 0)),
                  pl.BlockSpec((NB_PROMPT, NC_PAD), lambda b, g, i: (0, 0))],
        out_specs=[pl.BlockSpec((1, 1, 1, rows, DH), lambda b, g, i: (b, g, i, 0, 0)),
                   pl.BlockSpec((1, 1, 1, Q_BLOCK, NB_PROMPT), lambda b, g, i: (b, g, i, 0, 0))],
        out_shape=[jax.ShapeDtypeStruct((bsz, G, n_q, rows, DH), f32),
                   jax.ShapeDtypeStruct((bsz, G, n_q, Q_BLOCK, NB_PROMPT), bf16)],
        compiler_params=pltpu.CompilerParams(dimension_semantics=("parallel", "parallel", "parallel"),
                                             vmem_limit_bytes=VMEM_LIMIT),
        name="nsa_cmp_select",
    )(qh, ck, cv, impt)


KEXT = NB_PROMPT + DH
SEL_KT = 4 * Q_BLOCK


def _sel_win_kernel(qe_ref, ke_ref, vst_ref, kw_ref, vwt_ref, oc_ref, gt_ref, o_ref, s_scr, p_scr):
    qi = pl.program_id(2)
    rows = HPG * Q_BLOCK
    qe = qe_ref[0, 0, 0]
    q = qe[:, NB_PROMPT:]
    t_loc = lax.broadcasted_iota(jnp.int32, (Q_BLOCK, rows), 1) & (Q_BLOCK - 1)
    krow = lax.broadcasted_iota(jnp.int32, (Q_BLOCK, rows), 0)
    q_pos = qi * Q_BLOCK + t_loc

    def flash_step(s, ok, vt, carry):
        m, l, acc = carry
        m_new = jnp.maximum(m, jnp.max(s, axis=0, keepdims=True))
        alpha = jnp.exp(m - m_new)
        p = jnp.exp(s - m_new)
        if ok is not None:
            p = jnp.where(ok, p, 0.0)
        l = alpha * l + jnp.sum(p, axis=0, keepdims=True)
        acc = alpha * acc + jnp.dot(vt, p.astype(bf16), preferred_element_type=f32)
        return m_new, l, acc

    init = (jnp.full((1, rows), NEG_INF, f32), jnp.zeros((1, rows), f32), jnp.zeros((DH, rows), f32))

    def sel_scores(off):
        return _nt(ke_ref[0, 0, pl.ds(off, SEL_KT), :], qe)

    def pv(j):
        off = pl.multiple_of(jnp.maximum(j, 0) * SEL_KT, SEL_KT)
        return jnp.dot(vst_ref[0, 0, :, pl.ds(off, SEL_KT)], p_scr[...], preferred_element_type=f32)

    def softmax_step(s, m, l):
        m_new = jnp.maximum(m, jnp.max(s, axis=0, keepdims=True))
        alpha = jnp.exp(m - m_new)
        p = jnp.exp(s - m_new)
        return m_new, alpha * l + jnp.sum(p, axis=0, keepdims=True), alpha, p.astype(bf16)

    n_full = (qi * Q_BLOCK) // SEL_KT
    s_scr[...] = sel_scores(0)
    p_scr[...] = jnp.zeros((SEL_KT, rows), bf16)

    def sel_body(j, carry):
        m, l, acc, alpha_prev = carry
        s_next = sel_scores(pl.multiple_of((j + 1) * SEL_KT, SEL_KT))
        acc = alpha_prev * acc + pv(j - 1)
        m, l, alpha, p = softmax_step(s_scr[...], m, l)
        p_scr[...] = p
        s_scr[...] = s_next
        return m, l, acc, alpha

    m, l, acc, alpha_prev = lax.fori_loop(0, n_full, sel_body, init + (jnp.ones((1, rows), f32),))
    acc = alpha_prev * acc + pv(n_full - 1)
    off = pl.multiple_of(n_full * SEL_KT, SEL_KT)
    key_pos = off + lax.broadcasted_iota(jnp.int32, (SEL_KT, rows), 0)
    t_row = lax.broadcasted_iota(jnp.int32, (SEL_KT, rows), 1) & (Q_BLOCK - 1)
    s = jnp.where(key_pos <= qi * Q_BLOCK + t_row, s_scr[...], 2.0 * MASK_SCORE)
    _, l_s, alpha, p = softmax_step(s, m, l)
    acc_s = alpha * acc + jnp.dot(vst_ref[0, 0, :, pl.ds(off, SEL_KT)], p, preferred_element_type=f32)
    o_s = (acc_s / jnp.maximum(l_s, 1e-30)).T

    carry = init
    n_wt = WINDOW // Q_BLOCK + 1
    for j in range(n_wt):
        kt = qi - (n_wt - 1) + j
        ktc = jnp.maximum(kt, 0)
        off = pl.multiple_of(ktc * Q_BLOCK, Q_BLOCK)
        dist = q_pos - (ktc * Q_BLOCK + krow)
        ok = (dist >= 0) & (dist < WINDOW) & (kt >= 0)
        s = jnp.where(ok, _nt(kw_ref[0, 0, pl.ds(off, Q_BLOCK), :], q), NEG_INF)
        carry = flash_step(s, ok, vwt_ref[0, 0, :, pl.ds(off, Q_BLOCK)], carry)
    _, l_w, acc_w = carry
    o_w = (acc_w / jnp.maximum(l_w, 1e-30)).T

    gt = gt_ref[0, 0, 0]
    o_ref[0, 0, 0] = gt[:, 0:1] * oc_ref[0, 0, 0] + gt[:, 1:2] * o_s + gt[:, 2:3] * o_w


def nsa_sel_win(qe, ke, vst, kw, vwt, oc, gt):
    bsz, _, n_q = qe.shape[:3]
    t_len = ke.shape[2]
    rows = HPG * Q_BLOCK
    tile = lambda w: pl.BlockSpec((1, 1, 1, rows, w), lambda b, g, i: (b, g, i, 0, 0))
    full = lambda w: pl.BlockSpec((1, 1, t_len, w), lambda b, g, i: (b, g, 0, 0))
    full_t = pl.BlockSpec((1, 1, DH, t_len), lambda b, g, i: (b, g, 0, 0))
    return pl.pallas_call(
        _sel_win_kernel,
        grid=(bsz, G, n_q),
        in_specs=[tile(KEXT), full(KEXT), full_t, full(DH), full_t, tile(DH), tile(3)],
        out_specs=tile(DH),
        out_shape=jax.ShapeDtypeStruct((bsz, G, n_q, rows, DH), f32),
        scratch_shapes=[pltpu.VMEM((SEL_KT, rows), f32), pltpu.VMEM((SEL_KT, rows), bf16)],
        compiler_params=pltpu.CompilerParams(dimension_semantics=("parallel", "parallel", "arbitrary"),
                                             vmem_limit_bytes=VMEM_LIMIT),
        name="nsa_sel_win",
    )(qe, ke, vst, kw, vwt, oc, gt)


N_SUB = PAST_LEN // CMP_STRIDE
HALF = 2 * G * DH


def _gelu(x):
    return 0.5 * x * (1.0 + jnp.tanh(0.7978845608028654 * (x + 0.044715 * (x * x * x))))


def _sample_cmp_kernel(pt_ref, *refs):
    pages = refs[:N_PAGES]
    pe_ref, w1_ref, b1_ref, w2_ref, b2_ref, kg_ref, ck_ref, cv_ref, x_scr = refs[N_PAGES:]
    n_chunk = HALF // LANES
    for c in range(n_chunk):
        for j in range(N_PAGES):
            x_scr[c, pl.ds(j * PAGE_SIZE, PAGE_SIZE), :] = pages[j][0, :, pl.ds(c * LANES, LANES)]
        x_scr[c, pl.ds(PAST_LEN, CMP_STRIDE), :] = jnp.zeros((CMP_STRIDE, LANES), f32)
    for part in range(2):
        acc = jnp.zeros((G * N_SUB, CMP_HID), f32)
        for s in range(CMP_LEN):
            two = [x_scr[part * (G // 2) + c, pl.ds(s, N_SUB, stride=CMP_STRIDE), :] for c in range(G // 2)]
            x = jnp.concatenate([two[g // 2][:, (g % 2) * DH:(g % 2 + 1) * DH] for g in range(G)],
                                axis=0)
            x = x + pe_ref[part, s:s + 1, :]
            acc = acc + jnp.dot(x.astype(bf16), w1_ref[part, s], preferred_element_type=f32)
        hid = _gelu(acc + b1_ref[part:part + 1, :])
        out = jnp.dot(hid.astype(bf16), w2_ref[part], preferred_element_type=f32) + b2_ref[part:part + 1, :]
        if part == 0:
            out = out * lax.rsqrt(jnp.mean(out * out, axis=-1, keepdims=True) + NORM_EPS) * kg_ref[...]
        out_ref = ck_ref if part == 0 else cv_ref
        for c in range(G // 2):
            out_ref[0, c] = jnp.concatenate([out[(2 * c) * N_SUB:(2 * c + 1) * N_SUB],
                                             out[(2 * c + 1) * N_SUB:(2 * c + 2) * N_SUB]], axis=1).astype(bf16)


def sample_compress(cache3, page_table, pe, w1, b1, w2, b2, kg):
    bsz = page_table.shape[0]

    def page_spec(j):
        return pl.BlockSpec((1, PAGE_SIZE, HALF), lambda b, pt: (pt[b, j], 0, 0))

    const = lambda shape: pl.BlockSpec(shape, lambda b, pt: (0,) * len(shape))
    out_spec = pl.BlockSpec((1, G // 2, N_SUB, LANES), lambda b, pt: (b, 0, 0, 0))
    grid_spec = pltpu.PrefetchScalarGridSpec(
        num_scalar_prefetch=1, grid=(bsz,),
        in_specs=[page_spec(j) for j in range(N_PAGES)] + [
            const((2, CMP_LEN, DH)), const((2, CMP_LEN, DH, CMP_HID)), const((2, CMP_HID)),
            const((2, CMP_HID, DH)), const((2, DH)), const((1, DH))],
        out_specs=[out_spec, out_spec],
        scratch_shapes=[pltpu.VMEM((HALF // LANES, PAST_LEN + CMP_STRIDE, LANES), f32)])
    return pl.pallas_call(
        _sample_cmp_kernel, grid_spec=grid_spec,
        out_shape=[jax.ShapeDtypeStruct((bsz, G // 2, N_SUB, LANES), bf16)] * 2,
        compiler_params=pltpu.CompilerParams(dimension_semantics=("arbitrary",), vmem_limit_bytes=VMEM_LIMIT),
        name="sample_compress",
    )(page_table, *([cache3] * N_PAGES), pe, w1.reshape(2, CMP_LEN, DH, CMP_HID).astype(bf16), b1,
      w2.astype(bf16), b2, kg)


T_NEW = 4
NEW_PAD = 128
NB_SAMPLE = -(-(PAST_LEN + T_NEW) // SEL_BLOCK)
N_CMP_SAMPLE = (PAST_LEN + T_NEW) // CMP_STRIDE - 1
ROWS_S = HPG * T_NEW


def _sample_attn_kernel(pt_ref, *refs):
    pages = refs[:N_PAGES]
    (q_ref, gt_ref, ck_ref, cv_ref, new_ref, win_ref, wnew_ref, impt_ref, oh_ref, ohn_ref, o_ref) = refs[N_PAGES:]
    rows = 2 * ROWS_S
    t_of_row = lax.broadcasted_iota(jnp.int32, (rows, LANES), 0) & (T_NEW - 1)
    lane = lax.broadcasted_iota(jnp.int32, (rows, LANES), 1)
    q_pos = PAST_LEN + t_of_row
    hs_o = lax.broadcasted_iota(jnp.int32, (SUBLANES, rows), 0)
    hs_r = lax.broadcasted_iota(jnp.int32, (SUBLANES, rows), 1)
    hsum = ((hs_o >> 2 == hs_r >> 4) & ((hs_o & 3) == (hs_r & 3))).astype(bf16)
    sp_r = lax.broadcasted_iota(jnp.int32, (rows, SUBLANES), 0)
    sp_o = lax.broadcasted_iota(jnp.int32, (rows, SUBLANES), 1)
    spread = ((sp_o >> 2 == sp_r >> 4) & ((sp_o & 3) == (sp_r & 3))).astype(bf16)
    blk8 = lax.broadcasted_iota(jnp.int32, (SUBLANES, LANES), 1)
    qp8 = PAST_LEN + (lax.broadcasted_iota(jnp.int32, (SUBLANES, LANES), 0) & (T_NEW - 1))
    qb8 = qp8 >> 6
    forced = (blk8 == 0) | (blk8 == qb8) | (blk8 == qb8 - 1)
    causal = (blk8 * SEL_BLOCK <= qp8) & (blk8 < NB_SAMPLE)
    n_buf = win_ref.shape[1]

    for c in range(G // 2):
        lanes_k = pl.ds(c * LANES, LANES)
        lanes_v = pl.ds(G * DH + c * LANES, LANES)
        q = q_ref[0, c]
        gt = gt_ref[0, c]
        s_c = _nt(q, ck_ref[0, c])
        ok_c = (lane * CMP_STRIDE + (CMP_LEN - 1) <= q_pos) & (lane < N_CMP_SAMPLE)
        p_c = _masked_softmax(s_c, ok_c)
        o_c = jnp.dot(p_c.astype(bf16), cv_ref[0, c], preferred_element_type=f32)
        hi, lo = _split_bf16(p_c)
        psum = jnp.dot(hsum, hi, preferred_element_type=f32) + jnp.dot(hsum, lo, preferred_element_type=f32)
        hi, lo = _split_bf16(psum)
        imp = _nt(hi, impt_ref[...]) + _nt(lo, impt_ref[...])
        score = jnp.where(blk8 < NB_SAMPLE,
                          jnp.where(causal, jnp.where(forced, FORCE_SCORE, imp), MASK_SCORE), PICKED)
        sel = _topk_by_rank(score, blk8, NB_SAMPLE) & causal
        bias8 = jnp.where(sel, 0.0, MASK_SCORE).astype(bf16)
        bias = jnp.dot(spread, bias8, preferred_element_type=f32).astype(bf16)
        k_all = jnp.concatenate([pages[j][0, :, lanes_k] for j in range(N_PAGES)], axis=0).astype(bf16)
        s_past = _nt(q, k_all) + jnp.dot(bias, oh_ref[...], preferred_element_type=f32)
        s_new = _nt(q, new_ref[0, :, lanes_k].astype(bf16)) + jnp.dot(bias, ohn_ref[...],
                                                                      preferred_element_type=f32)
        s_new = jnp.where(lane <= t_of_row, s_new, 2.0 * MASK_SCORE)
        s_all = jnp.concatenate([s_past, s_new], axis=1)
        e = jnp.exp(s_all - jnp.max(s_all, axis=1, keepdims=True))
        den = jnp.maximum(jnp.sum(e, axis=1, keepdims=True), 1e-30)
        v_all = jnp.concatenate([pages[j][0, :, lanes_v] for j in range(N_PAGES)] + [new_ref[0, :, lanes_v]],
                                axis=0).astype(bf16)
        o_s = jnp.dot(e.astype(bf16), v_all, preferred_element_type=f32) / den
        kw_all = jnp.concatenate([win_ref[0, :, lanes_k], wnew_ref[0, :, lanes_k]], axis=0).astype(bf16)
        idx = lax.broadcasted_iota(jnp.int32, (rows, n_buf), 1)
        t_w = lax.broadcasted_iota(jnp.int32, (rows, n_buf), 0) & (T_NEW - 1)
        w_pos = PAST_LEN - n_buf + idx
        dist = PAST_LEN + t_w - w_pos
        ok_w = jnp.concatenate([(dist >= 0) & (dist < WINDOW) & (w_pos >= 0), lane <= t_of_row], axis=1)
        p_w = _masked_softmax(_nt(q, kw_all), ok_w).astype(bf16)
        vw_all = jnp.concatenate([win_ref[0, :, lanes_v], wnew_ref[0, :, lanes_v]], axis=0).astype(bf16)
        o_w = jnp.dot(p_w, vw_all, preferred_element_type=f32)
        o_ref[0, c] = gt[:, 0:1] * o_c + gt[:, 1:2] * o_s + gt[:, 2:3] * o_w


def sample_attention(cache3, page_table, q2, gt2, ck2, cv2, new_sel, win_cache, new_win):
    bsz = page_table.shape[0]
    n_buf = win_cache.shape[1]
    rows = 2 * ROWS_S
    impt = _imp_map_t(N_SUB, LANES, N_CMP_SAMPLE, NB_SAMPLE)
    key_blk = np.arange(PAST_LEN) // SEL_BLOCK
    onehot = jnp.asarray(key_blk[None, :] == np.arange(LANES)[:, None], bf16)
    onehot_new = jnp.asarray(np.broadcast_to(np.arange(LANES)[:, None] == PAST_LEN // SEL_BLOCK,
                                             (LANES, NEW_PAD)), bf16)

    def page_spec(j):
        return pl.BlockSpec((1, PAGE_SIZE, HALF), lambda b, pt: (pt[b, j], 0, 1))

    per_seq = lambda shape: pl.BlockSpec((1,) + shape, lambda b, pt: (b,) + (0,) * len(shape))
    const = lambda shape: pl.BlockSpec(shape, lambda b, pt: (0,) * len(shape))
    grid_spec = pltpu.PrefetchScalarGridSpec(
        num_scalar_prefetch=1, grid=(bsz,),
        in_specs=[page_spec(j) for j in range(N_PAGES)] + [
            per_seq((G // 2, rows, LANES)), per_seq((G // 2, rows, 3)),
            per_seq((G // 2, N_SUB, LANES)), per_seq((G // 2, N_SUB, LANES)),
            per_seq((NEW_PAD, HALF)), per_seq((n_buf, HALF)), per_seq((NEW_PAD, HALF)),
            const((LANES, N_SUB)), const((LANES, PAST_LEN)), const((LANES, NEW_PAD))],
        out_specs=per_seq((G // 2, rows, LANES)))
    return pl.pallas_call(
        _sample_attn_kernel, grid_spec=grid_spec,
        out_shape=jax.ShapeDtypeStruct((bsz, G // 2, rows, LANES), f32),
        compiler_params=pltpu.CompilerParams(dimension_semantics=("arbitrary",), vmem_limit_bytes=VMEM_LIMIT),
        name="sample_attention",
    )(page_table, *([cache3] * N_PAGES), q2, gt2, ck2, cv2, new_sel, win_cache, new_win, impt, onehot, onehot_new)


def rmsnorm(x, g):
    y = x * lax.rsqrt(jnp.mean(x * x, -1, keepdims=True) + NORM_EPS)
    return y * g


def to_heads(x):
    return x.reshape(x.shape[0], x.shape[1], RW_HEADS, RW_HEAD_DIM)


def _ffn_in_kernel(x_ref, g_ref, w_ref, o_ref):
    x = x_ref[...]
    xn = x * lax.rsqrt(jnp.mean(x * x, axis=-1, keepdims=True) + NORM_EPS) * g_ref[...]
    o_ref[...] = jnp.dot(xn.astype(bf16), w_ref[...], preferred_element_type=f32)


def _ffn_out_kernel(x_ref, h_ref, halo_ref, cs_ref, cw_ref, cb_ref, w_ref, o_ref, u_scr, *, tiles_per_seq):
    first = pl.program_id(0) % tiles_per_seq == 0
    tm = h_ref.shape[0]
    u_scr[pl.ds(0, SUBLANES), :] = halo_ref[...]
    u_scr[pl.ds(SUBLANES, tm), :] = h_ref[:, pl.ds(0, D_FF)]

    @pl.when(first)
    def _():
        u_scr[pl.ds(SUBLANES - (CONV_W - 1), CONV_W - 1), :] = cs_ref[0]

    uc = cb_ref[...] + cw_ref[CONV_W - 1:CONV_W, :] * u_scr[pl.ds(SUBLANES, tm), :]
    for j in range(CONV_W - 1):
        uc = uc + cw_ref[j:j + 1, :] * u_scr[pl.ds(SUBLANES - (CONV_W - 1) + j, tm), :]
    act = _gelu(uc) * h_ref[:, pl.ds(D_FF, D_FF)]
    o_ref[...] = x_ref[...] + jnp.dot(act.astype(bf16), w_ref[...], preferred_element_type=f32)


FFN_TM = 512
FFN_TN = 1408
FFN_OUT_TM = 256


def conv_ffn_fused(x, g, conv_state, w_in, conv_w, conv_b, w_out):
    bsz, t_len, _ = x.shape
    m = bsz * t_len
    tiles_per_seq = t_len // FFN_OUT_TM
    x2 = x.reshape(m, D_MODEL)
    h = pl.pallas_call(
        _ffn_in_kernel,
        grid=(m // FFN_TM, 2 * D_FF // FFN_TN),
        in_specs=[pl.BlockSpec((FFN_TM, D_MODEL), lambda i, j: (i, 0)),
                  pl.BlockSpec((1, D_MODEL), lambda i, j: (0, 0)),
                  pl.BlockSpec((D_MODEL, FFN_TN), lambda i, j: (0, j))],
        out_specs=pl.BlockSpec((FFN_TM, FFN_TN), lambda i, j: (i, j)),
        out_shape=jax.ShapeDtypeStruct((m, 2 * D_FF), f32),
        compiler_params=pltpu.CompilerParams(dimension_semantics=("parallel", "parallel"),
                                             vmem_limit_bytes=VMEM_LIMIT),
        name="ffn_in",
    )(x2, g.reshape(1, D_MODEL), w_in.astype(bf16))
    halo_blocks = FFN_OUT_TM // SUBLANES
    y = pl.pallas_call(
        functools.partial(_ffn_out_kernel, tiles_per_seq=tiles_per_seq),
        grid=(m // FFN_OUT_TM,),
        in_specs=[pl.BlockSpec((FFN_OUT_TM, D_MODEL), lambda i: (i, 0)),
                  pl.BlockSpec((FFN_OUT_TM, 2 * D_FF), lambda i: (i, 0)),
                  pl.BlockSpec((SUBLANES, D_FF), lambda i: (jnp.maximum(i * halo_blocks - 1, 0), 0)),
                  pl.BlockSpec((1, CONV_W - 1, D_FF), lambda i: (i // tiles_per_seq, 0, 0)),
                  pl.BlockSpec((CONV_W, D_FF), lambda i: (0, 0)),
                  pl.BlockSpec((1, D_FF), lambda i: (0, 0)),
                  pl.BlockSpec((D_FF, D_MODEL), lambda i: (0, 0))],
        out_specs=pl.BlockSpec((FFN_OUT_TM, D_MODEL), lambda i: (i, 0)),
        out_shape=jax.ShapeDtypeStruct((m, D_MODEL), f32),
        scratch_shapes=[pltpu.VMEM((SUBLANES + FFN_OUT_TM, D_FF), f32)],
        compiler_params=pltpu.CompilerParams(dimension_semantics=("arbitrary",), vmem_limit_bytes=56 * 1024 * 1024),
        name="ffn_out",
    )(x2, h, h, conv_state, conv_w, conv_b.reshape(1, D_FF), w_out.astype(bf16))
    conv_new = h.reshape(bsz, t_len, 2 * D_FF)[:, t_len - (CONV_W - 1):, :D_FF]
    return y.reshape(bsz, t_len, D_MODEL), conv_new


def conv_ffn(xn, conv_state, w_in, conv_w, conv_b, w_out):
    t_len = xn.shape[1]
    u, gate = jnp.split(mm3(xn, w_in), 2, axis=-1)
    u_pad = jnp.concatenate([conv_state, u], axis=1)
    uc = conv_b + conv_w[0] * u_pad[:, 0:t_len]
    for j in range(1, CONV_W):
        uc = uc + conv_w[j] * u_pad[:, j:j + t_len]
    return mm3(jax.nn.gelu(uc) * gate, w_out), u_pad[:, t_len:]


def rwkv7_mix(xn, shift_prev, s0, v_first, li, p, scan_cfg):
    bsz, t_len, _ = xn.shape
    xx = jnp.concatenate([shift_prev[:, None], xn[:, :-1]], axis=1) - xn
    mu = p['rw_mu'][li]
    xr, xw, xk, xv, xa, xg = (xn + xx * mu[i] for i in range(6))
    r = mm3(xr, p['rw_w_r'][li])
    k = mm3(xk, p['rw_w_k'][li])
    v = mm3(xv, p['rw_w_v'][li])
    w_raw = -jax.nn.softplus(-(p['rw_w0'][li] + mm3(jnp.tanh(mm3(xw, p['rw_w1'][li])), p['rw_w2'][li]))) - 0.5
    decay = jnp.exp(-jnp.exp(w_raw))
    a = jax.nn.sigmoid(p['rw_a0'][li] + mm3(mm3(xa, p['rw_a1'][li]), p['rw_a2'][li]))
    if li == 0:
        v_first = v
    else:
        v = v + (v_first - v) * jax.nn.sigmoid(
            p['rw_v0'][li - 1] + mm3(mm3(xv, p['rw_v1'][li - 1]), p['rw_v2'][li - 1]))
    g = mm3(jax.nn.sigmoid(mm3(xg, p['rw_g1'][li])), p['rw_g2'][li])
    kk = to_heads(k * p['rw_k_k'][li])
    kk = kk / jnp.maximum(jnp.sqrt(jnp.sum(kk * kk, -1, keepdims=True)), 1e-12)
    k = k * (1 + (a - 1) * p['rw_k_a'][li])
    kk = kk.reshape(bsz, t_len, D_MODEL)
    y, s_new = wkv_scan(r, decay, k, v, -kk, kk * a, to_pairs(s0), **scan_cfg)
    s_new = from_pairs(s_new, bsz)
    rh, kh, vh = to_heads(r), to_heads(k), to_heads(v)
    y = to_heads(y)
    mean = jnp.mean(y, -1, keepdims=True)
    var = jnp.mean(jnp.square(y - mean), -1, keepdims=True)
    yn = ((y - mean) * lax.rsqrt(var + RW_GN_EPS)).reshape(bsz, t_len, D_MODEL)
    yn = yn * p['rw_ln_w'][li] + p['rw_ln_b'][li]
    bonus = jnp.sum(rh * kh * p['rw_r_k'][li], -1, keepdims=True) * vh
    y = yn + bonus.reshape(bsz, t_len, D_MODEL)
    return mm3(y * g, p['rw_w_o'][li]), s_new, xn[:, -1], v_first


def kv_rows(h, p):
    bsz, t_len, _ = h.shape
    kv = mm3(rmsnorm(h, p['kv_norm_g']), p['w_kv']).reshape(bsz, t_len, 6, G, DH)
    k_sel = rmsnorm(kv[:, :, 2], p['k_norm_g'][1])
    k_win = rmsnorm(kv[:, :, 4], p['k_norm_g'][2])
    paged = jnp.stack([kv[:, :, 0], kv[:, :, 1], k_sel, kv[:, :, 3]], axis=2)
    win = jnp.stack([k_win, kv[:, :, 5]], axis=2)
    return paged, win


def nsa_query(xn, lb, p):
    bsz, t_len, _ = xn.shape
    n_q = NSA_HEADS * DH
    w = jnp.pad(p['nsa_w_q'][lb], ((0, 0), (0, LANES - 3 * NSA_HEADS)))
    proj = mm3(xn, w)
    q = proj[..., :n_q].reshape(bsz, t_len, G, HPG, DH)
    q = rmsnorm(q, p['q_norm_g'][lb]) * (DH ** -0.5)
    gates = jax.nn.sigmoid(proj[..., n_q:n_q + 3 * NSA_HEADS]).reshape(bsz, t_len, G, HPG, 3)
    return q, gates


def compress_rows_prompt(rows, pe, w1, b1, w2, b2):
    bsz, t_len = rows.shape[:2]
    n_sub = t_len // CMP_STRIDE
    sub = rows.reshape(bsz, n_sub, CMP_STRIDE, G, DH).transpose(0, 1, 3, 2, 4)
    half = CMP_STRIDE * DH
    x0 = (sub + pe[:CMP_STRIDE][None, None, None]).reshape(bsz * n_sub * G, half)
    x1 = (sub + pe[CMP_STRIDE:][None, None, None]).reshape(bsz * n_sub * G, half)
    p0 = mm(x0, w1[:half]).reshape(bsz, n_sub, G, CMP_HID)
    p1 = mm(x1, w1[half:]).reshape(bsz, n_sub, G, CMP_HID)
    hid = jax.nn.gelu(p0[:, :-1] + p1[:, 1:] + b1)
    out = mm(hid.reshape(bsz * (n_sub - 1) * G, CMP_HID), w2) + b2
    return out.reshape(bsz, n_sub - 1, G, DH)


def to_q_tiles(x, n_q):
    bsz, _, _, _, w = x.shape
    x = x.reshape(bsz, n_q, Q_BLOCK, G, HPG, w).transpose(0, 3, 1, 4, 2, 5)
    return x.reshape(bsz, G, n_q, HPG * Q_BLOCK, w)


def from_q_tiles(x, t_len):
    bsz, _, n_q, _, w = x.shape
    x = x.reshape(bsz, G, n_q, HPG, Q_BLOCK, w).transpose(0, 2, 4, 1, 3, 5)
    return x.reshape(bsz, t_len, G * HPG * w)


def prompt_context(h, p):
    paged, win = kv_rows(h, p)
    bsz, t_len = h.shape[:2]
    ck = rmsnorm(compress_rows_prompt(paged[:, :, 0], p['cmp_pe'][0], p['cmp_w1'][0], p['cmp_b1'][0],
                                      p['cmp_w2'][0], p['cmp_b2'][0]), p['k_norm_g'][0])
    cv = compress_rows_prompt(paged[:, :, 1], p['cmp_pe'][1], p['cmp_w1'][1], p['cmp_b1'][1],
                              p['cmp_w2'][1], p['cmp_b2'][1])
    n_cmp = ck.shape[1]
    pad = lambda c: jnp.pad(c, ((0, 0), (0, NC_PAD - n_cmp), (0, 0), (0, 0))).transpose(0, 2, 1, 3).astype(bf16)
    heads = lambda x: x.transpose(0, 2, 1, 3).astype(bf16)
    onehot = jnp.asarray(np.arange(t_len)[:, None] // SEL_BLOCK == np.arange(NB_PROMPT)[None, :], bf16)
    ke = jnp.concatenate([jnp.broadcast_to(onehot, (bsz, G, t_len, NB_PROMPT)), heads(paged[:, :, 2])], axis=-1)
    heads_t = lambda x: x.transpose(0, 2, 3, 1).astype(bf16)
    ctx = dict(ck=pad(ck), cv=pad(cv), n_cmp=n_cmp, ke=ke, vst=heads_t(paged[:, :, 3]),
               kw=heads(win[:, :, 0]), vwt=heads_t(win[:, :, 1]))
    keep = min(WINDOW, t_len)
    return ctx, (paged, win[:, t_len - keep:])


def attend_prompt(q, gates, ctx):
    bsz, t_len = q.shape[:2]
    n_q = t_len // Q_BLOCK
    qh = to_q_tiles(q, n_q).astype(bf16)
    gt = to_q_tiles(gates, n_q)
    oc, sb = nsa_cmp_select(qh, ctx['ck'], ctx['cv'], ctx['n_cmp'])
    qe = jnp.concatenate([jnp.tile(sb, (1, 1, 1, HPG, 1)), qh], axis=-1)
    o = nsa_sel_win(qe, ctx['ke'], ctx['vst'], ctx['kw'], ctx['vwt'], oc, gt)
    return from_q_tiles(o, t_len)


def sample_context(h, cache3, page_table, cache_kv_win, p):
    paged, win = kv_rows(h, p)
    bsz, t_new = h.shape[:2]
    ck, cv = sample_compress(cache3, page_table, p['cmp_pe'], p['cmp_w1'], p['cmp_b1'], p['cmp_w2'], p['cmp_b2'],
                             p['k_norm_g'][0:1])
    n_buf = cache_kv_win.shape[1]
    pad_rows = lambda x: jnp.pad(x.reshape(bsz, t_new, HALF), ((0, 0), (0, NEW_PAD - t_new), (0, 0)))
    ctx = dict(ck=ck, cv=cv, new_sel=pad_rows(paged[:, :, 2:4]), new_win=pad_rows(win),
               win_cache=cache_kv_win.reshape(bsz, n_buf, HALF))
    win_full = jnp.concatenate([cache_kv_win, win], axis=1)
    return ctx, (paged, win_full[:, win_full.shape[1] - n_buf:])


def attend_sample(q, gates, ctx, cache3, page_table):
    bsz, t_new = q.shape[:2]
    rows = lambda x: x.transpose(0, 2, 3, 1, 4).reshape(bsz, G // 2, 2 * HPG * t_new, x.shape[-1])
    qr = rows(q)
    own = (jnp.arange(2 * HPG * t_new) // (HPG * t_new))[:, None] == (jnp.arange(LANES) // DH)[None, :]
    q2 = jnp.where(own, jnp.concatenate([qr, qr], axis=-1), 0.0).astype(bf16)
    o = sample_attention(cache3, page_table, q2, rows(gates), ctx['ck'], ctx['cv'],
                         ctx['new_sel'], ctx['win_cache'], ctx['new_win'])
    o = jnp.where(own, o, 0.0)
    o = o[..., :DH] + o[..., DH:]
    return o.reshape(bsz, G, HPG, t_new, DH).transpose(0, 3, 1, 2, 4).reshape(bsz, t_new, NSA_HEADS * DH)


def forward(x, wkv_in, shift_in, conv_in, make_ctx, attend, p, scan_cfg):
    wkv_out, shift_out, conv_out = [], [], []
    v_first = None
    ctx = None
    kv_state = None
    for layer in range(DEPTH):
        if layer == N_A_LAYERS:
            ctx, kv_state = make_ctx(x)
        xn = rmsnorm(x, p['norm_g'][layer, 0])
        if layer < N_A_LAYERS:
            mix, s_new, sh_new, v_first = rwkv7_mix(xn, shift_in[layer], wkv_in[layer], v_first, layer, p, scan_cfg)
            wkv_out.append(s_new)
            shift_out.append(sh_new)
        else:
            lb = layer - N_A_LAYERS
            q, gates = nsa_query(xn, lb, p)
            mix = mm3(attend(q, gates, ctx), p['nsa_w_o'][lb])
        x = x + mix
        ffn_w = (p['ffn_w_in'][layer], p['ffn_conv_w'][layer], p['ffn_conv_b'][layer], p['ffn_w_out'][layer])
        if x.shape[1] % FFN_TM == 0:
            x, conv_new = conv_ffn_fused(x, p['norm_g'][layer, 1], conv_in[layer], *ffn_w)
        else:
            f, conv_new = conv_ffn(rmsnorm(x, p['norm_g'][layer, 1]), conv_in[layer], *ffn_w)
            x = x + f
        conv_out.append(conv_new)
    return x, kv_state, jnp.stack(wkv_out), jnp.stack(shift_out), jnp.stack(conv_out)


def kernel(x_prompt, x_sample, cache_kv, cache_kv_win, state_wkv, state_shift, state_ffn_conv, page_table, norm_g, rw_mu, rw_w_r, rw_w_k, rw_w_v, rw_w_o, rw_w0, rw_w1, rw_w2, rw_a0, rw_a1, rw_a2, rw_v0, rw_v1, rw_v2, rw_g1, rw_g2, rw_k_k, rw_k_a, rw_r_k, rw_ln_w, rw_ln_b, kv_norm_g, w_kv, k_norm_g, cmp_pe, cmp_w1, cmp_b1, cmp_w2, cmp_b2, nsa_w_q, q_norm_g, nsa_w_o, ffn_w_in, ffn_conv_w, ffn_conv_b, ffn_w_out):
    p = dict(norm_g=norm_g, rw_mu=rw_mu, rw_w_r=rw_w_r, rw_w_k=rw_w_k, rw_w_v=rw_w_v, rw_w_o=rw_w_o,
             rw_w0=rw_w0, rw_w1=rw_w1, rw_w2=rw_w2, rw_a0=rw_a0, rw_a1=rw_a1, rw_a2=rw_a2,
             rw_v0=rw_v0, rw_v1=rw_v1, rw_v2=rw_v2, rw_g1=rw_g1, rw_g2=rw_g2, rw_k_k=rw_k_k, rw_k_a=rw_k_a,
             rw_r_k=rw_r_k, rw_ln_w=rw_ln_w, rw_ln_b=rw_ln_b, kv_norm_g=kv_norm_g, w_kv=w_kv,
             k_norm_g=k_norm_g, cmp_pe=cmp_pe, cmp_w1=cmp_w1, cmp_b1=cmp_b1, cmp_w2=cmp_w2, cmp_b2=cmp_b2,
             nsa_w_q=nsa_w_q, q_norm_g=q_norm_g, nsa_w_o=nsa_w_o, ffn_w_in=ffn_w_in,
             ffn_conv_w=ffn_conv_w, ffn_conv_b=ffn_conv_b, ffn_w_out=ffn_w_out)
    bp, t_p = x_prompt.shape[:2]
    bs = x_sample.shape[0]

    wkv0 = jnp.zeros((N_A_LAYERS, bp, RW_HEADS, RW_HEAD_DIM, RW_HEAD_DIM), f32)
    shift0 = jnp.zeros((N_A_LAYERS, bp, D_MODEL), f32)
    conv0 = jnp.zeros((DEPTH, bp, CONV_W - 1, D_FF), f32)
    y_prompt, (kv_prompt, win_prompt), wkv_prompt, shift_prompt, conv_prompt = forward(
        x_prompt, wkv0, shift0, conv0, lambda h: prompt_context(h, p), attend_prompt, p,
        dict(bb=bp, tc=128))

    cache3 = cache_kv.reshape(cache_kv.shape[0], PAGE_SIZE, 4 * G * DH)
    y_sample, (kv_sample, win_sample), wkv_sample, shift_sample, conv_sample = forward(
        x_sample, state_wkv, state_shift, state_ffn_conv,
        lambda h: sample_context(h, cache3, page_table, cache_kv_win, p),
        lambda q, gates, ctx: attend_sample(q, gates, ctx, cache3, page_table), p,
        dict(bb=2, tc=x_sample.shape[1]))

    return (y_prompt, y_sample, kv_prompt, kv_sample, win_prompt, win_sample,
            wkv_prompt, wkv_sample, shift_prompt, shift_sample, conv_prompt, conv_sample)
```

```python
import functools

import numpy as np
import jax
import jax.numpy as jnp
from jax import lax
from jax.experimental import pallas as pl
from jax.experimental.pallas import tpu as pltpu

f32 = jnp.float32
bf16 = jnp.bfloat16

D_MODEL = 1024
DEPTH = 4
N_A_LAYERS = 2
PAST_LEN = 2048
PAGE_SIZE = 128
N_PAGES = PAST_LEN // PAGE_SIZE
RW_HEAD_DIM = 64
RW_HEADS = 16
RW_GN_EPS = 64e-5
DH = 64
NSA_HEADS = 16
G = 4
HPG = 4
CMP_LEN = 32
CMP_STRIDE = 16
CMP_HID = 128
SEL_BLOCK = 64
N_SELECT = 16
WINDOW = 512
Q_BLOCK = 128
D_FF = 2816
CONV_W = 3
NORM_EPS = 1e-6
NEG_INF = -1e30
FORCE_SCORE = 1e9
MASK_SCORE = -1e9
PICKED = -3e38

LANES = 128
SUBLANES = 8
VMEM_LIMIT = 48 * 1024 * 1024


def _nt(a, b):
    return lax.dot_general(a, b, (((1,), (1,)), ((), ())), preferred_element_type=f32)


def _split_bf16(x):
    hi = x.astype(bf16)
    lo = (x - hi.astype(f32)).astype(bf16)
    return hi, lo


def _mm_kernel(x_ref, w_ref, o_ref):
    o_ref[...] = jnp.dot(x_ref[...].astype(bf16), w_ref[...], preferred_element_type=f32)


MM_TM = 512


def _pick_tile(n, cands):
    for c in cands:
        if n % c == 0:
            return c
    return n


def mm(x, w):
    m, k = x.shape
    n = w.shape[1]
    w = w.astype(bf16)
    tm = MM_TM if m >= MM_TM else -(-m // SUBLANES) * SUBLANES
    mp = -(-m // tm) * tm
    if mp != m:
        x = jnp.pad(x, ((0, mp - m), (0, 0)))
    tn = n if n <= 1536 else _pick_tile(n, (1408, 1024, 512, 256, 128))
    out = pl.pallas_call(
        _mm_kernel,
        grid=(mp // tm, n // tn),
        in_specs=[pl.BlockSpec((tm, k), lambda i, j: (i, 0)),
                  pl.BlockSpec((k, tn), lambda i, j: (0, j))],
        out_specs=pl.BlockSpec((tm, tn), lambda i, j: (i, j)),
        out_shape=jax.ShapeDtypeStruct((mp, n), f32),
        compiler_params=pltpu.CompilerParams(dimension_semantics=("parallel", "parallel"),
                                             vmem_limit_bytes=VMEM_LIMIT),
        name="mm",
    )(x, w)
    return out[:m] if mp != m else out


def mm3(x, w):
    b, t, k = x.shape
    return mm(x.reshape(b * t, k), w).reshape(b, t, w.shape[1])


PAIR_W = 2 * RW_HEAD_DIM
N_PAIRS = RW_HEADS // 2


def _scan_kernel(r_ref, w_ref, k_ref, v_ref, a_ref, b_ref, s0_ref, y_ref, s_ref, *, bb, tc):
    n = RW_HEAD_DIM
    npairs = bb * N_PAIRS

    @pl.when(pl.program_id(1) == 0)
    def _():
        s_ref[...] = s0_ref[...]

    row = lax.broadcasted_iota(jnp.int32, (n, PAIR_W), 0)
    lane = lax.broadcasted_iota(jnp.int32, (n, PAIR_W), 1)
    diag = (lane % n) == row
    kk = lax.broadcasted_iota(jnp.int32, (PAIR_W, PAIR_W), 0) // n
    ll = lax.broadcasted_iota(jnp.int32, (PAIR_W, PAIR_W), 1) // n
    bones = (kk == ll).astype(bf16)
    group = min(tc, SUBLANES)

    def seg_sum_bcast(x):
        return jnp.dot(x.astype(bf16), bones, preferred_element_type=f32)

    def token(t0, s):
        def rowvec(ref, bi, hp):
            tile = ref[bi, pl.ds(t0, group), pl.ds(hp * PAIR_W, PAIR_W)]
            return jnp.broadcast_to(tile[s:s + 1], (n, PAIR_W))

        pairs = [(bi, hp) for bi in range(bb) for hp in range(N_PAIRS)]
        st = [s_ref[p] for p in range(npairs)]
        sa_in = jnp.concatenate([st[p] * rowvec(a_ref, *pairs[p]) for p in range(npairs)], axis=0)
        v_in = jnp.concatenate([jnp.where(diag, rowvec(v_ref, *pairs[p]), 0.0) for p in range(npairs)], axis=0)
        sa_col = seg_sum_bcast(sa_in)
        v_col = seg_sum_bcast(v_in)
        new = []
        for p in range(npairs):
            sl = slice(p * n, (p + 1) * n)
            sn = (st[p] * rowvec(w_ref, *pairs[p]) + sa_col[sl] * rowvec(b_ref, *pairs[p])
                  + v_col[sl] * rowvec(k_ref, *pairs[p]))
            s_ref[p] = sn
            new.append(sn)
        y_in = jnp.concatenate([new[p] * rowvec(r_ref, *pairs[p]) for p in range(npairs)], axis=0)
        y_col = seg_sum_bcast(y_in)
        rows = []
        for p in range(npairs):
            sl = slice(p * n, (p + 1) * n)
            rows.append(jnp.sum(jnp.where(diag, y_col[sl], 0.0), axis=0, keepdims=True))
        return rows

    def block(g, carry):
        t0 = pl.multiple_of(g * group, group)
        rows = [token(t0, s) for s in range(group)]
        for p in range(npairs):
            bi, hp = divmod(p, N_PAIRS)
            y_ref[bi, pl.ds(t0, group), pl.ds(hp * PAIR_W, PAIR_W)] = jnp.concatenate(
                [rows[s][p] for s in range(group)], axis=0)
        return carry

    lax.fori_loop(0, tc // group, block, 0)


def wkv_scan(r, w, k, v, a, b, s0_pairs, *, bb, tc):
    bsz, t_len, _ = r.shape
    assert bsz % bb == 0 and t_len % tc == 0
    vec_spec = pl.BlockSpec((bb, tc, D_MODEL), lambda i, j: (i, j, 0))
    st_spec = pl.BlockSpec((bb * N_PAIRS, RW_HEAD_DIM, PAIR_W), lambda i, j: (i, 0, 0))
    return pl.pallas_call(
        functools.partial(_scan_kernel, bb=bb, tc=tc),
        grid=(bsz // bb, t_len // tc),
        in_specs=[vec_spec] * 6 + [st_spec],
        out_specs=[vec_spec, st_spec],
        out_shape=[jax.ShapeDtypeStruct((bsz, t_len, D_MODEL), f32),
                   jax.ShapeDtypeStruct((bsz * N_PAIRS, RW_HEAD_DIM, PAIR_W), f32)],
        compiler_params=pltpu.CompilerParams(dimension_semantics=("arbitrary", "arbitrary"),
                                             vmem_limit_bytes=VMEM_LIMIT),
        name="wkv_scan",
    )(r, w, k, v, a, b, s0_pairs)


def to_pairs(s):
    bsz = s.shape[0]
    return s.reshape(bsz, N_PAIRS, 2, RW_HEAD_DIM, RW_HEAD_DIM).transpose(0, 1, 3, 2, 4).reshape(
        bsz * N_PAIRS, RW_HEAD_DIM, PAIR_W)


def from_pairs(s, bsz):
    return s.reshape(bsz, N_PAIRS, RW_HEAD_DIM, 2, RW_HEAD_DIM).transpose(0, 1, 3, 2, 4).reshape(
        bsz, RW_HEADS, RW_HEAD_DIM, RW_HEAD_DIM)


def _masked_softmax(s, mask):
    s = jnp.where(mask, s, NEG_INF)
    e = jnp.where(mask, jnp.exp(s - jnp.max(s, axis=-1, keepdims=True)), 0.0)
    return e / jnp.maximum(jnp.sum(e, axis=-1, keepdims=True), 1e-30)


def _topk_rounds(score, idx, axis):
    sel = jnp.zeros(score.shape, f32)
    idx = idx.astype(f32)
    for _ in range(N_SELECT):
        m = jnp.max(score, axis=axis, keepdims=True)
        first = jnp.min(jnp.where(score == m, idx, 1e6), axis=axis, keepdims=True)
        pick = idx == first
        sel = jnp.where(pick, 1.0, sel)
        score = jnp.where(pick, PICKED, score)
    return sel > 0.5


def _topk_by_rank(score, idx, n_valid):
    rank = jnp.zeros(score.shape, f32)
    for i in range(n_valid):
        ci = score[:, i:i + 1]
        beats = (ci > score) | ((ci == score) & (idx > i))
        rank = rank + jnp.where(beats, 1.0, 0.0)
    return rank < (N_SELECT - 0.5)


def _imp_map_t(n_cmp_pad, n_blk_pad, n_cmp, n_blk):
    ci = np.arange(n_cmp_pad)[None, :] * CMP_STRIDE
    sj = np.arange(n_blk_pad)[:, None] * SEL_BLOCK
    ov = np.clip(np.minimum(ci + CMP_LEN, sj + SEL_BLOCK) - np.maximum(ci, sj), 0, None) / CMP_STRIDE
    ov = ov * (np.arange(n_cmp_pad)[None, :] < n_cmp) * (np.arange(n_blk_pad)[:, None] < n_blk)
    return jnp.asarray(ov, bf16)


NC_PAD = 512
NB_PROMPT = 128


def _cmp_select_kernel(q_ref, ck_ref, cv_ref, impt_ref, oc_ref, sb_ref, *, n_cmp):
    qi = pl.program_id(2)
    q = q_ref[0, 0, 0]
    s = _nt(q, ck_ref[0, 0])
    rows = HPG * Q_BLOCK
    t_loc = lax.broadcasted_iota(jnp.int32, (rows, NC_PAD), 0) & (Q_BLOCK - 1)
    q_pos = qi * Q_BLOCK + t_loc
    nn = lax.broadcasted_iota(jnp.int32, (rows, NC_PAD), 1)
    valid = (nn * CMP_STRIDE + (CMP_LEN - 1) <= q_pos) & (nn < n_cmp)
    p = _masked_softmax(s, valid)
    oc_ref[0, 0, 0] = jnp.dot(p.astype(bf16), cv_ref[0, 0], preferred_element_type=f32)
    psum = p[0:Q_BLOCK]
    for h in range(1, HPG):
        psum = psum + p[h * Q_BLOCK:(h + 1) * Q_BLOCK]
    hi, lo = _split_bf16(psum)
    imp_t = _nt(impt_ref[...], hi) + _nt(impt_ref[...], lo)
    blk = lax.broadcasted_iota(jnp.int32, (NB_PROMPT, Q_BLOCK), 0)
    qp = qi * Q_BLOCK + lax.broadcasted_iota(jnp.int32, (NB_PROMPT, Q_BLOCK), 1)
    qb = qp >> 6
    forced = (blk == 0) | (blk == qb) | (blk == qb - 1)
    causal = blk * SEL_BLOCK <= qp
    score = jnp.where(causal, jnp.where(forced, FORCE_SCORE, imp_t), MASK_SCORE)
    sel = _topk_rounds(score, blk, 0) & causal
    bias_t = jnp.where(sel, 0.0, MASK_SCORE)
    sb_ref[0, 0, 0] = bias_t.T.astype(bf16)


def nsa_cmp_select(qh, ck, cv, n_cmp):
    bsz, _, n_q = qh.shape[:3]
    rows = HPG * Q_BLOCK
    impt = _imp_map_t(NC_PAD, NB_PROMPT, n_cmp, NB_PROMPT)
    return pl.pallas_call(
        functools.partial(_cmp_select_kernel, n_cmp=n_cmp),
        grid=(bsz, G, n_q),
        in_specs=[pl.BlockSpec((1, 1, 1, rows, DH), lambda b, g, i: (b, g, i, 0, 0)),
                  pl.BlockSpec((1, 1, NC_PAD, DH), lambda b, g, i: (b, g, 0, 0)),
                  pl.BlockSpec((1, 1, NC_PAD, DH), lambda b, g, i: (b, g, 0, 0)),
                  pl.BlockSpec((NB_PROMPT, NC_PAD), lambda b, g, i: (0, 0))],
        out_specs=[pl.BlockSpec((1, 1, 1, rows, DH), lambda b, g, i: (b, g, i, 0, 0)),
                   pl.BlockSpec((1, 1, 1, Q_BLOCK, NB_PROMPT), lambda b, g, i: (b, g, i, 0, 0))],
        out_shape=[jax.ShapeDtypeStruct((bsz, G, n_q, rows, DH), f32),
                   jax.ShapeDtypeStruct((bsz, G, n_q, Q_BLOCK, NB_PROMPT), bf16)],
        compiler_params=pltpu.CompilerParams(dimension_semantics=("parallel", "parallel", "parallel"),
                                             vmem_limit_bytes=VMEM_LIMIT),
        name="nsa_cmp_select",
    )(qh, ck, cv, impt)


KEXT = NB_PROMPT + DH
SEL_KT = 4 * Q_BLOCK


def _sel_win_kernel(qe_ref, ke_ref, vst_ref, kw_ref, vwt_ref, oc_ref, gt_ref, o_ref, s_scr, p_scr):
    qi = pl.program_id(2)
    rows = HPG * Q_BLOCK
    qe = qe_ref[0, 0, 0]
    q = qe[:, NB_PROMPT:]
    init =(jnp.full((1, rows), NEG_INF, f32), jnp.zeros((1, rows), f32), jnp.zeros((DH, rows), f32))

    def sel_scores(off):
        return _nt(ke_ref[0, 0, pl.ds(off, SEL_KT), :], qe)

    def pv(j):
        off = pl.multiple_of(jnp.maximum(j, 0) * SEL_KT, SEL_KT)
        return jnp.dot(vst_ref[0, 0, :, pl.ds(off, SEL_KT)], p_scr[...], preferred_element_type=f32)

    def softmax_step(s, m, l):
        m_new = jnp.maximum(m, jnp.max(s, axis=0, keepdims=True))
        alpha = jnp.exp(m - m_new)
        p = jnp.exp(s - m_new)
        return m_new, alpha * l + jnp.sum(p, axis=0, keepdims=True), alpha, p.astype(bf16)

    n_full = (qi * Q_BLOCK) // SEL_KT
    s_scr[...] = sel_scores(0)
    p_scr[...] = jnp.zeros((SEL_KT, rows), bf16)

    def sel_body(j, carry):
        m, l, acc, alpha_prev = carry
        s_next = sel_scores(pl.multiple_of((j + 1) * SEL_KT, SEL_KT))
        acc = alpha_prev * acc + pv(j - 1)
        m, l, alpha, p = softmax_step(s_scr[...], m, l)
        p_scr[...] = p
        s_scr[...] = s_next
        return m, l, acc, alpha

    m, l, acc, alpha_prev = lax.fori_loop(0, n_full, sel_body, init + (jnp.ones((1, rows), f32),))
    acc = alpha_prev * acc + pv(n_full - 1)
    off = pl.multiple_of(n_full * SEL_KT, SEL_KT)
    key_pos = off + lax.broadcasted_iota(jnp.int32, (SEL_KT, rows), 0)
    t_row = lax.broadcasted_iota(jnp.int32, (SEL_KT, rows), 1) & (Q_BLOCK - 1)
    s = jnp.where(key_pos <= qi * Q_BLOCK + t_row, s_scr[...], 2.0 * MASK_SCORE)
    _, l_s, alpha, p = softmax_step(s, m, l)
    acc_s = alpha * acc + jnp.dot(vst_ref[0, 0, :, pl.ds(off, SEL_KT)], p, preferred_element_type=f32)
    o_s = (acc_s / jnp.maximum(l_s, 1e-30)).T

    n_wk = WINDOW + Q_BLOCK
    w0 = pl.multiple_of(jnp.maximum(qi * Q_BLOCK - WINDOW, 0), Q_BLOCK)
    dist = (qi * Q_BLOCK + (lax.broadcasted_iota(jnp.int32, (n_wk, rows), 1) & (Q_BLOCK - 1))
            - (w0 + lax.broadcasted_iota(jnp.int32, (n_wk, rows), 0)))
    ok = (dist >= 0) & (dist < WINDOW)
    s = jnp.where(ok, _nt(kw_ref[0, 0, pl.ds(w0, n_wk), :], q), NEG_INF)
    e = jnp.where(ok, jnp.exp(s - jnp.max(s, axis=0, keepdims=True)), 0.0)
    l_w = jnp.sum(e, axis=0, keepdims=True)
    acc_w = jnp.dot(vwt_ref[0, 0, :, pl.ds(w0, n_wk)], e.astype(bf16), preferred_element_type=f32)
    o_w = (acc_w / jnp.maximum(l_w, 1e-30)).T

    gt = gt_ref[0, 0, 0]
    o_ref[0, 0, 0] = gt[:, 0:1] * oc_ref[0, 0, 0] + gt[:, 1:2] * o_s + gt[:, 2:3] * o_w


def nsa_sel_win(qe, ke, vst, kw, vwt, oc, gt):
    bsz, _, n_q = qe.shape[:3]
    t_len = ke.shape[2]
    rows = HPG * Q_BLOCK
    tile = lambda w: pl.BlockSpec((1, 1, 1, rows, w), lambda b, g, i: (b, g, i, 0, 0))
    full = lambda w: pl.BlockSpec((1, 1, t_len, w), lambda b, g, i: (b, g, 0, 0))
    full_t = pl.BlockSpec((1, 1, DH, t_len), lambda b, g, i: (b, g, 0, 0))
    return pl.pallas_call(
        _sel_win_kernel,
        grid=(bsz, G, n_q),
        in_specs=[tile(KEXT), full(KEXT), full_t, full(DH), full_t, tile(DH), tile(3)],
        out_specs=tile(DH),
        out_shape=jax.ShapeDtypeStruct((bsz, G, n_q, rows, DH), f32),
        scratch_shapes=[pltpu.VMEM((SEL_KT, rows), f32), pltpu.VMEM((SEL_KT, rows), bf16)],
        compiler_params=pltpu.CompilerParams(dimension_semantics=("parallel", "parallel", "arbitrary"),
                                             vmem_limit_bytes=VMEM_LIMIT),
        name="nsa_sel_win",
    )(qe, ke, vst, kw, vwt, oc, gt)


N_SUB = PAST_LEN // CMP_STRIDE
HALF = 2 * G * DH


def _gelu(x):
    return 0.5 * x * (1.0 + jnp.tanh(0.7978845608028654 * (x + 0.044715 * (x * x * x))))


def _sample_cmp_kernel(pt_ref, *refs):
    pages = refs[:N_PAGES]
    pe_ref, w1_ref, b1_ref, w2_ref, b2_ref, kg_ref, ck_ref, cv_ref, x_scr = refs[N_PAGES:]
    n_chunk = HALF // LANES
    for c in range(n_chunk):
        for j in range(N_PAGES):
            x_scr[c, pl.ds(j * PAGE_SIZE, PAGE_SIZE), :] = pages[j][0, :, pl.ds(c * LANES, LANES)]
        x_scr[c, pl.ds(PAST_LEN, CMP_STRIDE), :] = jnp.zeros((CMP_STRIDE, LANES), f32)
    for part in range(2):
        acc = jnp.zeros((G * N_SUB, CMP_HID), f32)
        for s in range(CMP_LEN):
            two = [x_scr[part * (G // 2) + c, pl.ds(s, N_SUB, stride=CMP_STRIDE), :] for c in range(G // 2)]
            x = jnp.concatenate([two[g // 2][:, (g % 2) * DH:(g % 2 + 1) * DH] for g in range(G)],
                                axis=0)
            x = x + pe_ref[part, s:s + 1, :]
            acc = acc + jnp.dot(x.astype(bf16), w1_ref[part, s], preferred_element_type=f32)
        hid = _gelu(acc + b1_ref[part:part + 1, :])
        out = jnp.dot(hid.astype(bf16), w2_ref[part], preferred_element_type=f32) + b2_ref[part:part + 1, :]
        if part == 0:
            out = out * lax.rsqrt(jnp.mean(out * out, axis=-1, keepdims=True) + NORM_EPS) * kg_ref[...]
        out_ref = ck_ref if part == 0 else cv_ref
        for c in range(G // 2):
            out_ref[0, c] = jnp.concatenate([out[(2 * c) * N_SUB:(2 * c + 1) * N_SUB],
                                             out[(2 * c + 1) * N_SUB:(2 * c + 2) * N_SUB]], axis=1).astype(bf16)


def sample_compress(cache3, page_table, pe, w1, b1, w2, b2, kg):
    bsz = page_table.shape[0]

    def page_spec(j):
        return pl.BlockSpec((1, PAGE_SIZE, HALF), lambda b, pt: (pt[b, j], 0, 0))

    const = lambda shape: pl.BlockSpec(shape, lambda b, pt: (0,) * len(shape))
    out_spec = pl.BlockSpec((1, G // 2, N_SUB, LANES), lambda b, pt: (b, 0, 0, 0))
    grid_spec = pltpu.PrefetchScalarGridSpec(
        num_scalar_prefetch=1, grid=(bsz,),
        in_specs=[page_spec(j) for j in range(N_PAGES)] + [
            const((2, CMP_LEN, DH)), const((2, CMP_LEN, DH, CMP_HID)), const((2, CMP_HID)),
            const((2, CMP_HID, DH)), const((2, DH)), const((1, DH))],
        out_specs=[out_spec, out_spec],
        scratch_shapes=[pltpu.VMEM((HALF // LANES, PAST_LEN + CMP_STRIDE, LANES), f32)])
    return pl.pallas_call(
        _sample_cmp_kernel, grid_spec=grid_spec,
        out_shape=[jax.ShapeDtypeStruct((bsz, G // 2, N_SUB, LANES), bf16)] * 2,
        compiler_params=pltpu.CompilerParams(dimension_semantics=("arbitrary",), vmem_limit_bytes=VMEM_LIMIT),
        name="sample_compress",
    )(page_table, *([cache3] * N_PAGES), pe, w1.reshape(2, CMP_LEN, DH, CMP_HID).astype(bf16), b1,
      w2.astype(bf16), b2, kg)


T_NEW = 4
NEW_PAD = 128
NB_SAMPLE = -(-(PAST_LEN + T_NEW) // SEL_BLOCK)
N_CMP_SAMPLE = (PAST_LEN + T_NEW) // CMP_STRIDE - 1
ROWS_S = HPG * T_NEW


def _sample_attn_kernel(pt_ref, *refs):
    pages = refs[:N_PAGES]
    (q_ref, gt_ref, ck_ref, cv_ref, new_ref, win_ref, wnew_ref, impt_ref, oh_ref, ohn_ref, o_ref) = refs[N_PAGES:]
    rows = 2 * ROWS_S
    t_of_row = lax.broadcasted_iota(jnp.int32, (rows, LANES), 0) & (T_NEW - 1)
    lane = lax.broadcasted_iota(jnp.int32, (rows, LANES), 1)
    q_pos = PAST_LEN + t_of_row
    hs_o = lax.broadcasted_iota(jnp.int32, (SUBLANES, rows), 0)
    hs_r = lax.broadcasted_iota(jnp.int32, (SUBLANES, rows), 1)
    hsum = ((hs_o >> 2 == hs_r >> 4) & ((hs_o & 3) == (hs_r & 3))).astype(bf16)
    sp_r = lax.broadcasted_iota(jnp.int32, (rows, SUBLANES), 0)
    sp_o = lax.broadcasted_iota(jnp.int32, (rows, SUBLANES), 1)
    spread = ((sp_o >> 2 == sp_r >> 4) & ((sp_o & 3) == (sp_r & 3))).astype(bf16)
    blk8 = lax.broadcasted_iota(jnp.int32, (SUBLANES, LANES), 1)
    qp8 = PAST_LEN + (lax.broadcasted_iota(jnp.int32, (SUBLANES, LANES), 0) & (T_NEW - 1))
    qb8 = qp8 >> 6
    forced = (blk8 == 0) | (blk8 == qb8) | (blk8 == qb8 - 1)
    causal = (blk8 * SEL_BLOCK <= qp8) & (blk8 < NB_SAMPLE)
    n_buf = win_ref.shape[1]

    for c in range(G // 2):
        lanes_k = pl.ds(c * LANES, LANES)
        lanes_v = pl.ds(G * DH + c * LANES, LANES)
        q = q_ref[0, c]
        gt = gt_ref[0, c]
        s_c = _nt(q, ck_ref[0, c])
        ok_c = (lane * CMP_STRIDE + (CMP_LEN - 1) <= q_pos) & (lane < N_CMP_SAMPLE)
        p_c = _masked_softmax(s_c, ok_c)
        o_c = jnp.dot(p_c.astype(bf16), cv_ref[0, c], preferred_element_type=f32)
        hi, lo = _split_bf16(p_c)
        psum = jnp.dot(hsum, hi, preferred_element_type=f32) + jnp.dot(hsum, lo, preferred_element_type=f32)
        hi, lo = _split_bf16(psum)
        imp = _nt(hi, impt_ref[...]) + _nt(lo, impt_ref[...])
        score = jnp.where(blk8 < NB_SAMPLE,
                          jnp.where(causal, jnp.where(forced, FORCE_SCORE, imp), MASK_SCORE), PICKED)
        sel = _topk_by_rank(score, blk8, NB_SAMPLE) & causal
        bias8 = jnp.where(sel, 0.0, MASK_SCORE).astype(bf16)
        bias = jnp.dot(spread, bias8, preferred_element_type=f32).astype(bf16)
        k_all = jnp.concatenate([pages[j][0, :, lanes_k] for j in range(N_PAGES)], axis=0).astype(bf16)
        s_past = _nt(q, k_all) + jnp.dot(bias, oh_ref[...], preferred_element_type=f32)
        s_new = _nt(q, new_ref[0, :, lanes_k].astype(bf16)) + jnp.dot(bias, ohn_ref[...],
                                                                      preferred_element_type=f32)
        s_new = jnp.where(lane <= t_of_row, s_new, 2.0 * MASK_SCORE)
        s_all = jnp.concatenate([s_past, s_new], axis=1)
        e = jnp.exp(s_all - jnp.max(s_all, axis=1, keepdims=True))
        den = jnp.maximum(jnp.sum(e, axis=1, keepdims=True), 1e-30)
        v_all = jnp.concatenate([pages[j][0, :, lanes_v] for j in range(N_PAGES)] + [new_ref[0, :, lanes_v]],
                                axis=0).astype(bf16)
        o_s = jnp.dot(e.astype(bf16), v_all, preferred_element_type=f32) / den
        kw_all = jnp.concatenate([win_ref[0, :, lanes_k], wnew_ref[0, :, lanes_k]], axis=0).astype(bf16)
        idx = lax.broadcasted_iota(jnp.int32, (rows, n_buf), 1)
        t_w = lax.broadcasted_iota(jnp.int32, (rows, n_buf), 0) & (T_NEW - 1)
        w_pos = PAST_LEN - n_buf + idx
        dist = PAST_LEN + t_w - w_pos
        ok_w = jnp.concatenate([(dist >= 0) & (dist < WINDOW) & (w_pos >= 0), lane <= t_of_row], axis=1)
        p_w = _masked_softmax(_nt(q, kw_all), ok_w).astype(bf16)
        vw_all = jnp.concatenate([win_ref[0, :, lanes_v], wnew_ref[0, :, lanes_v]], axis=0).astype(bf16)
        o_w = jnp.dot(p_w, vw_all, preferred_element_type=f32)
        o_ref[0, c] = gt[:, 0:1] * o_c + gt[:, 1:2] * o_s + gt[:, 2:3] * o_w


def sample_attention(cache3, page_table, q2, gt2, ck2, cv2, new_sel, win_cache, new_win):
    bsz = page_table.shape[0]
    n_buf = win_cache.shape[1]
    rows = 2 * ROWS_S
    impt = _imp_map_t(N_SUB, LANES, N_CMP_SAMPLE, NB_SAMPLE)
    key_blk = np.arange(PAST_LEN) // SEL_BLOCK
    onehot = jnp.asarray(key_blk[None, :] == np.arange(LANES)[:, None], bf16)
    onehot_new = jnp.asarray(np.broadcast_to(np.arange(LANES)[:, None] == PAST_LEN // SEL_BLOCK,
                                             (LANES, NEW_PAD)), bf16)

    def page_spec(j):
        return pl.BlockSpec((1, PAGE_SIZE, HALF), lambda b, pt: (pt[b, j], 0, 1))

    per_seq = lambda shape: pl.BlockSpec((1,) + shape, lambda b, pt: (b,) + (0,) * len(shape))
    const = lambda shape: pl.BlockSpec(shape, lambda b, pt: (0,) * len(shape))
    grid_spec = pltpu.PrefetchScalarGridSpec(
        num_scalar_prefetch=1, grid=(bsz,),
        in_specs=[page_spec(j) for j in range(N_PAGES)] + [
            per_seq((G // 2, rows, LANES)), per_seq((G // 2, rows, 3)),
            per_seq((G // 2, N_SUB, LANES)), per_seq((G // 2, N_SUB, LANES)),
            per_seq((NEW_PAD, HALF)), per_seq((n_buf, HALF)), per_seq((NEW_PAD, HALF)),
            const((LANES, N_SUB)), const((LANES, PAST_LEN)), const((LANES, NEW_PAD))],
        out_specs=per_seq((G // 2, rows, LANES)))
    return pl.pallas_call(
        _sample_attn_kernel, grid_spec=grid_spec,
        out_shape=jax.ShapeDtypeStruct((bsz, G // 2, rows, LANES), f32),
        compiler_params=pltpu.CompilerParams(dimension_semantics=("arbitrary",), vmem_limit_bytes=VMEM_LIMIT),
        name="sample_attention",
    )(page_table, *([cache3] * N_PAGES), q2, gt2, ck2, cv2, new_sel, win_cache, new_win, impt, onehot, onehot_new)


def rmsnorm(x, g):
    y = x * lax.rsqrt(jnp.mean(x * x, -1, keepdims=True) + NORM_EPS)
    return y * g


def _ffn_in_kernel(x_ref, g_ref, w_ref, o_ref):
    x = x_ref[...]
    xn = x * lax.rsqrt(jnp.mean(x * x, axis=-1, keepdims=True) + NORM_EPS) * g_ref[...]
    o_ref[...] = jnp.dot(xn.astype(bf16), w_ref[...], preferred_element_type=f32)


def _ffn_out_kernel(x_ref, h_ref, halo_ref, cs_ref, cw_ref, cb_ref, w_ref, o_ref, u_scr, *, tiles_per_seq):
    first = pl.program_id(0) % tiles_per_seq == 0
    tm = h_ref.shape[0]
    u_scr[pl.ds(0, SUBLANES), :] = halo_ref[...]
    u_scr[pl.ds(SUBLANES, tm), :] = h_ref[:, pl.ds(0, D_FF)]

    @pl.when(first)
    def _():
        u_scr[pl.ds(SUBLANES - (CONV_W - 1), CONV_W - 1), :] = cs_ref[0]

    uc = cb_ref[...] + cw_ref[CONV_W - 1:CONV_W, :] * u_scr[pl.ds(SUBLANES, tm), :]
    for j in range(CONV_W - 1):
        uc = uc + cw_ref[j:j + 1, :] * u_scr[pl.ds(SUBLANES - (CONV_W - 1) + j, tm), :]
    act = _gelu(uc) * h_ref[:, pl.ds(D_FF, D_FF)]
    o_ref[...] = x_ref[...] + jnp.dot(act.astype(bf16), w_ref[...], preferred_element_type=f32)


FFN_TM = 512
FFN_TN = 1408
FFN_OUT_TM = 256


def conv_ffn_fused(x, g, conv_state, w_in, conv_w, conv_b, w_out):
    bsz, t_len, _ = x.shape
    m = bsz * t_len
    tiles_per_seq = t_len // FFN_OUT_TM
    x2 = x.reshape(m, D_MODEL)
    h = pl.pallas_call(
        _ffn_in_kernel,
        grid=(m // FFN_TM, 2 * D_FF // FFN_TN),
        in_specs=[pl.BlockSpec((FFN_TM, D_MODEL), lambda i, j: (i, 0)),
                  pl.BlockSpec((1, D_MODEL), lambda i, j: (0, 0)),
                  pl.BlockSpec((D_MODEL, FFN_TN), lambda i, j: (0, j))],
        out_specs=pl.BlockSpec((FFN_TM, FFN_TN), lambda i, j: (i, j)),
        out_shape=jax.ShapeDtypeStruct((m, 2 * D_FF), f32),
        compiler_params=pltpu.CompilerParams(dimension_semantics=("parallel", "parallel"),
                                             vmem_limit_bytes=VMEM_LIMIT),
        name="ffn_in",
    )(x2, g.reshape(1, D_MODEL), w_in.astype(bf16))
    halo_blocks = FFN_OUT_TM // SUBLANES
    y = pl.pallas_call(
        functools.partial(_ffn_out_kernel, tiles_per_seq=tiles_per_seq),
        grid=(m // FFN_OUT_TM,),
        in_specs=[pl.BlockSpec((FFN_OUT_TM, D_MODEL), lambda i: (i, 0)),
                  pl.BlockSpec((FFN_OUT_TM, 2 * D_FF), lambda i: (i, 0)),
                  pl.BlockSpec((SUBLANES, D_FF), lambda i: (jnp.maximum(i * halo_blocks - 1, 0), 0)),
                  pl.BlockSpec((1, CONV_W - 1, D_FF), lambda i: (i // tiles_per_seq, 0, 0)),
                  pl.BlockSpec((CONV_W, D_FF), lambda i: (0, 0)),
                  pl.BlockSpec((1, D_FF), lambda i: (0, 0)),
                  pl.BlockSpec((D_FF, D_MODEL), lambda i: (0, 0))],
        out_specs=pl.BlockSpec((FFN_OUT_TM, D_MODEL), lambda i: (i, 0)),
        out_shape=jax.ShapeDtypeStruct((m, D_MODEL), f32),
        scratch_shapes=[pltpu.VMEM((SUBLANES + FFN_OUT_TM, D_FF), f32)],
        compiler_params=pltpu.CompilerParams(dimension_semantics=("arbitrary",), vmem_limit_bytes=56 * 1024 * 1024),
        name="ffn_out",
    )(x2, h, h, conv_state, conv_w, conv_b.reshape(1, D_FF), w_out.astype(bf16))
    conv_new = h.reshape(bsz, t_len, 2 * D_FF)[:, t_len - (CONV_W - 1):, :D_FF]
    return y.reshape(bsz, t_len, D_MODEL), conv_new


def conv_ffn(xn, conv_state, w_in, conv_w, conv_b, w_out):
    t_len = xn.shape[1]
    u, gate = jnp.split(mm3(xn, w_in), 2, axis=-1)
    u_pad = jnp.concatenate([conv_state, u], axis=1)
    uc = conv_b + conv_w[0] * u_pad[:, 0:t_len]
    for j in range(1, CONV_W):
        uc = uc + conv_w[j] * u_pad[:, j:j + t_len]
    return mm3(jax.nn.gelu(uc) * gate, w_out), u_pad[:, t_len:]


RW_TM = 256
SEG_W = LANES


def _head_seg():
    seg = np.arange(D_MODEL)[:, None] // RW_HEAD_DIM == np.arange(SEG_W)[None, :]
    return jnp.asarray(seg, bf16), jnp.asarray(seg.T, bf16)


def _head_sum(x, seg_ref):
    hi, lo = _split_bf16(x)
    return jnp.dot(hi, seg_ref[...], preferred_element_type=f32) + jnp.dot(lo, seg_ref[...],
                                                                           preferred_element_type=f32)


def _head_expand(s, segt_ref):
    hi, lo = _split_bf16(s)
    return jnp.dot(hi, segt_ref[...], preferred_element_type=f32) + jnp.dot(lo, segt_ref[...],
                                                                            preferred_element_type=f32)


def _rwkv_pre_kernel(*refs, has_vfirst):
    it = iter(refs)
    xn_ref, xs_ref = next(it), next(it)
    vf_ref = next(it) if has_vfirst else None
    mu_ref, wr, wk, wv, w1, w2, a1, a2 = (next(it) for _ in range(8))
    v1, v2 = (next(it), next(it)) if has_vfirst else (None, None)
    g1, g2, w0, a0 = (next(it) for _ in range(4))
    v0 = next(it) if has_vfirst else None
    kk_ref, ka_ref, seg_ref, segt_ref = (next(it) for _ in range(4))
    r_o, w_o, k_o, v_o, nkk_o, kka_o, g_o = (next(it) for _ in range(7))

    dot = lambda x, w: jnp.dot(x.astype(bf16), w[...], preferred_element_type=f32)
    xn = xn_ref[...]
    xx = xs_ref[...] - xn
    xr, xw, xk, xv, xa, xg = (xn + xx * mu_ref[i:i + 1, :] for i in range(6))
    r = dot(xr, wr)
    k = dot(xk, wk)
    v = dot(xv, wv)
    z = -(w0[...] + dot(jnp.tanh(dot(xw, w1)), w2))
    softplus = jnp.maximum(z, 0.0) + jnp.log1p(jnp.exp(-jnp.abs(z)))
    decay = jnp.exp(-jnp.exp(-softplus - 0.5))
    a = jax.nn.sigmoid(a0[...] + dot(dot(xa, a1), a2))
    if has_vfirst:
        v = v + (vf_ref[...] - v) * jax.nn.sigmoid(v0[...] + dot(dot(xv, v1), v2))
    g = dot(jax.nn.sigmoid(dot(xg, g1)), g2)
    kk = k * kk_ref[...]
    norm = jnp.maximum(jnp.sqrt(_head_sum(kk * kk, seg_ref)), 1e-12)
    kk = kk * _head_expand(1.0 / norm, segt_ref)
    r_o[...] = r
    w_o[...] = decay
    k_o[...] = k * (1.0 + (a - 1.0) * ka_ref[...])
    v_o[...] = v
    nkk_o[...] = -kk
    kka_o[...] = kk * a
    g_o[...] = g


def rwkv_pre(xn, xs, v_first, li, p):
    m = xn.shape[0]
    tm = RW_TM if m % RW_TM == 0 else m
    has_vfirst = v_first is not None
    seg, segt = _head_seg()
    row = lambda a: a.reshape(1, -1)
    cast = lambda w: w.astype(bf16)
    ops = [xn, xs] + ([v_first] if has_vfirst else [])
    ops += [p['rw_mu'][li], cast(p['rw_w_r'][li]), cast(p['rw_w_k'][li]), cast(p['rw_w_v'][li]),
            cast(p['rw_w1'][li]), cast(p['rw_w2'][li]), cast(p['rw_a1'][li]), cast(p['rw_a2'][li])]
    if has_vfirst:
        ops += [cast(p['rw_v1'][li - 1]), cast(p['rw_v2'][li - 1])]
    ops += [cast(p['rw_g1'][li]), cast(p['rw_g2'][li]), row(p['rw_w0'][li]), row(p['rw_a0'][li])]
    if has_vfirst:
        ops += [row(p['rw_v0'][li - 1])]
    ops += [row(p['rw_k_k'][li]), row(p['rw_k_a'][li]), seg, segt]
    n_rows = 3 if has_vfirst else 2
    tile = pl.BlockSpec((tm, D_MODEL), lambda i: (i, 0))
    whole = lambda a: pl.BlockSpec(a.shape, lambda i: (0,) * a.ndim)
    return pl.pallas_call(
        functools.partial(_rwkv_pre_kernel, has_vfirst=has_vfirst),
        grid=(m // tm,),
        in_specs=[tile] * n_rows + [whole(a) for a in ops[n_rows:]],
        out_specs=[tile] * 7,
        out_shape=[jax.ShapeDtypeStruct((m, D_MODEL), f32)] * 7,
        compiler_params=pltpu.CompilerParams(dimension_semantics=("parallel",), vmem_limit_bytes=VMEM_LIMIT),
        name="rwkv_pre",
    )(*ops)


def _rwkv_post_kernel(y_ref, r_ref, k_ref, v_ref, g_ref, x_ref, lnw, lnb, rk, wo, seg_ref, segt_ref, o_ref):
    y = y_ref[...]
    inv_n = 1.0 / RW_HEAD_DIM
    yc = y - _head_expand(_head_sum(y, seg_ref) * inv_n, segt_ref)
    var = _head_expand(_head_sum(yc * yc, seg_ref) * inv_n, segt_ref)
    yn = yc * lax.rsqrt(var + RW_GN_EPS) * lnw[...] + lnb[...]
    bonus = _head_expand(_head_sum(r_ref[...] * k_ref[...] * rk[...], seg_ref), segt_ref) * v_ref[...]
    out = ((yn + bonus) * g_ref[...]).astype(bf16)
    o_ref[...] = x_ref[...] + jnp.dot(out, wo[...], preferred_element_type=f32)


def rwkv_post(y, r, k, v, g, x, li, p):
    m = y.shape[0]
    tm = RW_TM if m % RW_TM == 0 else m
    seg, segt = _head_seg()
    row = lambda a: a.reshape(1, -1)
    consts = [row(p['rw_ln_w'][li]), row(p['rw_ln_b'][li]), row(p['rw_r_k'][li]), p['rw_w_o'][li].astype(bf16),
              seg, segt]
    tile = pl.BlockSpec((tm, D_MODEL), lambda i: (i, 0))
    whole = lambda a: pl.BlockSpec(a.shape, lambda i: (0,) * a.ndim)
    return pl.pallas_call(
        _rwkv_post_kernel,
        grid=(m // tm,),
        in_specs=[tile] * 6 + [whole(a) for a in consts],
        out_specs=tile,
        out_shape=jax.ShapeDtypeStruct((m, D_MODEL), f32),
        compiler_params=pltpu.CompilerParams(dimension_semantics=("parallel",), vmem_limit_bytes=VMEM_LIMIT),
        name="rwkv_post",
    )(y, r, k, v, g, x, *consts)


def rwkv7_layer(x, shift_prev, s0, v_first, li, p, scan_cfg):
    bsz, t_len, _ = x.shape
    m = bsz * t_len
    xn = rmsnorm(x, p['norm_g'][li, 0])
    xs = jnp.concatenate([shift_prev[:, None], xn[:, :-1]], axis=1)
    flat = lambda a: a.reshape(m, D_MODEL)
    r, decay, k, v, nkk, kka, g = rwkv_pre(flat(xn), flat(xs), None if li == 0 else flat(v_first), li, p)
    if li == 0:
        v_first = v
    seq = lambda a: a.reshape(bsz, t_len, D_MODEL)
    y, s_new = wkv_scan(seq(r), seq(decay), seq(k), seq(v), seq(nkk), seq(kka), to_pairs(s0), **scan_cfg)
    x_new = rwkv_post(flat(y), r, k, v, g, flat(x), li, p)
    return seq(x_new), from_pairs(s_new, bsz), xn[:, -1], seq(v_first)


def kv_rows(h, p):
    bsz, t_len, _ = h.shape
    kv = mm3(rmsnorm(h, p['kv_norm_g']), p['w_kv']).reshape(bsz, t_len, 6, G, DH)
    k_sel = rmsnorm(kv[:, :, 2], p['k_norm_g'][1])
    k_win = rmsnorm(kv[:, :, 4], p['k_norm_g'][2])
    paged = jnp.stack([kv[:, :, 0], kv[:, :, 1], k_sel, kv[:, :, 3]], axis=2)
    win = jnp.stack([k_win, kv[:, :, 5]], axis=2)
    return paged, win


def nsa_query(xn, lb, p):
    bsz, t_len, _ = xn.shape
    n_q = NSA_HEADS * DH
    w = jnp.pad(p['nsa_w_q'][lb], ((0, 0), (0, LANES - 3 * NSA_HEADS)))
    proj = mm3(xn, w)
    q = proj[..., :n_q].reshape(bsz, t_len, G, HPG, DH)
    q = rmsnorm(q, p['q_norm_g'][lb]) * (DH ** -0.5)
    gates = jax.nn.sigmoid(proj[..., n_q:n_q + 3 * NSA_HEADS]).reshape(bsz, t_len, G, HPG, 3)
    return q, gates


def compress_rows_prompt(rows, pe, w1, b1, w2, b2):
    bsz, t_len = rows.shape[:2]
    n_sub = t_len // CMP_STRIDE
    sub = rows.reshape(bsz, n_sub, CMP_STRIDE, G, DH).transpose(0, 1, 3, 2, 4)
    half = CMP_STRIDE * DH
    x0 = (sub + pe[:CMP_STRIDE][None, None, None]).reshape(bsz * n_sub * G, half)
    x1 = (sub + pe[CMP_STRIDE:][None, None, None]).reshape(bsz * n_sub * G, half)
    p0 = mm(x0, w1[:half]).reshape(bsz, n_sub, G, CMP_HID)
    p1 = mm(x1, w1[half:]).reshape(bsz, n_sub, G, CMP_HID)
    hid = jax.nn.gelu(p0[:, :-1] + p1[:, 1:] + b1)
    out = mm(hid.reshape(bsz * (n_sub - 1) * G, CMP_HID), w2) + b2
    return out.reshape(bsz, n_sub - 1, G, DH)


def to_q_tiles(x, n_q):
    bsz, _, _, _, w = x.shape
    x = x.reshape(bsz, n_q, Q_BLOCK, G, HPG, w).transpose(0, 3, 1, 4, 2, 5)
    return x.reshape(bsz, G, n_q, HPG * Q_BLOCK, w)


def from_q_tiles(x, t_len):
    bsz, _, n_q, _, w = x.shape
    x = x.reshape(bsz, G, n_q, HPG, Q_BLOCK, w).transpose(0, 2, 4, 1, 3, 5)
    return x.reshape(bsz, t_len, G * HPG * w)


def prompt_context(h, p):
    paged, win = kv_rows(h, p)
    bsz, t_len = h.shape[:2]
    ck = rmsnorm(compress_rows_prompt(paged[:, :, 0], p['cmp_pe'][0], p['cmp_w1'][0], p['cmp_b1'][0],
                                      p['cmp_w2'][0], p['cmp_b2'][0]), p['k_norm_g'][0])
    cv = compress_rows_prompt(paged[:, :, 1], p['cmp_pe'][1], p['cmp_w1'][1], p['cmp_b1'][1],
                              p['cmp_w2'][1], p['cmp_b2'][1])
    n_cmp = ck.shape[1]
    pad = lambda c: jnp.pad(c, ((0, 0), (0, NC_PAD - n_cmp), (0, 0), (0, 0))).transpose(0, 2, 1, 3).astype(bf16)
    heads = lambda x: x.transpose(0, 2, 1, 3).astype(bf16)
    onehot = jnp.asarray(np.arange(t_len)[:, None] // SEL_BLOCK == np.arange(NB_PROMPT)[None, :], bf16)
    ke = jnp.concatenate([jnp.broadcast_to(onehot, (bsz, G, t_len, NB_PROMPT)), heads(paged[:, :, 2])], axis=-1)
    heads_t = lambda x: x.transpose(0, 2, 3, 1).astype(bf16)
    ctx = dict(ck=pad(ck), cv=pad(cv), n_cmp=n_cmp, ke=ke, vst=heads_t(paged[:, :, 3]),
               kw=heads(win[:, :, 0]), vwt=heads_t(win[:, :, 1]))
    keep = min(WINDOW, t_len)
    return ctx, (paged, win[:, t_len - keep:])


def attend_prompt(q, gates, ctx):
    bsz, t_len = q.shape[:2]
    n_q = t_len // Q_BLOCK
    qh = to_q_tiles(q, n_q).astype(bf16)
    gt = to_q_tiles(gates, n_q)
    oc, sb = nsa_cmp_select(qh, ctx['ck'], ctx['cv'], ctx['n_cmp'])
    qe = jnp.concatenate([jnp.tile(sb, (1, 1, 1, HPG, 1)), qh], axis=-1)
    o = nsa_sel_win(qe, ctx['ke'], ctx['vst'], ctx['kw'], ctx['vwt'], oc, gt)
    return from_q_tiles(o, t_len)


def sample_context(h, cache3, page_table, cache_kv_win, p):
    paged, win = kv_rows(h, p)
    bsz, t_new = h.shape[:2]
    ck, cv = sample_compress(cache3, page_table, p['cmp_pe'], p['cmp_w1'], p['cmp_b1'], p['cmp_w2'], p['cmp_b2'],
                             p['k_norm_g'][0:1])
    n_buf = cache_kv_win.shape[1]
    pad_rows = lambda x: jnp.pad(x.reshape(bsz, t_new, HALF), ((0, 0), (0, NEW_PAD - t_new), (0, 0)))
    ctx = dict(ck=ck, cv=cv, new_sel=pad_rows(paged[:, :, 2:4]), new_win=pad_rows(win),
               win_cache=cache_kv_win.reshape(bsz, n_buf, HALF))
    win_full = jnp.concatenate([cache_kv_win, win], axis=1)
    return ctx, (paged, win_full[:, win_full.shape[1] - n_buf:])


def attend_sample(q, gates, ctx, cache3, page_table):
    bsz, t_new = q.shape[:2]
    rows = lambda x: x.transpose(0, 2, 3, 1, 4).reshape(bsz, G // 2, 2 * HPG * t_new, x.shape[-1])
    qr = rows(q)
    own = (jnp.arange(2 * HPG * t_new) // (HPG * t_new))[:, None] == (jnp.arange(LANES) // DH)[None, :]
    q2 = jnp.where(own, jnp.concatenate([qr, qr], axis=-1), 0.0).astype(bf16)
    o = sample_attention(cache3, page_table, q2, rows(gates), ctx['ck'], ctx['cv'],
                         ctx['new_sel'], ctx['win_cache'], ctx['new_win'])
    o = jnp.where(own, o, 0.0)
    o = o[..., :DH] + o[..., DH:]
    return o.reshape(bsz, G, HPG, t_new, DH).transpose(0, 3, 1, 2, 4).reshape(bsz, t_new, NSA_HEADS * DH)


def forward(x, wkv_in, shift_in, conv_in, make_ctx, attend, p, scan_cfg):
    wkv_out, shift_out, conv_out = [], [], []
    v_first = None
    ctx = None
    kv_state = None
    for layer in range(DEPTH):
        if layer == N_A_LAYERS:
            ctx, kv_state = make_ctx(x)
        if layer < N_A_LAYERS:
            x, s_new, sh_new, v_first = rwkv7_layer(x, shift_in[layer], wkv_in[layer], v_first, layer, p, scan_cfg)
            wkv_out.append(s_new)
            shift_out.append(sh_new)
        else:
            lb = layer - N_A_LAYERS
            q, gates = nsa_query(rmsnorm(x, p['norm_g'][layer, 0]), lb, p)
            x = x + mm3(attend(q, gates, ctx), p['nsa_w_o'][lb])
        ffn_w = (p['ffn_w_in'][layer], p['ffn_conv_w'][layer], p['ffn_conv_b'][layer], p['ffn_w_out'][layer])
        if x.shape[1] % FFN_TM == 0:
            x, conv_new = conv_ffn_fused(x, p['norm_g'][layer, 1], conv_in[layer], *ffn_w)
        else:
            f, conv_new = conv_ffn(rmsnorm(x, p['norm_g'][layer, 1]), conv_in[layer], *ffn_w)
            x = x + f
        conv_out.append(conv_new)
    return x, kv_state, jnp.stack(wkv_out), jnp.stack(shift_out), jnp.stack(conv_out)


def kernel(x_prompt, x_sample, cache_kv, cache_kv_win, state_wkv, state_shift, state_ffn_conv, page_table, norm_g, rw_mu, rw_w_r, rw_w_k, rw_w_v, rw_w_o, rw_w0, rw_w1, rw_w2, rw_a0, rw_a1, rw_a2, rw_v0, rw_v1, rw_v2, rw_g1, rw_g2, rw_k_k, rw_k_a, rw_r_k, rw_ln_w, rw_ln_b, kv_norm_g, w_kv, k_norm_g, cmp_pe, cmp_w1, cmp_b1, cmp_w2, cmp_b2, nsa_w_q, q_norm_g, nsa_w_o, ffn_w_in, ffn_conv_w, ffn_conv_b, ffn_w_out):
    p = dict(norm_g=norm_g, rw_mu=rw_mu, rw_w_r=rw_w_r, rw_w_k=rw_w_k, rw_w_v=rw_w_v, rw_w_o=rw_w_o,
             rw_w0=rw_w0, rw_w1=rw_w1, rw_w2=rw_w2, rw_a0=rw_a0, rw_a1=rw_a1, rw_a2=rw_a2,
             rw_v0=rw_v0, rw_v1=rw_v1, rw_v2=rw_v2, rw_g1=rw_g1, rw_g2=rw_g2, rw_k_k=rw_k_k, rw_k_a=rw_k_a,
             rw_r_k=rw_r_k, rw_ln_w=rw_ln_w, rw_ln_b=rw_ln_b, kv_norm_g=kv_norm_g, w_kv=w_kv,
             k_norm_g=k_norm_g, cmp_pe=cmp_pe, cmp_w1=cmp_w1, cmp_b1=cmp_b1, cmp_w2=cmp_w2, cmp_b2=cmp_b2,
             nsa_w_q=nsa_w_q, q_norm_g=q_norm_g, nsa_w_o=nsa_w_o, ffn_w_in=ffn_w_in,
             ffn_conv_w=ffn_conv_w, ffn_conv_b=ffn_conv_b, ffn_w_out=ffn_w_out)
    bp, t_p = x_prompt.shape[:2]
    bs = x_sample.shape[0]

    wkv0 = jnp.zeros((N_A_LAYERS, bp, RW_HEADS, RW_HEAD_DIM, RW_HEAD_DIM), f32)
    shift0 = jnp.zeros((N_A_LAYERS, bp, D_MODEL), f32)
    conv0 = jnp.zeros((DEPTH, bp, CONV_W - 1, D_FF), f32)
    y_prompt, (kv_prompt, win_prompt), wkv_prompt, shift_prompt, conv_prompt = forward(
        x_prompt, wkv0, shift0, conv0, lambda h: prompt_context(h, p), attend_prompt, p,
        dict(bb=bp, tc=128))

    cache3 = cache_kv.reshape(cache_kv.shape[0], PAGE_SIZE, 4 * G * DH)
    y_sample, (kv_sample, win_sample), wkv_sample, shift_sample, conv_sample = forward(
        x_sample, state_wkv, state_shift, state_ffn_conv,
        lambda h: sample_context(h, cache3, page_table, cache_kv_win, p),
        lambda q, gates, ctx: attend_sample(q, gates, ctx, cache3, page_table), p,
        dict(bb=2, tc=x_sample.shape[1]))

    return (y_prompt, y_sample, kv_prompt, kv_sample, win_prompt, win_sample,
            wkv_prompt, wkv_sample, shift_prompt, shift_sample, conv_prompt, conv_sample)
```

```python
import functools

import numpy as np
import jax
import jax.numpy as jnp
from jax import lax
from jax.experimental import pallas as pl
from jax.experimental.pallas import tpu as pltpu

f32 = jnp.float32
bf16 = jnp.bfloat16

D_MODEL = 1024
DEPTH = 4
N_A_LAYERS = 2
PAST_LEN = 2048
PAGE_SIZE = 128
N_PAGES = PAST_LEN // PAGE_SIZE
RW_HEAD_DIM = 64
RW_HEADS = 16
RW_GN_EPS = 64e-5
DH = 64
NSA_HEADS = 16
G = 4
HPG = 4
CMP_LEN = 32
CMP_STRIDE = 16
CMP_HID = 128
SEL_BLOCK = 64
N_SELECT = 16
WINDOW = 512
Q_BLOCK = 128
D_FF = 2816
CONV_W = 3
NORM_EPS = 1e-6
NEG_INF = -1e30
FORCE_SCORE = 1e9
MASK_SCORE = -1e9
PICKED = -3e38

LANES = 128
SUBLANES = 8
VMEM_LIMIT = 48 * 1024 * 1024


def _nt(a, b):
    return lax.dot_general(a, b, (((1,), (1,)), ((), ())), preferred_element_type=f32)


def _split_bf16(x):
    hi = x.astype(bf16)
    lo = (x - hi.astype(f32)).astype(bf16)
    return hi, lo


def _mm_kernel(x_ref, w_ref, o_ref):
    o_ref[...] = jnp.dot(x_ref[...].astype(bf16), w_ref[...], preferred_element_type=f32)


MM_TM = 512


def _pick_tile(n, cands):
    for c in cands:
        if n % c == 0:
            return c
    return n


def mm(x, w):
    m, k = x.shape
    n = w.shape[1]
    w = w.astype(bf16)
    tm = MM_TM if m >= MM_TM else -(-m // SUBLANES) * SUBLANES
    mp = -(-m // tm) * tm
    if mp != m:
        x = jnp.pad(x, ((0, mp - m), (0, 0)))
    tn = n if n <= 1536 else _pick_tile(n, (1408, 1024, 512, 256, 128))
    out = pl.pallas_call(
        _mm_kernel,
        grid=(mp // tm, n // tn),
        in_specs=[pl.BlockSpec((tm, k), lambda i, j: (i, 0)),
                  pl.BlockSpec((k, tn), lambda i, j: (0, j))],
        out_specs=pl.BlockSpec((tm, tn), lambda i, j: (i, j)),
        out_shape=jax.ShapeDtypeStruct((mp, n), f32),
        compiler_params=pltpu.CompilerParams(dimension_semantics=("parallel", "parallel"),
                                             vmem_limit_bytes=VMEM_LIMIT),
        name="mm",
    )(x, w)
    return out[:m] if mp != m else out


def mm3(x, w):
    b, t, k = x.shape
    return mm(x.reshape(b * t, k), w).reshape(b, t, w.shape[1])


PAIR_W = 2 * RW_HEAD_DIM
N_PAIRS = RW_HEADS // 2


def _scan_kernel(r_ref, w_ref, k_ref, v_ref, a_ref, b_ref, s0_ref, y_ref, s_ref, *, bb, tc):
    n = RW_HEAD_DIM
    npairs = bb * N_PAIRS

    @pl.when(pl.program_id(1) == 0)
    def _():
        s_ref[...] = s0_ref[...]

    row = lax.broadcasted_iota(jnp.int32, (n, PAIR_W), 0)
    lane = lax.broadcasted_iota(jnp.int32, (n, PAIR_W), 1)
    diag = (lane % n) == row
    kk = lax.broadcasted_iota(jnp.int32, (PAIR_W, PAIR_W), 0) // n
    ll = lax.broadcasted_iota(jnp.int32, (PAIR_W, PAIR_W), 1) // n
    bones = (kk == ll).astype(bf16)
    group = min(tc, SUBLANES)

    def seg_sum_bcast(x):
        return jnp.dot(x.astype(bf16), bones, preferred_element_type=f32)

    def token(t0, s):
        def rowvec(ref, bi, hp):
            tile = ref[bi, pl.ds(t0, group), pl.ds(hp * PAIR_W, PAIR_W)]
            return jnp.broadcast_to(tile[s:s + 1], (n, PAIR_W))

        pairs = [(bi, hp) for bi in range(bb) for hp in range(N_PAIRS)]
        st = [s_ref[p] for p in range(npairs)]
        sa_in = jnp.concatenate([st[p] * rowvec(a_ref, *pairs[p]) for p in range(npairs)], axis=0)
        v_in = jnp.concatenate([jnp.where(diag, rowvec(v_ref, *pairs[p]), 0.0) for p in range(npairs)], axis=0)
        sa_col = seg_sum_bcast(sa_in)
        v_col = seg_sum_bcast(v_in)
        new = []
        for p in range(npairs):
            sl = slice(p * n, (p + 1) * n)
            sn = (st[p] * rowvec(w_ref, *pairs[p]) + sa_col[sl] * rowvec(b_ref, *pairs[p])
                  + v_col[sl] * rowvec(k_ref, *pairs[p]))
            s_ref[p] = sn
            new.append(sn)
        y_in = jnp.concatenate([new[p] * rowvec(r_ref, *pairs[p]) for p in range(npairs)], axis=0)
        y_col = seg_sum_bcast(y_in)
        rows = []
        for p in range(npairs):
            sl = slice(p * n, (p + 1) * n)
            rows.append(jnp.sum(jnp.where(diag, y_col[sl], 0.0), axis=0, keepdims=True))
        return rows

    def block(g, carry):
        t0 = pl.multiple_of(g * group, group)
        rows = [token(t0, s) for s in range(group)]
        for p in range(npairs):
            bi, hp = divmod(p, N_PAIRS)
            y_ref[bi, pl.ds(t0, group), pl.ds(hp * PAIR_W, PAIR_W)] = jnp.concatenate(
                [rows[s][p] for s in range(group)], axis=0)
        return carry

    lax.fori_loop(0, tc // group, block, 0)


def wkv_scan(r, w, k, v, a, b, s0_pairs, *, bb, tc):
    bsz, t_len, _ = r.shape
    assert bsz % bb == 0 and t_len % tc == 0
    vec_spec = pl.BlockSpec((bb, tc, D_MODEL), lambda i, j: (i, j, 0))
    st_spec = pl.BlockSpec((bb * N_PAIRS, RW_HEAD_DIM, PAIR_W), lambda i, j: (i, 0, 0))
    return pl.pallas_call(
        functools.partial(_scan_kernel, bb=bb, tc=tc),
        grid=(bsz // bb, t_len // tc),
        in_specs=[vec_spec] * 6 + [st_spec],
        out_specs=[vec_spec, st_spec],
        out_shape=[jax.ShapeDtypeStruct((bsz, t_len, D_MODEL), f32),
                   jax.ShapeDtypeStruct((bsz * N_PAIRS, RW_HEAD_DIM, PAIR_W), f32)],
        compiler_params=pltpu.CompilerParams(dimension_semantics=("arbitrary", "arbitrary"),
                                             vmem_limit_bytes=VMEM_LIMIT),
        name="wkv_scan",
    )(r, w, k, v, a, b, s0_pairs)


def to_pairs(s):
    bsz = s.shape[0]
    return s.reshape(bsz, N_PAIRS, 2, RW_HEAD_DIM, RW_HEAD_DIM).transpose(0, 1, 3, 2, 4).reshape(
        bsz * N_PAIRS, RW_HEAD_DIM, PAIR_W)


def from_pairs(s, bsz):
    return s.reshape(bsz, N_PAIRS, RW_HEAD_DIM, 2, RW_HEAD_DIM).transpose(0, 1, 3, 2, 4).reshape(
        bsz, RW_HEADS, RW_HEAD_DIM, RW_HEAD_DIM)


def _masked_softmax(s, mask):
    s = jnp.where(mask, s, NEG_INF)
    e = jnp.where(mask, jnp.exp(s - jnp.max(s, axis=-1, keepdims=True)), 0.0)
    return e / jnp.maximum(jnp.sum(e, axis=-1, keepdims=True), 1e-30)


def _topk_rounds(score, idx, axis):
    sel = jnp.zeros(score.shape, f32)
    idx = idx.astype(f32)
    for _ in range(N_SELECT):
        m = jnp.max(score, axis=axis, keepdims=True)
        first = jnp.min(jnp.where(score == m, idx, 1e6), axis=axis, keepdims=True)
        pick = idx == first
        sel = jnp.where(pick, 1.0, sel)
        score = jnp.where(pick, PICKED, score)
    return sel > 0.5


def _topk_by_rank(score, idx, n_valid):
    rank = jnp.zeros(score.shape, f32)
    for i in range(n_valid):
        ci = score[:, i:i + 1]
        beats = (ci > score) | ((ci == score) & (idx > i))
        rank = rank + jnp.where(beats, 1.0, 0.0)
    return rank < (N_SELECT - 0.5)


def _imp_map_t(n_cmp_pad, n_blk_pad, n_cmp, n_blk):
    ci = np.arange(n_cmp_pad)[None, :] * CMP_STRIDE
    sj = np.arange(n_blk_pad)[:, None] * SEL_BLOCK
    ov = np.clip(np.minimum(ci + CMP_LEN, sj + SEL_BLOCK) - np.maximum(ci, sj), 0, None) / CMP_STRIDE
    ov = ov * (np.arange(n_cmp_pad)[None, :] < n_cmp) * (np.arange(n_blk_pad)[:, None] < n_blk)
    return jnp.asarray(ov, bf16)


NC_PAD = 512
NB_PROMPT = 128


def _cmp_select_kernel(q_ref, ck_ref, cv_ref, impt_ref, oc_ref, sb_ref, *, n_cmp):
    qi = pl.program_id(2)
    q = q_ref[0, 0, 0]
    s = _nt(q, ck_ref[0, 0])
    rows = HPG * Q_BLOCK
    t_loc = lax.broadcasted_iota(jnp.int32, (rows, NC_PAD), 0) & (Q_BLOCK - 1)
    q_pos = qi * Q_BLOCK + t_loc
    nn = lax.broadcasted_iota(jnp.int32, (rows, NC_PAD), 1)
    valid = (nn * CMP_STRIDE + (CMP_LEN - 1) <= q_pos) & (nn < n_cmp)
    p = _masked_softmax(s, valid)
    oc_ref[0, 0, 0] = jnp.dot(p.astype(bf16), cv_ref[0, 0], preferred_element_type=f32)
    psum = p[0:Q_BLOCK]
    for h in range(1, HPG):
        psum = psum + p[h * Q_BLOCK:(h + 1) * Q_BLOCK]
    hi, lo = _split_bf16(psum)
    imp_t = _nt(impt_ref[...], hi) + _nt(impt_ref[...], lo)
    blk = lax.broadcasted_iota(jnp.int32, (NB_PROMPT, Q_BLOCK), 0)
    qp = qi * Q_BLOCK + lax.broadcasted_iota(jnp.int32, (NB_PROMPT, Q_BLOCK), 1)
    qb = qp >> 6
    forced = (blk == 0) | (blk == qb) | (blk == qb - 1)
    causal = blk * SEL_BLOCK <= qp
    score = jnp.where(causal, jnp.where(forced, FORCE_SCORE, imp_t), MASK_SCORE)
    sel = _topk_rounds(score, blk, 0) & causal
    bias_t = jnp.where(sel, 0.0, MASK_SCORE)
    sb_ref[0, 0, 0] = bias_t.T.astype(bf16)


def nsa_cmp_select(qh, ck, cv, n_cmp):
    bsz, _, n_q = qh.shape[:3]
    rows = HPG * Q_BLOCK
    impt = _imp_map_t(NC_PAD, NB_PROMPT, n_cmp, NB_PROMPT)
    return pl.pallas_call(
        functools.partial(_cmp_select_kernel, n_cmp=n_cmp),
        grid=(bsz, G, n_q),
        in_specs=[pl.BlockSpec((1, 1, 1, rows, DH), lambda b, g, i: (b, g, i, 0, 0)),
                  pl.BlockSpec((1, 1, NC_PAD, DH), lambda b, g, i: (b, g, 0, 0)),
                  pl.BlockSpec((1, 1, NC_PAD, DH), lambda b, g, i: (b, g, 0, 0)),
                  pl.BlockSpec((NB_PROMPT, NC_PAD), lambda b, g, i: (0, 0))],
        out_specs=[pl.BlockSpec((1, 1, 1, rows, DH), lambda b, g, i: (b, g, i, 0, 0)),
                   pl.BlockSpec((1, 1, 1, Q_BLOCK, NB_PROMPT), lambda b, g, i: (b, g, i, 0, 0))],
        out_shape=[jax.ShapeDtypeStruct((bsz, G, n_q, rows, DH), f32),
                   jax.ShapeDtypeStruct((bsz, G, n_q, Q_BLOCK, NB_PROMPT), bf16)],
        compiler_params=pltpu.CompilerParams(dimension_semantics=("parallel", "parallel", "parallel"),
                                             vmem_limit_bytes=VMEM_LIMIT),
        name="nsa_cmp_select",
    )(qh, ck, cv, impt)


KEXT = NB_PROMPT + DH
SEL_KT = 4 * Q_BLOCK
NQT = 1
QSPAN = NQT * Q_BLOCK


def _sel_win_kernel(qe_ref, ke_ref, vst_ref, kw_ref, vwt_ref, oc_ref, gt_ref, o_ref, s_scr, p_scr):
    qi = pl.program_id(2)
    trows = HPG * Q_BLOCK
    rows = NQT * trows
    q0 = qi * QSPAN
    qe = qe_ref[0, 0].reshape(rows, KEXT)
    q = qe[:, NB_PROMPT:]
    init = (jnp.full((1, rows), NEG_INF, f32), jnp.zeros((1, rows), f32), jnp.zeros((DH, rows), f32))

    def q_pos(n_keys):
        r = lax.broadcasted_iota(jnp.int32, (n_keys, rows), 1)
        return q0 + ((r >> (trows.bit_length() - 1)) << (Q_BLOCK.bit_length() - 1)) + (r & (Q_BLOCK - 1))

    def sel_scores(off):
        return _nt(ke_ref[0, 0, pl.ds(off, SEL_KT), :], qe)

    def pv(j, p):
        off = pl.multiple_of(jnp.maximum(j, 0) * SEL_KT, SEL_KT)
        return jnp.dot(vst_ref[0, 0, :, pl.ds(off, SEL_KT)], p, preferred_element_type=f32)

    def softmax_step(s, m, l):
        m_new = jnp.maximum(m, jnp.max(s, axis=0, keepdims=True))
        alpha = jnp.exp(m - m_new)
        p = jnp.exp(s - m_new)
        return m_new, alpha * l + jnp.sum(p, axis=0, keepdims=True), alpha, p.astype(bf16)

    n_full = q0 // SEL_KT
    s_scr[...] = sel_scores(0)
    p_scr[...] = jnp.zeros((SEL_KT, rows), bf16)

    def sel_body(j, carry):
        m, l, acc, alpha_prev = carry
        s_next = sel_scores(pl.multiple_of((j + 1) * SEL_KT, SEL_KT))
        acc = alpha_prev * acc + pv(j - 1, p_scr[...])
        m, l, alpha, p = softmax_step(s_scr[...], m, l)
        p_scr[...] = p
        s_scr[...] = s_next
        return m, l, acc, alpha

    m, l, acc, alpha_prev = lax.fori_loop(0, n_full, sel_body, init + (jnp.ones((1, rows), f32),))
    acc = alpha_prev * acc + pv(n_full - 1, p_scr[...])
    off = pl.multiple_of(n_full * SEL_KT, SEL_KT)
    key_pos = off + lax.broadcasted_iota(jnp.int32, (SEL_KT, rows), 0)
    s = jnp.where(key_pos <= q_pos(SEL_KT), s_scr[...], 2.0 * MASK_SCORE)
    _, l_s, alpha, p = softmax_step(s, m, l)
    acc_s = alpha * acc + jnp.dot(vst_ref[0, 0, :, pl.ds(off, SEL_KT)], p, preferred_element_type=f32)
    o_s = (acc_s / jnp.maximum(l_s, 1e-30)).T

    n_wk = WINDOW + QSPAN
    w0 = pl.multiple_of(jnp.maximum(q0 - WINDOW, 0), QSPAN)
    dist = q_pos(n_wk) - (w0 + lax.broadcasted_iota(jnp.int32, (n_wk, rows), 0))
    ok = (dist >= 0) & (dist < WINDOW)
    s = jnp.where(ok, _nt(kw_ref[0, 0, pl.ds(w0, n_wk), :], q), NEG_INF)
    e = jnp.where(ok, jnp.exp(s - jnp.max(s, axis=0, keepdims=True)), 0.0)
    l_w = jnp.sum(e, axis=0, keepdims=True)
    acc_w = jnp.dot(vwt_ref[0, 0, :, pl.ds(w0, n_wk)], e.astype(bf16), preferred_element_type=f32)
    o_w = (acc_w / jnp.maximum(l_w, 1e-30)).T

    gt = gt_ref[0, 0].reshape(rows, 3)
    o = gt[:, 0:1] * oc_ref[0, 0].reshape(rows, DH) + gt[:, 1:2] * o_s + gt[:, 2:3] * o_w
    o_ref[0, 0] = o.reshape(NQT, trows, DH)


def nsa_sel_win(qe, ke, vst, kw, vwt, oc, gt):
    bsz, _, n_q = qe.shape[:3]
    t_len = ke.shape[2]
    assert n_q % NQT == 0 and t_len % SEL_KT == 0 and WINDOW % QSPAN == 0 and t_len >= WINDOW + QSPAN
    trows = HPG * Q_BLOCK
    tile = lambda w: pl.BlockSpec((1, 1, NQT, trows, w), lambda b, g, i: (b, g, i, 0, 0))
    full = lambda w: pl.BlockSpec((1, 1, t_len, w), lambda b, g, i: (b, g, 0, 0))
    full_t = pl.BlockSpec((1, 1, DH, t_len), lambda b, g, i: (b, g, 0, 0))
    return pl.pallas_call(
        _sel_win_kernel,
        grid=(bsz, G, n_q // NQT),
        in_specs=[tile(KEXT), full(KEXT), full_t, full(DH), full_t, tile(DH), tile(3)],
        out_specs=tile(DH),
        out_shape=jax.ShapeDtypeStruct((bsz, G, n_q, trows, DH), f32),
        scratch_shapes=[pltpu.VMEM((SEL_KT, NQT * trows), f32), pltpu.VMEM((SEL_KT, NQT * trows), bf16)],
        compiler_params=pltpu.CompilerParams(dimension_semantics=("parallel", "parallel", "arbitrary"),
                                             vmem_limit_bytes=VMEM_LIMIT),
        name="nsa_sel_win",
    )(qe, ke, vst, kw, vwt, oc, gt)


QROWS = HPG * Q_BLOCK


def _prompt_attn_kernel(q_ref, gt_ref, ck_ref, cvt_ref, impt_ref, ke_ref, vst_ref, kw_ref, vwt_ref, o_ref,
                        s_scr, p_scr, *, n_cmp):
    qi = pl.program_id(2)
    rows = QROWS
    q0 = qi * Q_BLOCK
    q = q_ref[0, 0, 0]

    def q_pos(n_keys):
        return q0 + (lax.broadcasted_iota(jnp.int32, (n_keys, rows), 1) & (Q_BLOCK - 1))

    def softmax0(s, ok):
        e = jnp.where(ok, jnp.exp(s - jnp.max(s, axis=0, keepdims=True)), 0.0)
        return e, jnp.maximum(jnp.sum(e, axis=0, keepdims=True), 1e-30)

    n_wk = WINDOW + Q_BLOCK
    w0 = pl.multiple_of(jnp.maximum(q0 - WINDOW, 0), Q_BLOCK)
    dist = q_pos(n_wk) - (w0 + lax.broadcasted_iota(jnp.int32, (n_wk, rows), 0))
    ok = (dist >= 0) & (dist < WINDOW)
    e, l_w = softmax0(jnp.where(ok, _nt(kw_ref[0, 0, pl.ds(w0, n_wk), :], q), NEG_INF), ok)
    o_w = (jnp.dot(vwt_ref[0, 0, :, pl.ds(w0, n_wk)], e.astype(bf16), preferred_element_type=f32) / l_w).T

    nn = lax.broadcasted_iota(jnp.int32, (NC_PAD, rows), 0)
    ok = (nn * CMP_STRIDE + (CMP_LEN - 1) <= q_pos(NC_PAD)) & (nn < n_cmp)
    e, l_c = softmax0(jnp.where(ok, _nt(ck_ref[0, 0], q), NEG_INF), ok)
    p_c = e / l_c
    o_c = jnp.dot(cvt_ref[0, 0], p_c.astype(bf16), preferred_element_type=f32).T
    psum = p_c[:, 0:Q_BLOCK]
    for h in range(1, HPG):
        psum = psum + p_c[:, h * Q_BLOCK:(h + 1) * Q_BLOCK]
    hi, lo = _split_bf16(psum)
    imp_t = (jnp.dot(impt_ref[...], hi, preferred_element_type=f32)
             + jnp.dot(impt_ref[...], lo, preferred_element_type=f32))
    blk = lax.broadcasted_iota(jnp.int32, (NB_PROMPT, Q_BLOCK), 0)
    qp = q0 + lax.broadcasted_iota(jnp.int32, (NB_PROMPT, Q_BLOCK), 1)
    qb = qp >> 6
    forced = (blk == 0) | (blk == qb) | (blk == qb - 1)
    causal = blk * SEL_BLOCK <= qp
    score = jnp.where(causal, jnp.where(forced, FORCE_SCORE, imp_t), MASK_SCORE)
    sel = _topk_rounds(score, blk, 0) & causal
    bias = jnp.where(sel, 0.0, MASK_SCORE).T.astype(bf16)
    qe = jnp.concatenate([jnp.concatenate([bias] * HPG, axis=0), q], axis=1)

    def sel_scores(off):
        return _nt(ke_ref[0, 0, pl.ds(off, SEL_KT), :], qe)

    def pv(j, p):
        off = pl.multiple_of(jnp.maximum(j, 0) * SEL_KT, SEL_KT)
        return jnp.dot(vst_ref[0, 0, :, pl.ds(off, SEL_KT)], p, preferred_element_type=f32)

    def softmax_step(s, m, l):
        m_new = jnp.maximum(m, jnp.max(s, axis=0, keepdims=True))
        alpha = jnp.exp(m - m_new)
        p = jnp.exp(s - m_new)
        return m_new, alpha * l + jnp.sum(p, axis=0, keepdims=True), alpha, p.astype(bf16)

    n_full = q0 // SEL_KT
    s_scr[...] = sel_scores(0)
    p_scr[...] = jnp.zeros((SEL_KT, rows), bf16)

    def sel_body(j, carry):
        m, l, acc, alpha_prev = carry
        s_next = sel_scores(pl.multiple_of((j + 1) * SEL_KT, SEL_KT))
        acc = alpha_prev * acc + pv(j - 1, p_scr[...])
        m, l, alpha, p = softmax_step(s_scr[...], m, l)
        p_scr[...] = p
        s_scr[...] = s_next
        return m, l, acc, alpha

    init = (jnp.full((1, rows), NEG_INF, f32), jnp.zeros((1, rows), f32), jnp.zeros((DH, rows), f32),
            jnp.ones((1, rows), f32))
    m, l, acc, alpha_prev = lax.fori_loop(0, n_full, sel_body, init)
    acc = alpha_prev * acc + pv(n_full - 1, p_scr[...])
    off = pl.multiple_of(n_full * SEL_KT, SEL_KT)
    key_pos = off + lax.broadcasted_iota(jnp.int32, (SEL_KT, rows), 0)
    s = jnp.where(key_pos <= q_pos(SEL_KT), s_scr[...], 2.0 * MASK_SCORE)
    _, l_s, alpha, p = softmax_step(s, m, l)
    acc = alpha * acc + jnp.dot(vst_ref[0, 0, :, pl.ds(off, SEL_KT)], p, preferred_element_type=f32)
    o_s = (acc / jnp.maximum(l_s, 1e-30)).T

    gt = gt_ref[0, 0, 0]
    o_ref[0, 0, 0] = gt[:, 0:1] * o_c + gt[:, 1:2] * o_s + gt[:, 2:3] * o_w


def nsa_prompt_attention(qh, gt, ck, cvt, ke, vst, kw, vwt, n_cmp):
    bsz, _, n_q = qh.shape[:3]
    t_len = ke.shape[2]
    assert t_len % SEL_KT == 0 and t_len >= WINDOW + Q_BLOCK
    impt = _imp_map_t(NC_PAD, NB_PROMPT, n_cmp, NB_PROMPT)
    tile = lambda w: pl.BlockSpec((1, 1, 1, QROWS, w), lambda b, g, i: (b, g, i, 0, 0))
    per_bg = lambda r, w: pl.BlockSpec((1, 1, r, w), lambda b, g, i: (b, g, 0, 0))
    return pl.pallas_call(
        functools.partial(_prompt_attn_kernel, n_cmp=n_cmp),
        grid=(bsz, G, n_q),
        in_specs=[tile(DH), tile(3), per_bg(NC_PAD, DH), per_bg(DH, NC_PAD),
                  pl.BlockSpec((NB_PROMPT, NC_PAD), lambda b, g, i: (0, 0)),
                  per_bg(t_len, KEXT), per_bg(DH, t_len), per_bg(t_len, DH), per_bg(DH, t_len)],
        out_specs=tile(DH),
        out_shape=jax.ShapeDtypeStruct((bsz, G, n_q, QROWS, DH), f32),
        scratch_shapes=[pltpu.VMEM((SEL_KT, QROWS), f32), pltpu.VMEM((SEL_KT, QROWS), bf16)],
        compiler_params=pltpu.CompilerParams(dimension_semantics=("parallel", "parallel", "arbitrary"),
                                             vmem_limit_bytes=VMEM_LIMIT),
        name="nsa_prompt_attention",
    )(qh, gt, ck, cvt, impt, ke, vst, kw, vwt)


N_SUB = PAST_LEN // CMP_STRIDE
HALF = 2 * G * DH


def _gelu(x):
    return 0.5 * x * (1.0 + jnp.tanh(0.7978845608028654 * (x + 0.044715 * (x * x * x))))


def _sample_cmp_kernel(pt_ref, *refs):
    pages = refs[:N_PAGES]
    pe_ref, w1_ref, b1_ref, w2_ref, b2_ref, kg_ref, ck_ref, cv_ref, x_scr = refs[N_PAGES:]
    n_chunk = HALF // LANES
    for c in range(n_chunk):
        for j in range(N_PAGES):
            x_scr[c, pl.ds(j * PAGE_SIZE, PAGE_SIZE), :] = pages[j][0, :, pl.ds(c * LANES, LANES)]
        x_scr[c, pl.ds(PAST_LEN, CMP_STRIDE), :] = jnp.zeros((CMP_STRIDE, LANES), f32)
    for part in range(2):
        acc = jnp.zeros((G * N_SUB, CMP_HID), f32)
        for s in range(CMP_LEN):
            two = [x_scr[part * (G // 2) + c, pl.ds(s, N_SUB, stride=CMP_STRIDE), :] for c in range(G // 2)]
            x = jnp.concatenate([two[g // 2][:, (g % 2) * DH:(g % 2 + 1) * DH] for g in range(G)],
                                axis=0)
            x = x + pe_ref[part, s:s + 1, :]
            acc = acc + jnp.dot(x.astype(bf16), w1_ref[part, s], preferred_element_type=f32)
        hid = _gelu(acc + b1_ref[part:part + 1, :])
        out = jnp.dot(hid.astype(bf16), w2_ref[part], preferred_element_type=f32) + b2_ref[part:part + 1, :]
        if part == 0:
            out = out * lax.rsqrt(jnp.mean(out * out, axis=-1, keepdims=True) + NORM_EPS) * kg_ref[...]
        out_ref = ck_ref if part == 0 else cv_ref
        for c in range(G // 2):
            out_ref[0, c] = jnp.concatenate([out[(2 * c) * N_SUB:(2 * c + 1) * N_SUB],
                                             out[(2 * c + 1) * N_SUB:(2 * c + 2) * N_SUB]], axis=1).astype(bf16)


def sample_compress(cache3, page_table, pe, w1, b1, w2, b2, kg):
    bsz = page_table.shape[0]

    def page_spec(j):
        return pl.BlockSpec((1, PAGE_SIZE, HALF), lambda b, pt: (pt[b, j], 0, 0))

    const = lambda shape: pl.BlockSpec(shape, lambda b, pt: (0,) * len(shape))
    out_spec = pl.BlockSpec((1, G // 2, N_SUB, LANES), lambda b, pt: (b, 0, 0, 0))
    grid_spec = pltpu.PrefetchScalarGridSpec(
        num_scalar_prefetch=1, grid=(bsz,),
        in_specs=[page_spec(j) for j in range(N_PAGES)] + [
            const((2, CMP_LEN, DH)), const((2, CMP_LEN, DH, CMP_HID)), const((2, CMP_HID)),
            const((2, CMP_HID, DH)), const((2, DH)), const((1, DH))],
        out_specs=[out_spec, out_spec],
        scratch_shapes=[pltpu.VMEM((HALF // LANES, PAST_LEN + CMP_STRIDE, LANES), f32)])
    return pl.pallas_call(
        _sample_cmp_kernel, grid_spec=grid_spec,
        out_shape=[jax.ShapeDtypeStruct((bsz, G // 2, N_SUB, LANES), bf16)] * 2,
        compiler_params=pltpu.CompilerParams(dimension_semantics=("arbitrary",), vmem_limit_bytes=VMEM_LIMIT),
        name="sample_compress",
    )(page_table, *([cache3] * N_PAGES), pe, w1.reshape(2, CMP_LEN, DH, CMP_HID).astype(bf16), b1,
      w2.astype(bf16), b2, kg)


T_NEW = 4
NEW_PAD = 128
NB_SAMPLE = -(-(PAST_LEN + T_NEW) // SEL_BLOCK)
N_CMP_SAMPLE = (PAST_LEN + T_NEW) // CMP_STRIDE - 1
ROWS_S = HPG * T_NEW


def _sample_attn_kernel(pt_ref, *refs):
    pages = refs[:N_PAGES]
    (q_ref, gt_ref, ck_ref, cv_ref, new_ref, win_ref, wnew_ref, impt_ref, oh_ref, ohn_ref, o_ref) = refs[N_PAGES:]
    rows = 2 * ROWS_S
    t_of_row = lax.broadcasted_iota(jnp.int32, (rows, LANES), 0) & (T_NEW - 1)
    lane = lax.broadcasted_iota(jnp.int32, (rows, LANES), 1)
    q_pos = PAST_LEN + t_of_row
    hs_o = lax.broadcasted_iota(jnp.int32, (SUBLANES, rows), 0)
    hs_r = lax.broadcasted_iota(jnp.int32, (SUBLANES, rows), 1)
    hsum = ((hs_o >> 2 == hs_r >> 4) & ((hs_o & 3) == (hs_r & 3))).astype(bf16)
    sp_r = lax.broadcasted_iota(jnp.int32, (rows, SUBLANES), 0)
    sp_o = lax.broadcasted_iota(jnp.int32, (rows, SUBLANES), 1)
    spread = ((sp_o >> 2 == sp_r >> 4) & ((sp_o & 3) == (sp_r & 3))).astype(bf16)
    blk8 = lax.broadcasted_iota(jnp.int32, (SUBLANES, LANES), 1)
    qp8 = PAST_LEN + (lax.broadcasted_iota(jnp.int32, (SUBLANES, LANES), 0) & (T_NEW - 1))
    qb8 = qp8 >> 6
    forced = (blk8 == 0) | (blk8 == qb8) | (blk8 == qb8 - 1)
    causal = (blk8 * SEL_BLOCK <= qp8) & (blk8 < NB_SAMPLE)
    n_buf = win_ref.shape[1]

    for c in range(G // 2):
        lanes_k = pl.ds(c * LANES, LANES)
        lanes_v = pl.ds(G * DH + c * LANES, LANES)
        q = q_ref[0, c]
        gt = gt_ref[0, c]
        s_c = _nt(q, ck_ref[0, c])
        ok_c = (lane * CMP_STRIDE + (CMP_LEN - 1) <= q_pos) & (lane < N_CMP_SAMPLE)
        p_c = _masked_softmax(s_c, ok_c)
        o_c = jnp.dot(p_c.astype(bf16), cv_ref[0, c], preferred_element_type=f32)
        hi, lo = _split_bf16(p_c)
        psum = jnp.dot(hsum, hi, preferred_element_type=f32) + jnp.dot(hsum, lo, preferred_element_type=f32)
        hi, lo = _split_bf16(psum)
        imp = _nt(hi, impt_ref[...]) + _nt(lo, impt_ref[...])
        score = jnp.where(blk8 < NB_SAMPLE,
                          jnp.where(causal, jnp.where(forced, FORCE_SCORE, imp), MASK_SCORE), PICKED)
        sel = _topk_by_rank(score, blk8, NB_SAMPLE) & causal
        bias8 = jnp.where(sel, 0.0, MASK_SCORE).astype(bf16)
        bias = jnp.dot(spread, bias8, preferred_element_type=f32).astype(bf16)
        k_all = jnp.concatenate([pages[j][0, :, lanes_k] for j in range(N_PAGES)], axis=0).astype(bf16)
        s_past = _nt(q, k_all) + jnp.dot(bias, oh_ref[...], preferred_element_type=f32)
        s_new = _nt(q, new_ref[0, :, lanes_k].astype(bf16)) + jnp.dot(bias, ohn_ref[...],
                                                                      preferred_element_type=f32)
        s_new = jnp.where(lane <= t_of_row, s_new, 2.0 * MASK_SCORE)
        s_all = jnp.concatenate([s_past, s_new], axis=1)
        e = jnp.exp(s_all - jnp.max(s_all, axis=1, keepdims=True))
        den = jnp.maximum(jnp.sum(e, axis=1, keepdims=True), 1e-30)
        v_all = jnp.concatenate([pages[j][0, :, lanes_v] for j in range(N_PAGES)] + [new_ref[0, :, lanes_v]],
                                axis=0).astype(bf16)
        o_s = jnp.dot(e.astype(bf16), v_all, preferred_element_type=f32) / den
        kw_all = jnp.concatenate([win_ref[0, :, lanes_k], wnew_ref[0, :, lanes_k]], axis=0).astype(bf16)
        idx = lax.broadcasted_iota(jnp.int32, (rows, n_buf), 1)
        t_w = lax.broadcasted_iota(jnp.int32, (rows, n_buf), 0) & (T_NEW - 1)
        w_pos = PAST_LEN - n_buf + idx
        dist = PAST_LEN + t_w - w_pos
        ok_w = jnp.concatenate([(dist >= 0) & (dist < WINDOW) & (w_pos >= 0), lane <= t_of_row], axis=1)
        p_w = _masked_softmax(_nt(q, kw_all), ok_w).astype(bf16)
        vw_all = jnp.concatenate([win_ref[0, :, lanes_v], wnew_ref[0, :, lanes_v]], axis=0).astype(bf16)
        o_w = jnp.dot(p_w, vw_all, preferred_element_type=f32)
        o_ref[0, c] = gt[:, 0:1] * o_c + gt[:, 1:2] * o_s + gt[:, 2:3] * o_w


def sample_attention(cache3, page_table, q2, gt2, ck2, cv2, new_sel, win_cache, new_win):
    bsz = page_table.shape[0]
    n_buf = win_cache.shape[1]
    rows = 2 * ROWS_S
    impt = _imp_map_t(N_SUB, LANES, N_CMP_SAMPLE, NB_SAMPLE)
    key_blk = np.arange(PAST_LEN) // SEL_BLOCK
    onehot = jnp.asarray(key_blk[None, :] == np.arange(LANES)[:, None], bf16)
    onehot_new = jnp.asarray(np.broadcast_to(np.arange(LANES)[:, None] == PAST_LEN // SEL_BLOCK,
                                             (LANES, NEW_PAD)), bf16)

    def page_spec(j):
        return pl.BlockSpec((1, PAGE_SIZE, HALF), lambda b, pt: (pt[b, j], 0, 1))

    per_seq = lambda shape: pl.BlockSpec((1,) + shape, lambda b, pt: (b,) + (0,) * len(shape))
    const = lambda shape: pl.BlockSpec(shape, lambda b, pt: (0,) * len(shape))
    grid_spec = pltpu.PrefetchScalarGridSpec(
        num_scalar_prefetch=1, grid=(bsz,),
        in_specs=[page_spec(j) for j in range(N_PAGES)] + [
            per_seq((G // 2, rows, LANES)), per_seq((G // 2, rows, 3)),
            per_seq((G // 2, N_SUB, LANES)), per_seq((G // 2, N_SUB, LANES)),
            per_seq((NEW_PAD, HALF)), per_seq((n_buf, HALF)), per_seq((NEW_PAD, HALF)),
            const((LANES, N_SUB)), const((LANES, PAST_LEN)), const((LANES, NEW_PAD))],
        out_specs=per_seq((G // 2, rows, LANES)))
    return pl.pallas_call(
        _sample_attn_kernel, grid_spec=grid_spec,
        out_shape=jax.ShapeDtypeStruct((bsz, G // 2, rows, LANES), f32),
        compiler_params=pltpu.CompilerParams(dimension_semantics=("arbitrary",), vmem_limit_bytes=VMEM_LIMIT),
        name="sample_attention",
    )(page_table, *([cache3] * N_PAGES), q2, gt2, ck2, cv2, new_sel, win_cache, new_win, impt, onehot, onehot_new)


def rmsnorm(x, g):
    y = x * lax.rsqrt(jnp.mean(x * x, -1, keepdims=True) + NORM_EPS)
    return y * g


def _ffn_in_kernel(x_ref, g_ref, w_ref, o_ref):
    x = x_ref[...]
    xn = x * lax.rsqrt(jnp.mean(x * x, axis=-1, keepdims=True) + NORM_EPS) * g_ref[...]
    o_ref[...] = jnp.dot(xn.astype(bf16), w_ref[...], preferred_element_type=f32)


def _ffn_out_kernel(x_ref, h_ref, halo_ref, cs_ref, cw_ref, cb_ref, w_ref, o_ref, u_scr, *, tiles_per_seq):
    first = pl.program_id(0) % tiles_per_seq == 0
    tm = h_ref.shape[0]
    u_scr[pl.ds(0, SUBLANES), :] = halo_ref[...]
    u_scr[pl.ds(SUBLANES, tm), :] = h_ref[:, pl.ds(0, D_FF)]

    @pl.when(first)
    def _():
        u_scr[pl.ds(SUBLANES - (CONV_W - 1), CONV_W - 1), :] = cs_ref[0]

    uc = cb_ref[...] + cw_ref[CONV_W - 1:CONV_W, :] * u_scr[pl.ds(SUBLANES, tm), :]
    for j in range(CONV_W - 1):
        uc = uc + cw_ref[j:j + 1, :] * u_scr[pl.ds(SUBLANES - (CONV_W - 1) + j, tm), :]
    act = _gelu(uc) * h_ref[:, pl.ds(D_FF, D_FF)]
    o_ref[...] = x_ref[...] + jnp.dot(act.astype(bf16), w_ref[...], preferred_element_type=f32)


FFN_TM = 512
FFN_TN = 1408
FFN_OUT_TM = 256


def conv_ffn_fused(x, g, conv_state, w_in, conv_w, conv_b, w_out):
    bsz, t_len, _ = x.shape
    m = bsz * t_len
    tiles_per_seq = t_len // FFN_OUT_TM
    x2 = x.reshape(m, D_MODEL)
    h = pl.pallas_call(
        _ffn_in_kernel,
        grid=(m // FFN_TM, 2 * D_FF // FFN_TN),
        in_specs=[pl.BlockSpec((FFN_TM, D_MODEL), lambda i, j: (i, 0)),
                  pl.BlockSpec((1, D_MODEL), lambda i, j: (0, 0)),
                  pl.BlockSpec((D_MODEL, FFN_TN), lambda i, j: (0, j))],
        out_specs=pl.BlockSpec((FFN_TM, FFN_TN), lambda i, j: (i, j)),
        out_shape=jax.ShapeDtypeStruct((m, 2 * D_FF), f32),
        compiler_params=pltpu.CompilerParams(dimension_semantics=("parallel", "parallel"),
                                             vmem_limit_bytes=VMEM_LIMIT),
        name="ffn_in",
    )(x2, g.reshape(1, D_MODEL), w_in.astype(bf16))
    halo_blocks = FFN_OUT_TM // SUBLANES
    y = pl.pallas_call(
        functools.partial(_ffn_out_kernel, tiles_per_seq=tiles_per_seq),
        grid=(m // FFN_OUT_TM,),
        in_specs=[pl.BlockSpec((FFN_OUT_TM, D_MODEL), lambda i: (i, 0)),
                  pl.BlockSpec((FFN_OUT_TM, 2 * D_FF), lambda i: (i, 0)),
                  pl.BlockSpec((SUBLANES, D_FF), lambda i: (jnp.maximum(i * halo_blocks - 1, 0), 0)),
                  pl.BlockSpec((1, CONV_W - 1, D_FF), lambda i: (i // tiles_per_seq, 0, 0)),
                  pl.BlockSpec((CONV_W, D_FF), lambda i: (0, 0)),
                  pl.BlockSpec((1, D_FF), lambda i: (0, 0)),
                  pl.BlockSpec((D_FF, D_MODEL), lambda i: (0, 0))],
        out_specs=pl.BlockSpec((FFN_OUT_TM, D_MODEL), lambda i: (i, 0)),
        out_shape=jax.ShapeDtypeStruct((m, D_MODEL), f32),
        scratch_shapes=[pltpu.VMEM((SUBLANES + FFN_OUT_TM, D_FF), f32)],
        compiler_params=pltpu.CompilerParams(dimension_semantics=("arbitrary",), vmem_limit_bytes=56 * 1024 * 1024),
        name="ffn_out",
    )(x2, h, h, conv_state, conv_w, conv_b.reshape(1, D_FF), w_out.astype(bf16))
    conv_new = h.reshape(bsz, t_len, 2 * D_FF)[:, t_len - (CONV_W - 1):, :D_FF]
    return y.reshape(bsz, t_len, D_MODEL), conv_new


def conv_ffn(xn, conv_state, w_in, conv_w, conv_b, w_out):
    t_len = xn.shape[1]
    u, gate = jnp.split(mm3(xn, w_in), 2, axis=-1)
    u_pad = jnp.concatenate([conv_state, u], axis=1)
    uc = conv_b + conv_w[0] * u_pad[:, 0:t_len]
    for j in range(1, CONV_W):
        uc = uc + conv_w[j] * u_pad[:, j:j + t_len]
    return mm3(jax.nn.gelu(uc) * gate, w_out), u_pad[:, t_len:]


RW_TM = 256
SEG_W = LANES


def _head_seg():
    seg = np.arange(D_MODEL)[:, None] // RW_HEAD_DIM == np.arange(SEG_W)[None, :]
    return jnp.asarray(seg, bf16), jnp.asarray(seg.T, bf16)


def _head_sum(x, seg_ref):
    hi, lo = _split_bf16(x)
    return jnp.dot(hi, seg_ref[...], preferred_element_type=f32) + jnp.dot(lo, seg_ref[...],
                                                                           preferred_element_type=f32)


def _head_expand(s, segt_ref):
    hi, lo = _split_bf16(s)
    return jnp.dot(hi, segt_ref[...], preferred_element_type=f32) + jnp.dot(lo, segt_ref[...],
                                                                            preferred_element_type=f32)


def _rwkv_pre_kernel(*refs, has_vfirst):
    it = iter(refs)
    xn_ref, xs_ref = next(it), next(it)
    vf_ref = next(it) if has_vfirst else None
    mu_ref, wr, wk, wv, w1, w2, a1, a2 = (next(it) for _ in range(8))
    v1, v2 = (next(it), next(it)) if has_vfirst else (None, None)
    g1, g2, w0, a0 = (next(it) for _ in range(4))
    v0 = next(it) if has_vfirst else None
    kk_ref, ka_ref, seg_ref, segt_ref = (next(it) for _ in range(4))
    r_o, w_o, k_o, v_o, nkk_o, kka_o, g_o = (next(it) for _ in range(7))

    dot = lambda x, w: jnp.dot(x.astype(bf16), w[...], preferred_element_type=f32)
    xn = xn_ref[...]
    xx = xs_ref[...] - xn
    xr, xw, xk, xv, xa, xg = (xn + xx * mu_ref[i:i + 1, :] for i in range(6))
    r = dot(xr, wr)
    k = dot(xk, wk)
    v = dot(xv, wv)
    z = -(w0[...] + dot(jnp.tanh(dot(xw, w1)), w2))
    softplus = jnp.maximum(z, 0.0) + jnp.log1p(jnp.exp(-jnp.abs(z)))
    decay = jnp.exp(-jnp.exp(-softplus - 0.5))
    a = jax.nn.sigmoid(a0[...] + dot(dot(xa, a1), a2))
    if has_vfirst:
        v = v + (vf_ref[...] - v) * jax.nn.sigmoid(v0[...] + dot(dot(xv, v1), v2))
    g = dot(jax.nn.sigmoid(dot(xg, g1)), g2)
    kk = k * kk_ref[...]
    norm = jnp.maximum(jnp.sqrt(_head_sum(kk * kk, seg_ref)), 1e-12)
    kk = kk * _head_expand(1.0 / norm, segt_ref)
    r_o[...] = r
    w_o[...] = decay
    k_o[...] = k * (1.0 + (a - 1.0) * ka_ref[...])
    v_o[...] = v
    nkk_o[...] = -kk
    kka_o[...] = kk * a
    g_o[...] = g


def rwkv_pre(xn, xs, v_first, li, p):
    m = xn.shape[0]
    tm = RW_TM if m % RW_TM == 0 else m
    has_vfirst = v_first is not None
    seg, segt = _head_seg()
    row = lambda a: a.reshape(1, -1)
    cast = lambda w: w.astype(bf16)
    ops = [xn, xs] + ([v_first] if has_vfirst else [])
    ops += [p['rw_mu'][li], cast(p['rw_w_r'][li]), cast(p['rw_w_k'][li]), cast(p['rw_w_v'][li]),
            cast(p['rw_w1'][li]), cast(p['rw_w2'][li]), cast(p['rw_a1'][li]), cast(p['rw_a2'][li])]
    if has_vfirst:
        ops += [cast(p['rw_v1'][li - 1]), cast(p['rw_v2'][li - 1])]
    ops += [cast(p['rw_g1'][li]), cast(p['rw_g2'][li]), row(p['rw_w0'][li]), row(p['rw_a0'][li])]
    if has_vfirst:
        ops += [row(p['rw_v0'][li - 1])]
    ops += [row(p['rw_k_k'][li]), row(p['rw_k_a'][li]), seg, segt]
    n_rows = 3 if has_vfirst else 2
    tile = pl.BlockSpec((tm, D_MODEL), lambda i: (i, 0))
    whole = lambda a: pl.BlockSpec(a.shape, lambda i: (0,) * a.ndim)
    return pl.pallas_call(
        functools.partial(_rwkv_pre_kernel, has_vfirst=has_vfirst),
        grid=(m // tm,),
        in_specs=[tile] * n_rows + [whole(a) for a in ops[n_rows:]],
        out_specs=[tile] * 7,
        out_shape=[jax.ShapeDtypeStruct((m, D_MODEL), f32)] * 7,
        compiler_params=pltpu.CompilerParams(dimension_semantics=("parallel",), vmem_limit_bytes=VMEM_LIMIT),
        name="rwkv_pre",
    )(*ops)


def _rwkv_post_kernel(y_ref, r_ref, k_ref, v_ref, g_ref, x_ref, lnw, lnb, rk, wo, seg_ref, segt_ref, o_ref):
    y = y_ref[...]
    inv_n = 1.0 / RW_HEAD_DIM
    yc = y - _head_expand(_head_sum(y, seg_ref) * inv_n, segt_ref)
    var = _head_expand(_head_sum(yc * yc, seg_ref) * inv_n, segt_ref)
    yn = yc * lax.rsqrt(var + RW_GN_EPS) * lnw[...] + lnb[...]
    bonus = _head_expand(_head_sum(r_ref[...] * k_ref[...] * rk[...], seg_ref), segt_ref) * v_ref[...]
    out = ((yn + bonus) * g_ref[...]).astype(bf16)
    o_ref[...] = x_ref[...] + jnp.dot(out, wo[...], preferred_element_type=f32)


def rwkv_post(y, r, k, v, g, x, li, p):
    m = y.shape[0]
    tm = RW_TM if m % RW_TM == 0 else m
    seg, segt = _head_seg()
    row = lambda a: a.reshape(1, -1)
    consts = [row(p['rw_ln_w'][li]), row(p['rw_ln_b'][li]), row(p['rw_r_k'][li]), p['rw_w_o'][li].astype(bf16),
              seg, segt]
    tile = pl.BlockSpec((tm, D_MODEL), lambda i: (i, 0))
    whole = lambda a: pl.BlockSpec(a.shape, lambda i: (0,) * a.ndim)
    return pl.pallas_call(
        _rwkv_post_kernel,
        grid=(m // tm,),
        in_specs=[tile] * 6 + [whole(a) for a in consts],
        out_specs=tile,
        out_shape=jax.ShapeDtypeStruct((m, D_MODEL), f32),
        compiler_params=pltpu.CompilerParams(dimension_semantics=("parallel",), vmem_limit_bytes=VMEM_LIMIT),
        name="rwkv_post",
    )(y, r, k, v, g, x, *consts)


def rwkv7_layer(x, shift_prev, s0, v_first, li, p, scan_cfg):
    bsz, t_len, _ = x.shape
    m = bsz * t_len
    xn = rmsnorm(x, p['norm_g'][li, 0])
    xs = jnp.concatenate([shift_prev[:, None], xn[:, :-1]], axis=1)
    flat = lambda a: a.reshape(m, D_MODEL)
    r, decay, k, v, nkk, kka, g = rwkv_pre(flat(xn), flat(xs), None if li == 0 else flat(v_first), li, p)
    if li == 0:
        v_first = v
    seq = lambda a: a.reshape(bsz, t_len, D_MODEL)
    y, s_new = wkv_scan(seq(r), seq(decay), seq(k), seq(v), seq(nkk), seq(kka), to_pairs(s0), **scan_cfg)
    x_new = rwkv_post(flat(y), r, k, v, g, flat(x), li, p)
    return seq(x_new), from_pairs(s_new, bsz), xn[:, -1], seq(v_first)


def kv_rows(h, p):
    bsz, t_len, _ = h.shape
    kv = mm3(rmsnorm(h, p['kv_norm_g']), p['w_kv']).reshape(bsz, t_len, 6, G, DH)
    k_sel = rmsnorm(kv[:, :, 2], p['k_norm_g'][1])
    k_win = rmsnorm(kv[:, :, 4], p['k_norm_g'][2])
    paged = jnp.stack([kv[:, :, 0], kv[:, :, 1], k_sel, kv[:, :, 3]], axis=2)
    win = jnp.stack([k_win, kv[:, :, 5]], axis=2)
    return paged, win


def nsa_query(xn, lb, p):
    bsz, t_len, _ = xn.shape
    n_q = NSA_HEADS * DH
    w = jnp.pad(p['nsa_w_q'][lb], ((0, 0), (0, LANES - 3 * NSA_HEADS)))
    proj = mm3(xn, w)
    q = proj[..., :n_q].reshape(bsz, t_len, G, HPG, DH)
    q = rmsnorm(q, p['q_norm_g'][lb]) * (DH ** -0.5)
    gates = jax.nn.sigmoid(proj[..., n_q:n_q + 3 * NSA_HEADS]).reshape(bsz, t_len, G, HPG, 3)
    return q, gates


def compress_rows_prompt(rows, pe, w1, b1, w2, b2):
    bsz, t_len = rows.shape[:2]
    n_sub = t_len // CMP_STRIDE
    sub = rows.reshape(bsz, n_sub, CMP_STRIDE, G, DH).transpose(0, 1, 3, 2, 4)
    half = CMP_STRIDE * DH
    x0 = (sub + pe[:CMP_STRIDE][None, None, None]).reshape(bsz * n_sub * G, half)
    x1 = (sub + pe[CMP_STRIDE:][None, None, None]).reshape(bsz * n_sub * G, half)
    p0 = mm(x0, w1[:half]).reshape(bsz, n_sub, G, CMP_HID)
    p1 = mm(x1, w1[half:]).reshape(bsz, n_sub, G, CMP_HID)
    hid = jax.nn.gelu(p0[:, :-1] + p1[:, 1:] + b1)
    out = mm(hid.reshape(bsz * (n_sub - 1) * G, CMP_HID), w2) + b2
    return out.reshape(bsz, n_sub - 1, G, DH)


def to_q_tiles(x, n_q):
    bsz, _, _, _, w = x.shape
    x = x.reshape(bsz, n_q, Q_BLOCK, G, HPG, w).transpose(0, 3, 1, 4, 2, 5)
    return x.reshape(bsz, G, n_q, HPG * Q_BLOCK, w)


def from_q_tiles(x, t_len):
    bsz, _, n_q, _, w = x.shape
    x = x.reshape(bsz, G, n_q, HPG, Q_BLOCK, w).transpose(0, 2, 4, 1, 3, 5)
    return x.reshape(bsz, t_len, G * HPG * w)


def prompt_context(h, p):
    paged, win = kv_rows(h, p)
    bsz, t_len = h.shape[:2]
    ck = rmsnorm(compress_rows_prompt(paged[:, :, 0], p['cmp_pe'][0], p['cmp_w1'][0], p['cmp_b1'][0],
                                      p['cmp_w2'][0], p['cmp_b2'][0]), p['k_norm_g'][0])
    cv = compress_rows_prompt(paged[:, :, 1], p['cmp_pe'][1], p['cmp_w1'][1], p['cmp_b1'][1],
                              p['cmp_w2'][1], p['cmp_b2'][1])
    n_cmp = ck.shape[1]
    pad = lambda c: jnp.pad(c, ((0, 0), (0, NC_PAD - n_cmp), (0, 0), (0, 0)))
    heads = lambda x: x.transpose(0, 2, 1, 3).astype(bf16)
    heads_t = lambda x: x.transpose(0, 2, 3, 1).astype(bf16)
    onehot = jnp.asarray(np.arange(t_len)[:, None] // SEL_BLOCK == np.arange(NB_PROMPT)[None, :], bf16)
    ke = jnp.concatenate([jnp.broadcast_to(onehot, (bsz, G, t_len, NB_PROMPT)), heads(paged[:, :, 2])], axis=-1)
    ctx = dict(ck=heads(pad(ck)), cvt=heads_t(pad(cv)), n_cmp=n_cmp, ke=ke, vst=heads_t(paged[:, :, 3]),
               kw=heads(win[:, :, 0]), vwt=heads_t(win[:, :, 1]))
    keep = min(WINDOW, t_len)
    return ctx, (paged, win[:, t_len - keep:])


def attend_prompt(q, gates, ctx):
    bsz, t_len = q.shape[:2]
    n_q = t_len // Q_BLOCK
    o = nsa_prompt_attention(to_q_tiles(q, n_q).astype(bf16), to_q_tiles(gates, n_q), ctx['ck'], ctx['cvt'],
                             ctx['ke'], ctx['vst'], ctx['kw'], ctx['vwt'], ctx['n_cmp'])
    return from_q_tiles(o, t_len)


def sample_context(h, cache3, page_table, cache_kv_win, p):
    paged, win = kv_rows(h, p)
    bsz, t_new = h.shape[:2]
    ck, cv = sample_compress(cache3, page_table, p['cmp_pe'], p['cmp_w1'], p['cmp_b1'], p['cmp_w2'], p['cmp_b2'],
                             p['k_norm_g'][0:1])
    n_buf = cache_kv_win.shape[1]
    pad_rows = lambda x: jnp.pad(x.reshape(bsz, t_new, HALF), ((0, 0), (0, NEW_PAD - t_new), (0, 0)))
    ctx = dict(ck=ck, cv=cv, new_sel=pad_rows(paged[:, :, 2:4]), new_win=pad_rows(win),
               win_cache=cache_kv_win.reshape(bsz, n_buf, HALF))
    win_full = jnp.concatenate([cache_kv_win, win], axis=1)
    return ctx, (paged, win_full[:, win_full.shape[1] - n_buf:])


def attend_sample(q, gates, ctx, cache3, page_table):
    bsz, t_new = q.shape[:2]
    rows = lambda x: x.transpose(0, 2, 3, 1, 4).reshape(bsz, G // 2, 2 * HPG * t_new, x.shape[-1])
    qr = rows(q)
    own = (jnp.arange(2 * HPG * t_new) // (HPG * t_new))[:, None] == (jnp.arange(LANES) // DH)[None, :]
    q2 = jnp.where(own, jnp.concatenate([qr, qr], axis=-1), 0.0).astype(bf16)
    o = sample_attention(cache3, page_table, q2, rows(gates), ctx['ck'], ctx['cv'],
                         ctx['new_sel'], ctx['win_cache'], ctx['new_win'])
    o = jnp.where(own, o, 0.0)
    o = o[..., :DH] + o[..., DH:]
    return o.reshape(bsz, G, HPG, t_new, DH).transpose(0, 3, 1, 2, 4).reshape(bsz, t_new, NSA_HEADS * DH)


def forward(x, wkv_in, shift_in, conv_in, make_ctx, attend, p, scan_cfg):
    wkv_out, shift_out, conv_out = [], [], []
    v_first = None
    ctx = None
    kv_state = None
    for layer in range(DEPTH):
        if layer == N_A_LAYERS:
            ctx, kv_state = make_ctx(x)
        if layer < N_A_LAYERS:
            x, s_new, sh_new, v_first = rwkv7_layer(x, shift_in[layer], wkv_in[layer], v_first, layer, p, scan_cfg)
            wkv_out.append(s_new)
            shift_out.append(sh_new)
        else:
            lb = layer - N_A_LAYERS
            q, gates = nsa_query(rmsnorm(x, p['norm_g'][layer, 0]), lb, p)
            x = x + mm3(attend(q, gates, ctx), p['nsa_w_o'][lb])
        ffn_w = (p['ffn_w_in'][layer], p['ffn_conv_w'][layer], p['ffn_conv_b'][layer], p['ffn_w_out'][layer])
        if x.shape[1] % FFN_TM == 0:
            x, conv_new = conv_ffn_fused(x, p['norm_g'][layer, 1], conv_in[layer], *ffn_w)
        else:
            f, conv_new = conv_ffn(rmsnorm(x, p['norm_g'][layer, 1]), conv_in[layer], *ffn_w)
            x = x + f
        conv_out.append(conv_new)
    return x, kv_state, jnp.stack(wkv_out), jnp.stack(shift_out), jnp.stack(conv_out)


def kernel(x_prompt, x_sample, cache_kv, cache_kv_win, state_wkv, state_shift, state_ffn_conv, page_table, norm_g, rw_mu, rw_w_r, rw_w_k, rw_w_v, rw_w_o, rw_w0, rw_w1, rw_w2, rw_a0, rw_a1, rw_a2, rw_v0, rw_v1, rw_v2, rw_g1, rw_g2, rw_k_k, rw_k_a, rw_r_k, rw_ln_w, rw_ln_b, kv_norm_g, w_kv, k_norm_g, cmp_pe, cmp_w1, cmp_b1, cmp_w2, cmp_b2, nsa_w_q, q_norm_g, nsa_w_o, ffn_w_in, ffn_conv_w, ffn_conv_b, ffn_w_out):
    p = dict(norm_g=norm_g, rw_mu=rw_mu, rw_w_r=rw_w_r, rw_w_k=rw_w_k, rw_w_v=rw_w_v, rw_w_o=rw_w_o,
             rw_w0=rw_w0, rw_w1=rw_w1, rw_w2=rw_w2, rw_a0=rw_a0, rw_a1=rw_a1, rw_a2=rw_a2,
             rw_v0=rw_v0, rw_v1=rw_v1, rw_v2=rw_v2, rw_g1=rw_g1, rw_g2=rw_g2, rw_k_k=rw_k_k, rw_k_a=rw_k_a,
             rw_r_k=rw_r_k, rw_ln_w=rw_ln_w, rw_ln_b=rw_ln_b, kv_norm_g=kv_norm_g, w_kv=w_kv,
             k_norm_g=k_norm_g, cmp_pe=cmp_pe, cmp_w1=cmp_w1, cmp_b1=cmp_b1, cmp_w2=cmp_w2, cmp_b2=cmp_b2,
             nsa_w_q=nsa_w_q, q_norm_g=q_norm_g, nsa_w_o=nsa_w_o, ffn_w_in=ffn_w_in,
             ffn_conv_w=ffn_conv_w, ffn_conv_b=ffn_conv_b, ffn_w_out=ffn_w_out)
    bp, t_p = x_prompt.shape[:2]
    bs = x_sample.shape[0]

    wkv0 = jnp.zeros((N_A_LAYERS, bp, RW_HEADS, RW_HEAD_DIM, RW_HEAD_DIM), f32)
    shift0 = jnp.zeros((N_A_LAYERS, bp, D_MODEL), f32)
    conv0 = jnp.zeros((DEPTH, bp, CONV_W - 1, D_FF), f32)
    y_prompt, (kv_prompt, win_prompt), wkv_prompt, shift_prompt, conv_prompt = forward(
        x_prompt, wkv0, shift0, conv0, lambda h: prompt_context(h, p), attend_prompt, p,
        dict(bb=bp, tc=128))

    cache3 = cache_kv.reshape(cache_kv.shape[0], PAGE_SIZE, 4 * G * DH)
    y_sample, (kv_sample, win_sample), wkv_sample, shift_sample, conv_sample = forward(
        x_sample, state_wkv, state_shift, state_ffn_conv,
        lambda h: sample_context(h, cache3, page_table, cache_kv_win, p),
        lambda q, gates, ctx: attend_sample(q, gates, ctx, cache3, page_table), p,
        dict(bb=2, tc=x_sample.shape[1]))

    return (y_prompt, y_sample, kv_prompt, kv_sample, win_prompt, win_sample,
            wkv_prompt, wkv_sample, shift_prompt, shift_sample, conv_prompt, conv_sample)
```

```python
import functools

import numpy as np
import jax
import jax.numpy as jnp
from jax import lax
from jax.experimental import pallas as pl
from jax.experimental.pallas import tpu as pltpu

f32 = jnp.float32
bf16 = jnp.bfloat16

D_MODEL = 1024
DEPTH = 4
N_A_LAYERS = 2
PAST_LEN = 2048
PAGE_SIZE = 128
N_PAGES = PAST_LEN // PAGE_SIZE
RW_HEAD_DIM = 64
RW_HEADS = 16
RW_GN_EPS = 64e-5
DH = 64
NSA_HEADS = 16
G = 4
HPG = 4
CMP_LEN = 32
CMP_STRIDE = 16
CMP_HID = 128
SEL_BLOCK = 64
N_SELECT = 16
WINDOW = 512
Q_BLOCK = 128
D_FF = 2816
CONV_W = 3
NORM_EPS = 1e-6
NEG_INF = -1e30
FORCE_SCORE = 1e9
MASK_SCORE = -1e9
PICKED = -3e38

LANES = 128
SUBLANES = 8
VMEM_LIMIT = 48 * 1024 * 1024


def _nt(a, b):
    return lax.dot_general(a, b, (((1,), (1,)), ((), ())), preferred_element_type=f32)


def _split_bf16(x):
    hi = x.astype(bf16)
    lo = (x - hi.astype(f32)).astype(bf16)
    return hi, lo


def _mm_kernel(x_ref, w_ref, o_ref):
    o_ref[...] = jnp.dot(x_ref[...].astype(bf16), w_ref[...], preferred_element_type=f32)


MM_TM = 512


def _pick_tile(n, cands):
    for c in cands:
        if n % c == 0:
            return c
    return n


def mm(x, w):
    m, k = x.shape
    n = w.shape[1]
    w = w.astype(bf16)
    tm = MM_TM if m >= MM_TM else -(-m // SUBLANES) * SUBLANES
    mp = -(-m // tm) * tm
    if mp != m:
        x = jnp.pad(x, ((0, mp - m), (0, 0)))
    tn = n if n <= 1536 else _pick_tile(n, (1408, 1024, 512, 256, 128))
    out = pl.pallas_call(
        _mm_kernel,
        grid=(mp // tm, n // tn),
        in_specs=[pl.BlockSpec((tm, k), lambda i, j: (i, 0)),
                  pl.BlockSpec((k, tn), lambda i, j: (0, j))],
        out_specs=pl.BlockSpec((tm, tn), lambda i, j: (i, j)),
        out_shape=jax.ShapeDtypeStruct((mp, n), f32),
        compiler_params=pltpu.CompilerParams(dimension_semantics=("parallel", "parallel"),
                                             vmem_limit_bytes=VMEM_LIMIT),
        name="mm",
    )(x, w)
    return out[:m] if mp != m else out


def mm3(x, w):
    b, t, k = x.shape
    return mm(x.reshape(b * t, k), w).reshape(b, t, w.shape[1])


PAIR_W = 2 * RW_HEAD_DIM
N_PAIRS = RW_HEADS // 2


def _scan_kernel(r_ref, w_ref, k_ref, v_ref, a_ref, b_ref, s0_ref, y_ref, s_ref, *, bb, tc):
    n = RW_HEAD_DIM
    npairs = bb * N_PAIRS

    @pl.when(pl.program_id(1) == 0)
    def _():
        s_ref[...] = s0_ref[...]

    row = lax.broadcasted_iota(jnp.int32, (n, PAIR_W), 0)
    lane = lax.broadcasted_iota(jnp.int32, (n, PAIR_W), 1)
    diag = (lane % n) == row
    kk = lax.broadcasted_iota(jnp.int32, (PAIR_W, PAIR_W), 0) // n
    ll = lax.broadcasted_iota(jnp.int32, (PAIR_W, PAIR_W), 1) // n
    bones = (kk == ll).astype(bf16)
    group = min(tc, SUBLANES)

    def seg_sum_bcast(x):
        return jnp.dot(x.astype(bf16), bones, preferred_element_type=f32)

    def token(t0, s):
        def rowvec(ref, bi, hp):
            tile = ref[bi, pl.ds(t0, group), pl.ds(hp * PAIR_W, PAIR_W)]
            return jnp.broadcast_to(tile[s:s + 1], (n, PAIR_W))

        pairs = [(bi, hp) for bi in range(bb) for hp in range(N_PAIRS)]
        st = [s_ref[p] for p in range(npairs)]
        sa_in = jnp.concatenate([st[p] * rowvec(a_ref, *pairs[p]) for p in range(npairs)], axis=0)
        v_in = jnp.concatenate([jnp.where(diag, rowvec(v_ref, *pairs[p]), 0.0) for p in range(npairs)], axis=0)
        sa_col = seg_sum_bcast(sa_in)
        v_col = seg_sum_bcast(v_in)
        new = []
        for p in range(npairs):
            sl = slice(p * n, (p + 1) * n)
            sn = (st[p] * rowvec(w_ref, *pairs[p]) + sa_col[sl] * rowvec(b_ref, *pairs[p])
                  + v_col[sl] * rowvec(k_ref, *pairs[p]))
            s_ref[p] = sn
            new.append(sn)
        y_in = jnp.concatenate([new[p] * rowvec(r_ref, *pairs[p]) for p in range(npairs)], axis=0)
        y_col = seg_sum_bcast(y_in)
        rows = []
        for p in range(npairs):
            sl = slice(p * n, (p + 1) * n)
            rows.append(jnp.sum(jnp.where(diag, y_col[sl], 0.0), axis=0, keepdims=True))
        return rows

    def block(g, carry):
        t0 = pl.multiple_of(g * group, group)
        rows = [token(t0, s) for s in range(group)]
        for p in range(npairs):
            bi, hp = divmod(p, N_PAIRS)
            y_ref[bi, pl.ds(t0, group), pl.ds(hp * PAIR_W, PAIR_W)] = jnp.concatenate(
                [rows[s][p] for s in range(group)], axis=0)
        return carry

    lax.fori_loop(0, tc // group, block, 0)


def wkv_scan(r, w, k, v, a, b, s0_pairs, *, bb, tc):
    bsz, t_len, _ = r.shape
    assert bsz % bb == 0 and t_len % tc == 0
    vec_spec = pl.BlockSpec((bb, tc, D_MODEL), lambda i, j: (i, j, 0))
    st_spec = pl.BlockSpec((bb * N_PAIRS, RW_HEAD_DIM, PAIR_W), lambda i, j: (i, 0, 0))
    return pl.pallas_call(
        functools.partial(_scan_kernel, bb=bb, tc=tc),
        grid=(bsz // bb, t_len // tc),
        in_specs=[vec_spec] * 6 + [st_spec],
        out_specs=[vec_spec, st_spec],
        out_shape=[jax.ShapeDtypeStruct((bsz, t_len, D_MODEL), f32),
                   jax.ShapeDtypeStruct((bsz * N_PAIRS, RW_HEAD_DIM, PAIR_W), f32)],
        compiler_params=pltpu.CompilerParams(dimension_semantics=("arbitrary", "arbitrary"),
                                             vmem_limit_bytes=VMEM_LIMIT),
        name="wkv_scan",
    )(r, w, k, v, a, b, s0_pairs)


def to_pairs(s):
    bsz = s.shape[0]
    return s.reshape(bsz, N_PAIRS, 2, RW_HEAD_DIM, RW_HEAD_DIM).transpose(0, 1, 3, 2, 4).reshape(
        bsz * N_PAIRS, RW_HEAD_DIM, PAIR_W)


def from_pairs(s, bsz):
    return s.reshape(bsz, N_PAIRS, RW_HEAD_DIM, 2, RW_HEAD_DIM).transpose(0, 1, 3, 2, 4).reshape(
        bsz, RW_HEADS, RW_HEAD_DIM, RW_HEAD_DIM)


def _masked_softmax(s, mask):
    s = jnp.where(mask, s, NEG_INF)
    e = jnp.where(mask, jnp.exp(s - jnp.max(s, axis=-1, keepdims=True)), 0.0)
    return e / jnp.maximum(jnp.sum(e, axis=-1, keepdims=True), 1e-30)


def _topk_rounds(score, idx, axis):
    sel = jnp.zeros(score.shape, f32)
    idx = idx.astype(f32)
    for _ in range(N_SELECT):
        m = jnp.max(score, axis=axis, keepdims=True)
        first = jnp.min(jnp.where(score == m, idx, 1e6), axis=axis, keepdims=True)
        pick = idx == first
        sel = jnp.where(pick, 1.0, sel)
        score = jnp.where(pick, PICKED, score)
    return sel > 0.5


def _topk_by_rank(score, idx, n_valid):
    rank = jnp.zeros(score.shape, f32)
    for i in range(n_valid):
        ci = score[:, i:i + 1]
        beats = (ci > score) | ((ci == score) & (idx > i))
        rank = rank + jnp.where(beats, 1.0, 0.0)
    return rank < (N_SELECT - 0.5)


def _imp_map_t(n_cmp_pad, n_blk_pad, n_cmp, n_blk):
    ci = np.arange(n_cmp_pad)[None, :] * CMP_STRIDE
    sj = np.arange(n_blk_pad)[:, None] * SEL_BLOCK
    ov = np.clip(np.minimum(ci + CMP_LEN, sj + SEL_BLOCK) - np.maximum(ci, sj), 0, None) / CMP_STRIDE
    ov = ov * (np.arange(n_cmp_pad)[None, :] < n_cmp) * (np.arange(n_blk_pad)[:, None] < n_blk)
    return jnp.asarray(ov, bf16)


NC_PAD = 512
NB_PROMPT = 128
KEXT = NB_PROMPT + DH
SEL_KT = 4 * Q_BLOCK
QROWS = HPG * Q_BLOCK


def _prompt_attn_kernel(q_ref, gt_ref, ck_ref, cvt_ref, impt_ref, ke_ref, vst_ref, kw_ref, vwt_ref, o_ref,
                        s_scr, p_scr, *, n_cmp):
    qi = pl.program_id(2)
    rows = QROWS
    q0 = qi * Q_BLOCK
    q = jnp.concatenate([q_ref[0, :, h * DH:(h + 1) * DH] for h in range(HPG)], axis=0)

    def q_pos(n_keys):
        return q0 + (lax.broadcasted_iota(jnp.int32, (n_keys, rows), 1) & (Q_BLOCK - 1))

    def softmax0(s, ok):
        e = jnp.where(ok, jnp.exp(s - jnp.max(s, axis=0, keepdims=True)), 0.0)
        return e, jnp.maximum(jnp.sum(e, axis=0, keepdims=True), 1e-30)

    nn = lax.broadcasted_iota(jnp.int32, (NC_PAD, rows), 0)
    ok = (nn * CMP_STRIDE + (CMP_LEN - 1) <= q_pos(NC_PAD)) & (nn < n_cmp)
    e, l_c = softmax0(jnp.where(ok, _nt(ck_ref[0, 0], q), NEG_INF), ok)
    p_c = e / l_c
    o_c = jnp.dot(cvt_ref[0, 0], p_c.astype(bf16), preferred_element_type=f32).T
    psum = p_c[:, 0:Q_BLOCK]
    for h in range(1, HPG):
        psum = psum + p_c[:, h * Q_BLOCK:(h + 1) * Q_BLOCK]
    hi, lo = _split_bf16(psum)
    imp_t = (jnp.dot(impt_ref[...], hi, preferred_element_type=f32)
             + jnp.dot(impt_ref[...], lo, preferred_element_type=f32))
    blk = lax.broadcasted_iota(jnp.int32, (NB_PROMPT, Q_BLOCK), 0)
    qp = q0 + lax.broadcasted_iota(jnp.int32, (NB_PROMPT, Q_BLOCK), 1)
    qb = qp >> 6
    forced = (blk == 0) | (blk == qb) | (blk == qb - 1)
    causal = blk * SEL_BLOCK <= qp
    score = jnp.where(causal, jnp.where(forced, FORCE_SCORE, imp_t), MASK_SCORE)
    sel = _topk_rounds(score, blk, 0) & causal
    bias = jnp.where(sel, 0.0, MASK_SCORE).T.astype(bf16)
    qe = jnp.concatenate([jnp.concatenate([bias] * HPG, axis=0), q], axis=1)

    n_wk = WINDOW + Q_BLOCK
    w0 = pl.multiple_of(jnp.maximum(q0 - WINDOW, 0), Q_BLOCK)
    dist = q_pos(n_wk) - (w0 + lax.broadcasted_iota(jnp.int32, (n_wk, rows), 0))
    ok = (dist >= 0) & (dist < WINDOW)
    e, l_w = softmax0(jnp.where(ok, _nt(kw_ref[0, 0, pl.ds(w0, n_wk), :], q), NEG_INF), ok)
    o_w = (jnp.dot(vwt_ref[0, 0, :, pl.ds(w0, n_wk)], e.astype(bf16), preferred_element_type=f32) / l_w).T

    def sel_scores(off):
        return _nt(ke_ref[0, 0, pl.ds(off, SEL_KT), :], qe)

    def pv(j, p):
        off = pl.multiple_of(jnp.maximum(j, 0) * SEL_KT, SEL_KT)
        return jnp.dot(vst_ref[0, 0, :, pl.ds(off, SEL_KT)], p, preferred_element_type=f32)

    def softmax_step(s, m, l):
        m_new = jnp.maximum(m, jnp.max(s, axis=0, keepdims=True))
        alpha = jnp.exp(m - m_new)
        p = jnp.exp(s - m_new)
        return m_new, alpha * l + jnp.sum(p, axis=0, keepdims=True), alpha, p.astype(bf16)

    n_full = q0 // SEL_KT
    s_scr[...] = sel_scores(0)
    p_scr[...] = jnp.zeros((SEL_KT, rows), bf16)

    def sel_body(j, carry):
        m, l, acc, alpha_prev = carry
        s_next = sel_scores(pl.multiple_of((j + 1) * SEL_KT, SEL_KT))
        acc = alpha_prev * acc + pv(j - 1, p_scr[...])
        m, l, alpha, p = softmax_step(s_scr[...], m, l)
        p_scr[...] = p
        s_scr[...] = s_next
        return m, l, acc, alpha

    init = (jnp.full((1, rows), NEG_INF, f32), jnp.zeros((1, rows), f32), jnp.zeros((DH, rows), f32),
            jnp.ones((1, rows), f32))
    m, l, acc, alpha_prev = lax.fori_loop(0, n_full, sel_body, init)
    acc = alpha_prev * acc + pv(n_full - 1, p_scr[...])
    off = pl.multiple_of(n_full * SEL_KT, SEL_KT)
    key_pos = off + lax.broadcasted_iota(jnp.int32, (SEL_KT, rows), 0)
    s = jnp.where(key_pos <= q_pos(SEL_KT), s_scr[...], 2.0 * MASK_SCORE)
    _, l_s, alpha, p = softmax_step(s, m, l)
    acc = alpha * acc + jnp.dot(vst_ref[0, 0, :, pl.ds(off, SEL_KT)], p, preferred_element_type=f32)
    o_s = (acc / jnp.maximum(l_s, 1e-30)).T

    gt = gt_ref[0, 0]
    gate = lambda c: jnp.concatenate([gt[:, 3 * h + c:3 * h + c + 1] for h in range(HPG)], axis=0)
    o = gate(0) * o_c + gate(1) * o_s + gate(2) * o_w
    for h in range(HPG):
        o_ref[0, :, h * DH:(h + 1) * DH] = o[h * Q_BLOCK:(h + 1) * Q_BLOCK]


def nsa_prompt_attention(q, gt, ck, cvt, ke, vst, kw, vwt, n_cmp):
    bsz, t_len, _ = q.shape
    n_q = t_len // Q_BLOCK
    assert t_len % SEL_KT == 0 and t_len >= WINDOW + Q_BLOCK
    impt = _imp_map_t(NC_PAD, NB_PROMPT, n_cmp, NB_PROMPT)
    tile = pl.BlockSpec((1, Q_BLOCK, HPG * DH), lambda b, g, i: (b, i, g))
    per_bg = lambda r, w: pl.BlockSpec((1, 1, r, w), lambda b, g, i: (b, g, 0, 0))
    return pl.pallas_call(
        functools.partial(_prompt_attn_kernel, n_cmp=n_cmp),
        grid=(bsz, G, n_q),
        in_specs=[tile, pl.BlockSpec((1, 1, Q_BLOCK, 3 * HPG), lambda b, g, i: (b, g, i, 0)),
                  per_bg(NC_PAD, DH), per_bg(DH, NC_PAD),
                  pl.BlockSpec((NB_PROMPT, NC_PAD), lambda b, g, i: (0, 0)),
                  per_bg(t_len, KEXT), per_bg(DH, t_len), per_bg(t_len, DH), per_bg(DH, t_len)],
        out_specs=tile,
        out_shape=jax.ShapeDtypeStruct((bsz, t_len, G * HPG * DH), f32),
        scratch_shapes=[pltpu.VMEM((SEL_KT, QROWS), f32), pltpu.VMEM((SEL_KT, QROWS), bf16)],
        compiler_params=pltpu.CompilerParams(dimension_semantics=("parallel", "parallel", "arbitrary"),
                                             vmem_limit_bytes=VMEM_LIMIT),
        name="nsa_prompt_attention",
    )(q, gt, ck, cvt, impt, ke, vst, kw, vwt)


N_SUB = PAST_LEN // CMP_STRIDE
HALF = 2 * G * DH


def _gelu(x):
    return 0.5 * x * (1.0 + jnp.tanh(0.7978845608028654 * (x + 0.044715 * (x * x * x))))


def _sample_cmp_kernel(pt_ref, *refs):
    pages = refs[:N_PAGES]
    pe_ref, w1_ref, b1_ref, w2_ref, b2_ref, kg_ref, ck_ref, cv_ref, x_scr = refs[N_PAGES:]
    n_chunk = HALF // LANES
    for c in range(n_chunk):
        for j in range(N_PAGES):
            x_scr[c, pl.ds(j * PAGE_SIZE, PAGE_SIZE), :] = pages[j][0, :, pl.ds(c * LANES, LANES)]
        x_scr[c, pl.ds(PAST_LEN, CMP_STRIDE), :] = jnp.zeros((CMP_STRIDE, LANES), f32)
    for part in range(2):
        acc = jnp.zeros((G * N_SUB, CMP_HID), f32)
        for s in range(CMP_LEN):
            two = [x_scr[part * (G // 2) + c, pl.ds(s, N_SUB, stride=CMP_STRIDE), :] for c in range(G // 2)]
            x = jnp.concatenate([two[g // 2][:, (g % 2) * DH:(g % 2 + 1) * DH] for g in range(G)],
                                axis=0)
            x = x + pe_ref[part, s:s + 1, :]
            acc = acc + jnp.dot(x.astype(bf16), w1_ref[part, s], preferred_element_type=f32)
        hid = _gelu(acc + b1_ref[part:part + 1, :])
        out = jnp.dot(hid.astype(bf16), w2_ref[part], preferred_element_type=f32) + b2_ref[part:part + 1, :]
        if part == 0:
            out = out * lax.rsqrt(jnp.mean(out * out, axis=-1, keepdims=True) + NORM_EPS) * kg_ref[...]
        out_ref = ck_ref if part == 0 else cv_ref
        for c in range(G // 2):
            out_ref[0, c] = jnp.concatenate([out[(2 * c) * N_SUB:(2 * c + 1) * N_SUB],
                                             out[(2 * c + 1) * N_SUB:(2 * c + 2) * N_SUB]], axis=1).astype(bf16)


def sample_compress(cache3, page_table, pe, w1, b1, w2, b2, kg):
    bsz = page_table.shape[0]

    def page_spec(j):
        return pl.BlockSpec((1, PAGE_SIZE, HALF), lambda b, pt: (pt[b, j], 0, 0))

    const = lambda shape: pl.BlockSpec(shape, lambda b, pt: (0,) * len(shape))
    out_spec = pl.BlockSpec((1, G // 2, N_SUB, LANES), lambda b, pt: (b, 0, 0, 0))
    grid_spec = pltpu.PrefetchScalarGridSpec(
        num_scalar_prefetch=1, grid=(bsz,),
        in_specs=[page_spec(j) for j in range(N_PAGES)] + [
            const((2, CMP_LEN, DH)), const((2, CMP_LEN, DH, CMP_HID)), const((2, CMP_HID)),
            const((2, CMP_HID, DH)), const((2, DH)), const((1, DH))],
        out_specs=[out_spec, out_spec],
        scratch_shapes=[pltpu.VMEM((HALF // LANES, PAST_LEN + CMP_STRIDE, LANES), f32)])
    return pl.pallas_call(
        _sample_cmp_kernel, grid_spec=grid_spec,
        out_shape=[jax.ShapeDtypeStruct((bsz, G // 2, N_SUB, LANES), bf16)] * 2,
        compiler_params=pltpu.CompilerParams(dimension_semantics=("arbitrary",), vmem_limit_bytes=VMEM_LIMIT),
        name="sample_compress",
    )(page_table, *([cache3] * N_PAGES), pe, w1.reshape(2, CMP_LEN, DH, CMP_HID).astype(bf16), b1,
      w2.astype(bf16), b2, kg)


T_NEW = 4
NEW_PAD = 128
NB_SAMPLE = -(-(PAST_LEN + T_NEW) // SEL_BLOCK)
N_CMP_SAMPLE = (PAST_LEN + T_NEW) // CMP_STRIDE - 1
ROWS_S = HPG * T_NEW


def _sample_attn_kernel(pt_ref, *refs):
    pages = refs[:N_PAGES]
    (q_ref, gt_ref, ck_ref, cv_ref, new_ref, win_ref, wnew_ref, impt_ref, oh_ref, ohn_ref, o_ref) = refs[N_PAGES:]
    rows = 2 * ROWS_S
    t_of_row = lax.broadcasted_iota(jnp.int32, (rows, LANES), 0) & (T_NEW - 1)
    lane = lax.broadcasted_iota(jnp.int32, (rows, LANES), 1)
    q_pos = PAST_LEN + t_of_row
    hs_o = lax.broadcasted_iota(jnp.int32, (SUBLANES, rows), 0)
    hs_r = lax.broadcasted_iota(jnp.int32, (SUBLANES, rows), 1)
    hsum = ((hs_o >> 2 == hs_r >> 4) & ((hs_o & 3) == (hs_r & 3))).astype(bf16)
    sp_r = lax.broadcasted_iota(jnp.int32, (rows, SUBLANES), 0)
    sp_o = lax.broadcasted_iota(jnp.int32, (rows, SUBLANES), 1)
    spread = ((sp_o >> 2 == sp_r >> 4) & ((sp_o & 3) == (sp_r & 3))).astype(bf16)
    blk8 = lax.broadcasted_iota(jnp.int32, (SUBLANES, LANES), 1)
    qp8 = PAST_LEN + (lax.broadcasted_iota(jnp.int32, (SUBLANES, LANES), 0) & (T_NEW - 1))
    qb8 = qp8 >> 6
    forced = (blk8 == 0) | (blk8 == qb8) | (blk8 == qb8 - 1)
    causal = (blk8 * SEL_BLOCK <= qp8) & (blk8 < NB_SAMPLE)
    n_buf = win_ref.shape[1]

    for c in range(G // 2):
        lanes_k = pl.ds(c * LANES, LANES)
        lanes_v = pl.ds(G * DH + c * LANES, LANES)
        q = q_ref[0, c]
        gt = gt_ref[0, c]
        s_c = _nt(q, ck_ref[0, c])
        ok_c = (lane * CMP_STRIDE + (CMP_LEN - 1) <= q_pos) & (lane < N_CMP_SAMPLE)
        p_c = _masked_softmax(s_c, ok_c)
        o_c = jnp.dot(p_c.astype(bf16), cv_ref[0, c], preferred_element_type=f32)
        hi, lo = _split_bf16(p_c)
        psum = jnp.dot(hsum, hi, preferred_element_type=f32) + jnp.dot(hsum, lo, preferred_element_type=f32)
        hi, lo = _split_bf16(psum)
        imp = _nt(hi, impt_ref[...]) + _nt(lo, impt_ref[...])
        score = jnp.where(blk8 < NB_SAMPLE,
                          jnp.where(causal, jnp.where(forced, FORCE_SCORE, imp), MASK_SCORE), PICKED)
        sel = _topk_by_rank(score, blk8, NB_SAMPLE) & causal
        bias8 = jnp.where(sel, 0.0, MASK_SCORE).astype(bf16)
        bias = jnp.dot(spread, bias8, preferred_element_type=f32).astype(bf16)
        k_all = jnp.concatenate([pages[j][0, :, lanes_k] for j in range(N_PAGES)], axis=0).astype(bf16)
        s_past = _nt(q, k_all) + jnp.dot(bias, oh_ref[...], preferred_element_type=f32)
        s_new = _nt(q, new_ref[0, :, lanes_k].astype(bf16)) + jnp.dot(bias, ohn_ref[...],
                                                                      preferred_element_type=f32)
        s_new = jnp.where(lane <= t_of_row, s_new, 2.0 * MASK_SCORE)
        s_all = jnp.concatenate([s_past, s_new], axis=1)
        e = jnp.exp(s_all - jnp.max(s_all, axis=1, keepdims=True))
        den = jnp.maximum(jnp.sum(e, axis=1, keepdims=True), 1e-30)
        v_all = jnp.concatenate([pages[j][0, :, lanes_v] for j in range(N_PAGES)] + [new_ref[0, :, lanes_v]],
                                axis=0).astype(bf16)
        o_s = jnp.dot(e.astype(bf16), v_all, preferred_element_type=f32) / den
        kw_all = jnp.concatenate([win_ref[0, :, lanes_k], wnew_ref[0, :, lanes_k]], axis=0).astype(bf16)
        idx = lax.broadcasted_iota(jnp.int32, (rows, n_buf), 1)
        t_w = lax.broadcasted_iota(jnp.int32, (rows, n_buf), 0) & (T_NEW - 1)
        w_pos = PAST_LEN - n_buf + idx
        dist = PAST_LEN + t_w - w_pos
        ok_w = jnp.concatenate([(dist >= 0) & (dist < WINDOW) & (w_pos >= 0), lane <= t_of_row], axis=1)
        p_w = _masked_softmax(_nt(q, kw_all), ok_w).astype(bf16)
        vw_all = jnp.concatenate([win_ref[0, :, lanes_v], wnew_ref[0, :, lanes_v]], axis=0).astype(bf16)
        o_w = jnp.dot(p_w, vw_all, preferred_element_type=f32)
        o_ref[0, c] = gt[:, 0:1] * o_c + gt[:, 1:2] * o_s + gt[:, 2:3] * o_w


def sample_attention(cache3, page_table, q2, gt2, ck2, cv2, new_sel, win_cache, new_win):
    bsz = page_table.shape[0]
    n_buf = win_cache.shape[1]
    rows = 2 * ROWS_S
    impt = _imp_map_t(N_SUB, LANES, N_CMP_SAMPLE, NB_SAMPLE)
    key_blk = np.arange(PAST_LEN) // SEL_BLOCK
    onehot = jnp.asarray(key_blk[None, :] == np.arange(LANES)[:, None], bf16)
    onehot_new = jnp.asarray(np.broadcast_to(np.arange(LANES)[:, None] == PAST_LEN // SEL_BLOCK,
                                             (LANES, NEW_PAD)), bf16)

    def page_spec(j):
        return pl.BlockSpec((1, PAGE_SIZE, HALF), lambda b, pt: (pt[b, j], 0, 1))

    per_seq = lambda shape: pl.BlockSpec((1,) + shape, lambda b, pt: (b,) + (0,) * len(shape))
    const = lambda shape: pl.BlockSpec(shape, lambda b, pt: (0,) * len(shape))
    grid_spec = pltpu.PrefetchScalarGridSpec(
        num_scalar_prefetch=1, grid=(bsz,),
        in_specs=[page_spec(j) for j in range(N_PAGES)] + [
            per_seq((G // 2, rows, LANES)), per_seq((G // 2, rows, 3)),
            per_seq((G // 2, N_SUB, LANES)), per_seq((G // 2, N_SUB, LANES)),
            per_seq((NEW_PAD, HALF)), per_seq((n_buf, HALF)), per_seq((NEW_PAD, HALF)),
            const((LANES, N_SUB)), const((LANES, PAST_LEN)), const((LANES, NEW_PAD))],
        out_specs=per_seq((G // 2, rows, LANES)))
    return pl.pallas_call(
        _sample_attn_kernel, grid_spec=grid_spec,
        out_shape=jax.ShapeDtypeStruct((bsz, G // 2, rows, LANES), f32),
        compiler_params=pltpu.CompilerParams(dimension_semantics=("arbitrary",), vmem_limit_bytes=VMEM_LIMIT),
        name="sample_attention",
    )(page_table, *([cache3] * N_PAGES), q2, gt2, ck2, cv2, new_sel, win_cache, new_win, impt, onehot, onehot_new)


def rmsnorm(x, g):
    y = x * lax.rsqrt(jnp.mean(x * x, -1, keepdims=True) + NORM_EPS)
    return y * g


def _ffn_in_kernel(x_ref, g_ref, w_ref, o_ref):
    x = x_ref[...]
    xn = x * lax.rsqrt(jnp.mean(x * x, axis=-1, keepdims=True) + NORM_EPS) * g_ref[...]
    o_ref[...] = jnp.dot(xn.astype(bf16), w_ref[...], preferred_element_type=f32)


def _ffn_out_kernel(x_ref, h_ref, halo_ref, cs_ref, cw_ref, cb_ref, w_ref, o_ref, u_scr, *, tiles_per_seq):
    first = pl.program_id(0) % tiles_per_seq == 0
    tm = h_ref.shape[0]
    u_scr[pl.ds(0, SUBLANES), :] = halo_ref[...]
    u_scr[pl.ds(SUBLANES, tm), :] = h_ref[:, pl.ds(0, D_FF)]

    @pl.when(first)
    def _():
        u_scr[pl.ds(SUBLANES - (CONV_W - 1), CONV_W - 1), :] = cs_ref[0]

    uc = cb_ref[...] + cw_ref[CONV_W - 1:CONV_W, :] * u_scr[pl.ds(SUBLANES, tm), :]
    for j in range(CONV_W - 1):
        uc = uc + cw_ref[j:j + 1, :] * u_scr[pl.ds(SUBLANES - (CONV_W - 1) + j, tm), :]
    act = _gelu(uc) * h_ref[:, pl.ds(D_FF, D_FF)]
    o_ref[...] = x_ref[...] + jnp.dot(act.astype(bf16), w_ref[...], preferred_element_type=f32)


FFN_TM = 1024
FFN_TN = 1408
FFN_OUT_TM = 256


def conv_ffn_fused(x, g, conv_state, w_in, conv_w, conv_b, w_out):
    bsz, t_len, _ = x.shape
    m = bsz * t_len
    tiles_per_seq = t_len // FFN_OUT_TM
    x2 = x.reshape(m, D_MODEL)
    h = pl.pallas_call(
        _ffn_in_kernel,
        grid=(m // FFN_TM, 2 * D_FF // FFN_TN),
        in_specs=[pl.BlockSpec((FFN_TM, D_MODEL), lambda i, j: (i, 0)),
                  pl.BlockSpec((1, D_MODEL), lambda i, j: (0, 0)),
                  pl.BlockSpec((D_MODEL, FFN_TN), lambda i, j: (0, j))],
        out_specs=pl.BlockSpec((FFN_TM, FFN_TN), lambda i, j: (i, j)),
        out_shape=jax.ShapeDtypeStruct((m, 2 * D_FF), f32),
        compiler_params=pltpu.CompilerParams(dimension_semantics=("parallel", "parallel"),
                                             vmem_limit_bytes=VMEM_LIMIT),
        name="ffn_in",
    )(x2, g.reshape(1, D_MODEL), w_in.astype(bf16))
    halo_blocks = FFN_OUT_TM // SUBLANES
    y = pl.pallas_call(
        functools.partial(_ffn_out_kernel, tiles_per_seq=tiles_per_seq),
        grid=(m // FFN_OUT_TM,),
        in_specs=[pl.BlockSpec((FFN_OUT_TM, D_MODEL), lambda i: (i, 0)),
                  pl.BlockSpec((FFN_OUT_TM, 2 * D_FF), lambda i: (i, 0)),
                  pl.BlockSpec((SUBLANES, D_FF), lambda i: (jnp.maximum(i * halo_blocks - 1, 0), 0)),
                  pl.BlockSpec((1, CONV_W - 1, D_FF), lambda i: (i // tiles_per_seq, 0, 0)),
                  pl.BlockSpec((CONV_W, D_FF), lambda i: (0, 0)),
                  pl.BlockSpec((1, D_FF), lambda i: (0, 0)),
                  pl.BlockSpec((D_FF, D_MODEL), lambda i: (0, 0))],
        out_specs=pl.BlockSpec((FFN_OUT_TM, D_MODEL), lambda i: (i, 0)),
        out_shape=jax.ShapeDtypeStruct((m, D_MODEL), f32),
        scratch_shapes=[pltpu.VMEM((SUBLANES + FFN_OUT_TM, D_FF), f32)],
        compiler_params=pltpu.CompilerParams(dimension_semantics=("arbitrary",), vmem_limit_bytes=56 * 1024 * 1024),
        name="ffn_out",
    )(x2, h, h, conv_state, conv_w, conv_b.reshape(1, D_FF), w_out.astype(bf16))
    conv_new = h.reshape(bsz, t_len, 2 * D_FF)[:, t_len - (CONV_W - 1):, :D_FF]
    return y.reshape(bsz, t_len, D_MODEL), conv_new


def conv_ffn(xn, conv_state, w_in, conv_w, conv_b, w_out):
    t_len = xn.shape[1]
    u, gate = jnp.split(mm3(xn, w_in), 2, axis=-1)
    u_pad = jnp.concatenate([conv_state, u], axis=1)
    uc = conv_b + conv_w[0] * u_pad[:, 0:t_len]
    for j in range(1, CONV_W):
        uc = uc + conv_w[j] * u_pad[:, j:j + t_len]
    return mm3(jax.nn.gelu(uc) * gate, w_out), u_pad[:, t_len:]


RW_TM = 256
SEG_W = LANES


def _head_seg():
    seg = np.arange(D_MODEL)[:, None] // RW_HEAD_DIM == np.arange(SEG_W)[None, :]
    return jnp.asarray(seg, bf16), jnp.asarray(seg.T, bf16)


def _head_sum(x, seg_ref):
    hi, lo = _split_bf16(x)
    return jnp.dot(hi, seg_ref[...], preferred_element_type=f32) + jnp.dot(lo, seg_ref[...],
                                                                           preferred_element_type=f32)


def _head_expand(s, segt_ref):
    hi, lo = _split_bf16(s)
    return jnp.dot(hi, segt_ref[...], preferred_element_type=f32) + jnp.dot(lo, segt_ref[...],
                                                                            preferred_element_type=f32)


def _rwkv_pre_kernel(*refs, has_vfirst):
    it = iter(refs)
    xn_ref, xs_ref = next(it), next(it)
    vf_ref = next(it) if has_vfirst else None
    mu_ref, wr, wk, wv, w1, w2, a1, a2 = (next(it) for _ in range(8))
    v1, v2 = (next(it), next(it)) if has_vfirst else (None, None)
    g1, g2, w0, a0 = (next(it) for _ in range(4))
    v0 = next(it) if has_vfirst else None
    kk_ref, ka_ref, seg_ref, segt_ref = (next(it) for _ in range(4))
    r_o, w_o, k_o, v_o, nkk_o, kka_o, g_o = (next(it) for _ in range(7))

    dot = lambda x, w: jnp.dot(x.astype(bf16), w[...], preferred_element_type=f32)
    xn = xn_ref[...]
    xx = xs_ref[...] - xn
    xr, xw, xk, xv, xa, xg = (xn + xx * mu_ref[i:i + 1, :] for i in range(6))
    r = dot(xr, wr)
    k = dot(xk, wk)
    v = dot(xv, wv)
    z = -(w0[...] + dot(jnp.tanh(dot(xw, w1)), w2))
    softplus = jnp.maximum(z, 0.0) + jnp.log1p(jnp.exp(-jnp.abs(z)))
    decay = jnp.exp(-jnp.exp(-softplus - 0.5))
    a = jax.nn.sigmoid(a0[...] + dot(dot(xa, a1), a2))
    if has_vfirst:
        v = v + (vf_ref[...] - v) * jax.nn.sigmoid(v0[...] + dot(dot(xv, v1), v2))
    g = dot(jax.nn.sigmoid(dot(xg, g1)), g2)
    kk = k * kk_ref[...]
    norm = jnp.maximum(jnp.sqrt(_head_sum(kk * kk, seg_ref)), 1e-12)
    kk = kk * _head_expand(1.0 / norm, segt_ref)
    r_o[...] = r
    w_o[...] = decay
    k_o[...] = k * (1.0 + (a - 1.0) * ka_ref[...])
    v_o[...] = v
    nkk_o[...] = -kk
    kka_o[...] = kk * a
    g_o[...] = g


def rwkv_pre(xn, xs, v_first, li, p):
    m = xn.shape[0]
    tm = RW_TM if m % RW_TM == 0 else m
    has_vfirst = v_first is not None
    seg, segt = _head_seg()
    row = lambda a: a.reshape(1, -1)
    cast = lambda w: w.astype(bf16)
    ops = [xn, xs] + ([v_first] if has_vfirst else [])
    ops += [p['rw_mu'][li], cast(p['rw_w_r'][li]), cast(p['rw_w_k'][li]), cast(p['rw_w_v'][li]),
            cast(p['rw_w1'][li]), cast(p['rw_w2'][li]), cast(p['rw_a1'][li]), cast(p['rw_a2'][li])]
    if has_vfirst:
        ops += [cast(p['rw_v1'][li - 1]), cast(p['rw_v2'][li - 1])]
    ops += [cast(p['rw_g1'][li]), cast(p['rw_g2'][li]), row(p['rw_w0'][li]), row(p['rw_a0'][li])]
    if has_vfirst:
        ops += [row(p['rw_v0'][li - 1])]
    ops += [row(p['rw_k_k'][li]), row(p['rw_k_a'][li]), seg, segt]
    n_rows = 3 if has_vfirst else 2
    tile = pl.BlockSpec((tm, D_MODEL), lambda i: (i, 0))
    whole = lambda a: pl.BlockSpec(a.shape, lambda i: (0,) * a.ndim)
    return pl.pallas_call(
        functools.partial(_rwkv_pre_kernel, has_vfirst=has_vfirst),
        grid=(m // tm,),
        in_specs=[tile] * n_rows + [whole(a) for a in ops[n_rows:]],
        out_specs=[tile] * 7,
        out_shape=[jax.ShapeDtypeStruct((m, D_MODEL), f32)] * 7,
        compiler_params=pltpu.CompilerParams(dimension_semantics=("parallel",), vmem_limit_bytes=VMEM_LIMIT),
        name="rwkv_pre",
    )(*ops)


def _rwkv_post_kernel(y_ref, r_ref, k_ref, v_ref, g_ref, x_ref, lnw, lnb, rk, wo, seg_ref, segt_ref, o_ref):
    y = y_ref[...]
    inv_n = 1.0 / RW_HEAD_DIM
    yc = y - _head_expand(_head_sum(y, seg_ref) * inv_n, segt_ref)
    var = _head_expand(_head_sum(yc * yc, seg_ref) * inv_n, segt_ref)
    yn = yc * lax.rsqrt(var + RW_GN_EPS) * lnw[...] + lnb[...]
    bonus = _head_expand(_head_sum(r_ref[...] * k_ref[...] * rk[...], seg_ref), segt_ref) * v_ref[...]
    out = ((yn + bonus) * g_ref[...]).astype(bf16)
    o_ref[...] = x_ref[...] + jnp.dot(out, wo[...], preferred_element_type=f32)


def rwkv_post(y, r, k, v, g, x, li, p):
    m = y.shape[0]
    tm = RW_TM if m % RW_TM == 0 else m
    seg, segt = _head_seg()
    row = lambda a: a.reshape(1, -1)
    consts = [row(p['rw_ln_w'][li]), row(p['rw_ln_b'][li]), row(p['rw_r_k'][li]), p['rw_w_o'][li].astype(bf16),
              seg, segt]
    tile = pl.BlockSpec((tm, D_MODEL), lambda i: (i, 0))
    whole = lambda a: pl.BlockSpec(a.shape, lambda i: (0,) * a.ndim)
    return pl.pallas_call(
        _rwkv_post_kernel,
        grid=(m // tm,),
        in_specs=[tile] * 6 + [whole(a) for a in consts],
        out_specs=tile,
        out_shape=jax.ShapeDtypeStruct((m, D_MODEL), f32),
        compiler_params=pltpu.CompilerParams(dimension_semantics=("parallel",), vmem_limit_bytes=VMEM_LIMIT),
        name="rwkv_post",
    )(y, r, k, v, g, x, *consts)


def rwkv7_layer(x, shift_prev, s0, v_first, li, p, scan_cfg):
    bsz, t_len, _ = x.shape
    m = bsz * t_len
    xn = rmsnorm(x, p['norm_g'][li, 0])
    xs = jnp.concatenate([shift_prev[:, None], xn[:, :-1]], axis=1)
    flat = lambda a: a.reshape(m, D_MODEL)
    r, decay, k, v, nkk, kka, g = rwkv_pre(flat(xn), flat(xs), None if li == 0 else flat(v_first), li, p)
    if li == 0:
        v_first = v
    seq = lambda a: a.reshape(bsz, t_len, D_MODEL)
    y, s_new = wkv_scan(seq(r), seq(decay), seq(k), seq(v), seq(nkk), seq(kka), to_pairs(s0), **scan_cfg)
    x_new = rwkv_post(flat(y), r, k, v, g, flat(x), li, p)
    return seq(x_new), from_pairs(s_new, bsz), xn[:, -1], seq(v_first)


def kv_rows(h, p):
    bsz, t_len, _ = h.shape
    kv = mm3(rmsnorm(h, p['kv_norm_g']), p['w_kv']).reshape(bsz, t_len, 6, G, DH)
    k_sel = rmsnorm(kv[:, :, 2], p['k_norm_g'][1])
    k_win = rmsnorm(kv[:, :, 4], p['k_norm_g'][2])
    paged = jnp.stack([kv[:, :, 0], kv[:, :, 1], k_sel, kv[:, :, 3]], axis=2)
    win = jnp.stack([k_win, kv[:, :, 5]], axis=2)
    return paged, win


def nsa_query(xn, lb, p):
    bsz, t_len, _ = xn.shape
    n_q = NSA_HEADS * DH
    w = jnp.pad(p['nsa_w_q'][lb], ((0, 0), (0, LANES - 3 * NSA_HEADS)))
    proj = mm3(xn, w)
    q = proj[..., :n_q].reshape(bsz, t_len, G, HPG, DH)
    q = rmsnorm(q, p['q_norm_g'][lb]) * (DH ** -0.5)
    gates = jax.nn.sigmoid(proj[..., n_q:n_q + 3 * NSA_HEADS]).reshape(bsz, t_len, G, HPG, 3)
    return q, gates


def compress_rows_prompt(rows, pe, w1, b1, w2, b2):
    bsz, t_len = rows.shape[:2]
    n_sub = t_len // CMP_STRIDE
    sub = rows.reshape(bsz, n_sub, CMP_STRIDE, G, DH).transpose(0, 1, 3, 2, 4)
    half = CMP_STRIDE * DH
    x0 = (sub + pe[:CMP_STRIDE][None, None, None]).reshape(bsz * n_sub * G, half)
    x1 = (sub + pe[CMP_STRIDE:][None, None, None]).reshape(bsz * n_sub * G, half)
    p0 = mm(x0, w1[:half]).reshape(bsz, n_sub, G, CMP_HID)
    p1 = mm(x1, w1[half:]).reshape(bsz, n_sub, G, CMP_HID)
    hid = jax.nn.gelu(p0[:, :-1] + p1[:, 1:] + b1)
    out = mm(hid.reshape(bsz * (n_sub - 1) * G, CMP_HID), w2) + b2
    return out.reshape(bsz, n_sub - 1, G, DH)


def prompt_context(h, p):
    paged, win = kv_rows(h, p)
    bsz, t_len = h.shape[:2]
    ck = rmsnorm(compress_rows_prompt(paged[:, :, 0], p['cmp_pe'][0], p['cmp_w1'][0], p['cmp_b1'][0],
                                      p['cmp_w2'][0], p['cmp_b2'][0]), p['k_norm_g'][0])
    cv = compress_rows_prompt(paged[:, :, 1], p['cmp_pe'][1], p['cmp_w1'][1], p['cmp_b1'][1],
                              p['cmp_w2'][1], p['cmp_b2'][1])
    n_cmp = ck.shape[1]
    pad = lambda c: jnp.pad(c, ((0, 0), (0, NC_PAD - n_cmp), (0, 0), (0, 0)))
    heads = lambda x: x.transpose(0, 2, 1, 3).astype(bf16)
    heads_t = lambda x: x.transpose(0, 2, 3, 1).astype(bf16)
    onehot = jnp.asarray(np.arange(t_len)[:, None] // SEL_BLOCK == np.arange(NB_PROMPT)[None, :], bf16)
    ke = jnp.concatenate([jnp.broadcast_to(onehot, (bsz, G, t_len, NB_PROMPT)), heads(paged[:, :, 2])], axis=-1)
    ctx = dict(ck=heads(pad(ck)), cvt=heads_t(pad(cv)), n_cmp=n_cmp, ke=ke, vst=heads_t(paged[:, :, 3]),
               kw=heads(win[:, :, 0]), vwt=heads_t(win[:, :, 1]))
    keep = min(WINDOW, t_len)
    return ctx, (paged, win[:, t_len - keep:])


def attend_prompt(q, gates, ctx):
    bsz, t_len = q.shape[:2]
    gt = gates.reshape(bsz, t_len, G, 3 * HPG).transpose(0, 2, 1, 3)
    return nsa_prompt_attention(q.reshape(bsz, t_len, NSA_HEADS * DH).astype(bf16), gt, ctx['ck'], ctx['cvt'],
                                ctx['ke'], ctx['vst'], ctx['kw'], ctx['vwt'], ctx['n_cmp'])


def sample_context(h, cache3, page_table, cache_kv_win, p):
    paged, win = kv_rows(h, p)
    bsz, t_new = h.shape[:2]
    ck, cv = sample_compress(cache3, page_table, p['cmp_pe'], p['cmp_w1'], p['cmp_b1'], p['cmp_w2'], p['cmp_b2'],
                             p['k_norm_g'][0:1])
    n_buf = cache_kv_win.shape[1]
    pad_rows = lambda x: jnp.pad(x.reshape(bsz, t_new, HALF), ((0, 0), (0, NEW_PAD - t_new), (0, 0)))
    ctx = dict(ck=ck, cv=cv, new_sel=pad_rows(paged[:, :, 2:4]), new_win=pad_rows(win),
               win_cache=cache_kv_win.reshape(bsz, n_buf, HALF))
    win_full = jnp.concatenate([cache_kv_win, win], axis=1)
    return ctx, (paged, win_full[:, win_full.shape[1] - n_buf:])


def attend_sample(q, gates, ctx, cache3, page_table):
    bsz, t_new = q.shape[:2]
    rows = lambda x: x.transpose(0, 2, 3, 1, 4).reshape(bsz, G // 2, 2 * HPG * t_new, x.shape[-1])
    qr = rows(q)
    own = (jnp.arange(2 * HPG * t_new) // (HPG * t_new))[:, None] == (jnp.arange(LANES) // DH)[None, :]
    q2 = jnp.where(own, jnp.concatenate([qr, qr], axis=-1), 0.0).astype(bf16)
    o = sample_attention(cache3, page_table, q2, rows(gates), ctx['ck'], ctx['cv'],
                         ctx['new_sel'], ctx['win_cache'], ctx['new_win'])
    o = jnp.where(own, o, 0.0)
    o = o[..., :DH] + o[..., DH:]
    return o.reshape(bsz, G, HPG, t_new, DH).transpose(0, 3, 1, 2, 4).reshape(bsz, t_new, NSA_HEADS * DH)


def forward(x, wkv_in, shift_in, conv_in, make_ctx, attend, p, scan_cfg):
    wkv_out, shift_out, conv_out = [], [], []
    v_first = None
    ctx = None
    kv_state = None
    for layer in range(DEPTH):
        if layer == N_A_LAYERS:
            ctx, kv_state = make_ctx(x)
        if layer < N_A_LAYERS:
            x, s_new, sh_new, v_first = rwkv7_layer(x, shift_in[layer], wkv_in[layer], v_first, layer, p, scan_cfg)
            wkv_out.append(s_new)
            shift_out.append(sh_new)
        else:
            lb = layer - N_A_LAYERS
            q, gates = nsa_query(rmsnorm(x, p['norm_g'][layer, 0]), lb, p)
            x = x + mm3(attend(q, gates, ctx), p['nsa_w_o'][lb])
        ffn_w = (p['ffn_w_in'][layer], p['ffn_conv_w'][layer], p['ffn_conv_b'][layer], p['ffn_w_out'][layer])
        if x.shape[1] % FFN_TM == 0:
            x, conv_new = conv_ffn_fused(x, p['norm_g'][layer, 1], conv_in[layer], *ffn_w)
        else:
            f, conv_new = conv_ffn(rmsnorm(x, p['norm_g'][layer, 1]), conv_in[layer], *ffn_w)
            x = x + f
        conv_out.append(conv_new)
    return x, kv_state, jnp.stack(wkv_out), jnp.stack(shift_out), jnp.stack(conv_out)


def kernel(x_prompt, x_sample, cache_kv, cache_kv_win, state_wkv, state_shift, state_ffn_conv, page_table, norm_g, rw_mu, rw_w_r, rw_w_k, rw_w_v, rw_w_o, rw_w0, rw_w1, rw_w2, rw_a0, rw_a1, rw_a2, rw_v0, rw_v1, rw_v2, rw_g1, rw_g2, rw_k_k, rw_k_a, rw_r_k, rw_ln_w, rw_ln_b, kv_norm_g, w_kv, k_norm_g, cmp_pe, cmp_w1, cmp_b1, cmp_w2, cmp_b2, nsa_w_q, q_norm_g, nsa_w_o, ffn_w_in, ffn_conv_w, ffn_conv_b, ffn_w_out):
    p = dict(norm_g=norm_g, rw_mu=rw_mu, rw_w_r=rw_w_r, rw_w_k=rw_w_k, rw_w_v=rw_w_v, rw_w_o=rw_w_o,
             rw_w0=rw_w0, rw_w1=rw_w1, rw_w2=rw_w2, rw_a0=rw_a0, rw_a1=rw_a1, rw_a2=rw_a2,
             rw_v0=rw_v0, rw_v1=rw_v1, rw_v2=rw_v2, rw_g1=rw_g1, rw_g2=rw_g2, rw_k_k=rw_k_k, rw_k_a=rw_k_a,
             rw_r_k=rw_r_k, rw_ln_w=rw_ln_w, rw_ln_b=rw_ln_b, kv_norm_g=kv_norm_g, w_kv=w_kv,
             k_norm_g=k_norm_g, cmp_pe=cmp_pe, cmp_w1=cmp_w1, cmp_b1=cmp_b1, cmp_w2=cmp_w2, cmp_b2=cmp_b2,
             nsa_w_q=nsa_w_q, q_norm_g=q_norm_g, nsa_w_o=nsa_w_o, ffn_w_in=ffn_w_in,
             ffn_conv_w=ffn_conv_w, ffn_conv_b=ffn_conv_b, ffn_w_out=ffn_w_out)
    bp, t_p = x_prompt.shape[:2]
    bs = x_sample.shape[0]

    wkv0 = jnp.zeros((N_A_LAYERS, bp, RW_HEADS, RW_HEAD_DIM, RW_HEAD_DIM), f32)
    shift0 = jnp.zeros((N_A_LAYERS, bp, D_MODEL), f32)
    conv0 = jnp.zeros((DEPTH, bp, CONV_W - 1, D_FF), f32)
    y_prompt, (kv_prompt, win_prompt), wkv_prompt, shift_prompt, conv_prompt = forward(
        x_prompt, wkv0, shift0, conv0, lambda h: prompt_context(h, p), attend_prompt, p,
        dict(bb=bp, tc=128))

    cache3 = cache_kv.reshape(cache_kv.shape[0], PAGE_SIZE, 4 * G * DH)
    y_sample, (kv_sample, win_sample), wkv_sample, shift_sample, conv_sample = forward(
        x_sample, state_wkv, state_shift, state_ffn_conv,
        lambda h: sample_context(h, cache3, page_table, cache_kv_win, p),
        lambda q, gates, ctx: attend_sample(q, gates, ctx, cache3, page_table), p,
        dict(bb=2, tc=x_sample.shape[1]))

    return (y_prompt, y_sample, kv_prompt, kv_sample, win_prompt, win_sample,
            wkv_prompt, wkv_sample, shift_prompt, shift_sample, conv_prompt, conv_sample)
```

```python
import functools

import numpy as np
import jax
import jax.numpy as jnp
from jax import lax
from jax.experimental import pallas as pl
from jax.experimental.pallas import tpu as pltpu

f32 = jnp.float32
bf16 = jnp.bfloat16

D_MODEL = 1024
DEPTH = 4
N_A_LAYERS = 2
PAST_LEN = 2048
PAGE_SIZE = 128
N_PAGES = PAST_LEN // PAGE_SIZE
RW_HEAD_DIM = 64
RW_HEADS = 16
RW_GN_EPS = 64e-5
DH = 64
NSA_HEADS = 16
G = 4
HPG = 4
CMP_LEN = 32
CMP_STRIDE = 16
CMP_HID = 128
SEL_BLOCK = 64
N_SELECT = 16
WINDOW = 512
Q_BLOCK = 128
D_FF = 2816
CONV_W = 3
NORM_EPS = 1e-6
NEG_INF = -1e30
FORCE_SCORE = 1e9
MASK_SCORE = -1e9
PICKED = -3e38

LANES = 128
SUBLANES = 8
VMEM_LIMIT = 48 * 1024 * 1024


def _nt(a, b):
    return lax.dot_general(a, b, (((1,), (1,)), ((), ())), preferred_element_type=f32)


def _split_bf16(x):
    hi = x.astype(bf16)
    lo = (x - hi.astype(f32)).astype(bf16)
    return hi, lo


def _mm_kernel(x_ref, w_ref, o_ref):
    o_ref[...] = jnp.dot(x_ref[...].astype(bf16), w_ref[...], preferred_element_type=f32)


MM_TM = 512


def _pick_tile(n, cands):
    for c in cands:
        if n % c == 0:
            return c
    return n


def mm(x, w):
    m, k = x.shape
    n = w.shape[1]
    w = w.astype(bf16)
    tm = MM_TM if m >= MM_TM else -(-m // SUBLANES) * SUBLANES
    mp = -(-m // tm) * tm
    if mp != m:
        x = jnp.pad(x, ((0, mp - m), (0, 0)))
    tn = n if n <= 1536 else _pick_tile(n, (1408, 1024, 512, 256, 128))
    out = pl.pallas_call(
        _mm_kernel,
        grid=(mp // tm, n // tn),
        in_specs=[pl.BlockSpec((tm, k), lambda i, j: (i, 0)),
                  pl.BlockSpec((k, tn), lambda i, j: (0, j))],
        out_specs=pl.BlockSpec((tm, tn), lambda i, j: (i, j)),
        out_shape=jax.ShapeDtypeStruct((mp, n), f32),
        compiler_params=pltpu.CompilerParams(dimension_semantics=("parallel", "parallel"),
                                             vmem_limit_bytes=VMEM_LIMIT),
        name="mm",
    )(x, w)
    return out[:m] if mp != m else out


def mm3(x, w):
    b, t, k = x.shape
    return mm(x.reshape(b * t, k), w).reshape(b, t, w.shape[1])


PAIR_W = 2 * RW_HEAD_DIM
N_PAIRS = RW_HEADS // 2


def _scan_kernel(r_ref, w_ref, k_ref, v_ref, a_ref, b_ref, s0_ref, y_ref, s_ref, *, bb, tc):
    n = RW_HEAD_DIM
    npairs = bb * N_PAIRS

    @pl.when(pl.program_id(1) == 0)
    def _():
        s_ref[...] = s0_ref[...]

    row = lax.broadcasted_iota(jnp.int32, (n, PAIR_W), 0)
    lane = lax.broadcasted_iota(jnp.int32, (n, PAIR_W), 1)
    diag = (lane % n) == row
    kk = lax.broadcasted_iota(jnp.int32, (PAIR_W, PAIR_W), 0) // n
    ll = lax.broadcasted_iota(jnp.int32, (PAIR_W, PAIR_W), 1) // n
    bones = (kk == ll).astype(bf16)
    group = min(tc, SUBLANES)

    def seg_sum_bcast(x):
        return jnp.dot(x.astype(bf16), bones, preferred_element_type=f32)

    def token(t0, s):
        def rowvec(ref, bi, hp):
            tile = ref[bi, pl.ds(t0, group), pl.ds(hp * PAIR_W, PAIR_W)]
            return jnp.broadcast_to(tile[s:s + 1], (n, PAIR_W))

        pairs = [(bi, hp) for bi in range(bb) for hp in range(N_PAIRS)]
        st = [s_ref[p] for p in range(npairs)]
        sa_in = jnp.concatenate([st[p] * rowvec(a_ref, *pairs[p]) for p in range(npairs)], axis=0)
        v_in = jnp.concatenate([jnp.where(diag, rowvec(v_ref, *pairs[p]), 0.0) for p in range(npairs)], axis=0)
        sa_col = seg_sum_bcast(sa_in)
        v_col = seg_sum_bcast(v_in)
        new = []
        for p in range(npairs):
            sl = slice(p * n, (p + 1) * n)
            sn = (st[p] * rowvec(w_ref, *pairs[p]) + sa_col[sl] * rowvec(b_ref, *pairs[p])
                  + v_col[sl] * rowvec(k_ref, *pairs[p]))
            s_ref[p] = sn
            new.append(sn)
        y_in = jnp.concatenate([new[p] * rowvec(r_ref, *pairs[p]) for p in range(npairs)], axis=0)
        y_col = seg_sum_bcast(y_in)
        rows = []
        for p in range(npairs):
            sl = slice(p * n, (p + 1) * n)
            rows.append(jnp.sum(jnp.where(diag, y_col[sl], 0.0), axis=0, keepdims=True))
        return rows

    def block(g, carry):
        t0 = pl.multiple_of(g * group, group)
        rows = [token(t0, s) for s in range(group)]
        for p in range(npairs):
            bi, hp = divmod(p, N_PAIRS)
            y_ref[bi, pl.ds(t0, group), pl.ds(hp * PAIR_W, PAIR_W)] = jnp.concatenate(
                [rows[s][p] for s in range(group)], axis=0)
        return carry

    lax.fori_loop(0, tc // group, block, 0)


def wkv_scan(r, w, k, v, a, b, s0_pairs, *, bb, tc):
    bsz, t_len, _ = r.shape
    assert bsz % bb == 0 and t_len % tc == 0
    vec_spec = pl.BlockSpec((bb, tc, D_MODEL), lambda i, j: (i, j, 0))
    st_spec = pl.BlockSpec((bb * N_PAIRS, RW_HEAD_DIM, PAIR_W), lambda i, j: (i, 0, 0))
    return pl.pallas_call(
        functools.partial(_scan_kernel, bb=bb, tc=tc),
        grid=(bsz // bb, t_len // tc),
        in_specs=[vec_spec] * 6 + [st_spec],
        out_specs=[vec_spec, st_spec],
        out_shape=[jax.ShapeDtypeStruct((bsz, t_len, D_MODEL), f32),
                   jax.ShapeDtypeStruct((bsz * N_PAIRS, RW_HEAD_DIM, PAIR_W), f32)],
        compiler_params=pltpu.CompilerParams(dimension_semantics=("arbitrary", "arbitrary"),
                                             vmem_limit_bytes=VMEM_LIMIT),
        name="wkv_scan",
    )(r, w, k, v, a, b, s0_pairs)


def to_pairs(s):
    bsz = s.shape[0]
    return s.reshape(bsz, N_PAIRS, 2, RW_HEAD_DIM, RW_HEAD_DIM).transpose(0, 1, 3, 2, 4).reshape(
        bsz * N_PAIRS, RW_HEAD_DIM, PAIR_W)


def from_pairs(s, bsz):
    return s.reshape(bsz, N_PAIRS, RW_HEAD_DIM, 2, RW_HEAD_DIM).transpose(0, 1, 3, 2, 4).reshape(
        bsz, RW_HEADS, RW_HEAD_DIM, RW_HEAD_DIM)


def _masked_softmax(s, mask):
    s = jnp.where(mask, s, NEG_INF)
    e = jnp.where(mask, jnp.exp(s - jnp.max(s, axis=-1, keepdims=True)), 0.0)
    return e / jnp.maximum(jnp.sum(e, axis=-1, keepdims=True), 1e-30)


def _topk_rounds(score, idx, axis):
    sel = jnp.zeros(score.shape, f32)
    idx = idx.astype(f32)
    for _ in range(N_SELECT):
        m = jnp.max(score, axis=axis, keepdims=True)
        first = jnp.min(jnp.where(score == m, idx, 1e6), axis=axis, keepdims=True)
        pick = idx == first
        sel = jnp.where(pick, 1.0, sel)
        score = jnp.where(pick, PICKED, score)
    return sel > 0.5


def _topk_by_rank(score, idx, n_valid):
    rank = jnp.zeros(score.shape, f32)
    for i in range(n_valid):
        ci = score[:, i:i + 1]
        beats = (ci > score) | ((ci == score) & (idx > i))
        rank = rank + jnp.where(beats, 1.0, 0.0)
    return rank < (N_SELECT - 0.5)


def _imp_map_t(n_cmp_pad, n_blk_pad, n_cmp, n_blk):
    ci = np.arange(n_cmp_pad)[None, :] * CMP_STRIDE
    sj = np.arange(n_blk_pad)[:, None] * SEL_BLOCK
    ov = np.clip(np.minimum(ci + CMP_LEN, sj + SEL_BLOCK) - np.maximum(ci, sj), 0, None) / CMP_STRIDE
    ov = ov * (np.arange(n_cmp_pad)[None, :] < n_cmp) * (np.arange(n_blk_pad)[:, None] < n_blk)
    return jnp.asarray(ov, bf16)


NC_PAD = 512
NB_PROMPT = 128
KEXT = NB_PROMPT + DH
SEL_KT = 4 * Q_BLOCK
QROWS = HPG * Q_BLOCK


def _prompt_attn_kernel(q_ref, gt_ref, ck_ref, cvt_ref, impt_ref, ke_ref, vst_ref, kw_ref, vwt_ref, o_ref,
                        s_scr, p_scr, *, n_cmp):
    qi = pl.program_id(2)
    rows = QROWS
    q0 = qi * Q_BLOCK
    q = jnp.concatenate([q_ref[0, :, h * DH:(h + 1) * DH] for h in range(HPG)], axis=0)

    def q_pos(n_keys):
        return q0 + (lax.broadcasted_iota(jnp.int32, (n_keys, rows), 1) & (Q_BLOCK - 1))

    def softmax0(s, ok):
        e = jnp.where(ok, jnp.exp(s - jnp.max(s, axis=0, keepdims=True)), 0.0)
        return e, jnp.maximum(jnp.sum(e, axis=0, keepdims=True), 1e-30)

    nn = lax.broadcasted_iota(jnp.int32, (NC_PAD, rows), 0)
    ok = (nn * CMP_STRIDE + (CMP_LEN - 1) <= q_pos(NC_PAD)) & (nn < n_cmp)
    e, l_c = softmax0(jnp.where(ok, _nt(ck_ref[0, 0], q), NEG_INF), ok)
    p_c = e / l_c
    o_c = jnp.dot(cvt_ref[0, 0], p_c.astype(bf16), preferred_element_type=f32).T
    psum = p_c[:, 0:Q_BLOCK]
    for h in range(1, HPG):
        psum = psum + p_c[:, h * Q_BLOCK:(h + 1) * Q_BLOCK]
    hi, lo = _split_bf16(psum)
    imp_t = (jnp.dot(impt_ref[...], hi, preferred_element_type=f32)
             + jnp.dot(impt_ref[...], lo, preferred_element_type=f32))
    blk = lax.broadcasted_iota(jnp.int32, (NB_PROMPT, Q_BLOCK), 0)
    qp = q0 + lax.broadcasted_iota(jnp.int32, (NB_PROMPT, Q_BLOCK), 1)
    qb = qp >> 6
    forced = (blk == 0) | (blk == qb) | (blk == qb - 1)
    causal = blk * SEL_BLOCK <= qp
    score = jnp.where(causal, jnp.where(forced, FORCE_SCORE, imp_t), MASK_SCORE)
    sel = _topk_rounds(score, blk, 0) & causal
    bias = jnp.where(sel, 0.0, MASK_SCORE).T.astype(bf16)
    qe = jnp.concatenate([jnp.concatenate([bias] * HPG, axis=0), q], axis=1)

    n_wk = WINDOW + Q_BLOCK
    w0 = pl.multiple_of(jnp.maximum(q0 - WINDOW, 0), Q_BLOCK)
    dist = q_pos(n_wk) - (w0 + lax.broadcasted_iota(jnp.int32, (n_wk, rows), 0))
    ok = (dist >= 0) & (dist < WINDOW)
    e, l_w = softmax0(jnp.where(ok, _nt(kw_ref[0, 0, pl.ds(w0, n_wk), :], q), NEG_INF), ok)
    o_w = (jnp.dot(vwt_ref[0, 0, :, pl.ds(w0, n_wk)], e.astype(bf16), preferred_element_type=f32) / l_w).T

    def sel_scores(off):
        return _nt(ke_ref[0, 0, pl.ds(off, SEL_KT), :], qe)

    def pv(j, p):
        off = pl.multiple_of(jnp.maximum(j, 0) * SEL_KT, SEL_KT)
        return jnp.dot(vst_ref[0, 0, :, pl.ds(off, SEL_KT)], p, preferred_element_type=f32)

    def softmax_step(s, m, l):
        m_new = jnp.maximum(m, jnp.max(s, axis=0, keepdims=True))
        alpha = jnp.exp(m - m_new)
        p = jnp.exp(s - m_new)
        return m_new, alpha * l + jnp.sum(p, axis=0, keepdims=True), alpha, p.astype(bf16)

    n_full = q0 // SEL_KT
    s_scr[...] = sel_scores(0)
    p_scr[...] = jnp.zeros((SEL_KT, rows), bf16)

    def sel_body(j, carry):
        m, l, acc, alpha_prev = carry
        s_next = sel_scores(pl.multiple_of((j + 1) * SEL_KT, SEL_KT))
        acc = alpha_prev * acc + pv(j - 1, p_scr[...])
        m, l, alpha, p = softmax_step(s_scr[...], m, l)
        p_scr[...] = p
        s_scr[...] = s_next
        return m, l, acc, alpha

    init = (jnp.full((1, rows), NEG_INF, f32), jnp.zeros((1, rows), f32), jnp.zeros((DH, rows), f32),
            jnp.ones((1, rows), f32))
    m, l, acc, alpha_prev = lax.fori_loop(0, n_full, sel_body, init)
    acc = alpha_prev * acc + pv(n_full - 1, p_scr[...])
    off = pl.multiple_of(n_full * SEL_KT, SEL_KT)
    key_pos = off + lax.broadcasted_iota(jnp.int32, (SEL_KT, rows), 0)
    s = jnp.where(key_pos <= q_pos(SEL_KT), s_scr[...], 2.0 * MASK_SCORE)
    _, l_s, alpha, p = softmax_step(s, m, l)
    acc = alpha * acc + jnp.dot(vst_ref[0, 0, :, pl.ds(off, SEL_KT)], p, preferred_element_type=f32)
    o_s = (acc / jnp.maximum(l_s, 1e-30)).T

    gt = gt_ref[0, 0]
    gate = lambda c: jnp.concatenate([gt[:, 3 * h + c:3 * h + c + 1] for h in range(HPG)], axis=0)
    o = gate(0) * o_c + gate(1) * o_s + gate(2) * o_w
    for h in range(HPG):
        o_ref[0, :, h * DH:(h + 1) * DH] = o[h * Q_BLOCK:(h + 1) * Q_BLOCK]


def nsa_prompt_attention(q, gt, ck, cvt, ke, vst, kw, vwt, n_cmp):
    bsz, t_len, _ = q.shape
    n_q = t_len // Q_BLOCK
    assert t_len % SEL_KT == 0 and t_len >= WINDOW + Q_BLOCK
    impt = _imp_map_t(NC_PAD, NB_PROMPT, n_cmp, NB_PROMPT)
    tile = pl.BlockSpec((1, Q_BLOCK, HPG * DH), lambda b, g, i: (b, i, g))
    per_bg = lambda r, w: pl.BlockSpec((1, 1, r, w), lambda b, g, i: (b, g, 0, 0))
    return pl.pallas_call(
        functools.partial(_prompt_attn_kernel, n_cmp=n_cmp),
        grid=(bsz, G, n_q),
        in_specs=[tile, pl.BlockSpec((1, 1, Q_BLOCK, 3 * HPG), lambda b, g, i: (b, g, i, 0)),
                  per_bg(NC_PAD, DH), per_bg(DH, NC_PAD),
                  pl.BlockSpec((NB_PROMPT, NC_PAD), lambda b, g, i: (0, 0)),
                  per_bg(t_len, KEXT), per_bg(DH, t_len), per_bg(t_len, DH), per_bg(DH, t_len)],
        out_specs=tile,
        out_shape=jax.ShapeDtypeStruct((bsz, t_len, G * HPG * DH), f32),
        scratch_shapes=[pltpu.VMEM((SEL_KT, QROWS), f32), pltpu.VMEM((SEL_KT, QROWS), bf16)],
        compiler_params=pltpu.CompilerParams(dimension_semantics=("parallel", "parallel", "arbitrary"),
                                             vmem_limit_bytes=VMEM_LIMIT),
        name="nsa_prompt_attention",
    )(q, gt, ck, cvt, impt, ke, vst, kw, vwt)


N_SUB = PAST_LEN // CMP_STRIDE
HALF = 2 * G * DH


def _gelu(x):
    return 0.5 * x * (1.0 + jnp.tanh(0.7978845608028654 * (x + 0.044715 * (x * x * x))))


def _sample_cmp_kernel(pt_ref, *refs):
    pages = refs[:N_PAGES]
    pe_ref, w1_ref, b1_ref, w2_ref, b2_ref, kg_ref, ck_ref, cv_ref, x_scr = refs[N_PAGES:]
    n_chunk = HALF // LANES
    for c in range(n_chunk):
        for j in range(N_PAGES):
            x_scr[c, pl.ds(j * PAGE_SIZE, PAGE_SIZE), :] = pages[j][0, :, pl.ds(c * LANES, LANES)].astype(f32)
        x_scr[c, pl.ds(PAST_LEN, CMP_STRIDE), :] = jnp.zeros((CMP_STRIDE, LANES), f32)
    low = lax.broadcasted_iota(jnp.int32, ((G // 2) * N_SUB, LANES), 1) < DH
    for part in range(2):
        acc = jnp.zeros(((G // 2) * N_SUB, 2 * CMP_HID), f32)
        for s in range(CMP_LEN):
            x = jnp.concatenate([x_scr[part * (G // 2) + c, pl.ds(s, N_SUB, stride=CMP_STRIDE), :]
                                 for c in range(G // 2)], axis=0)
            x = x + pe_ref[part, s:s + 1, :]
            acc = acc + jnp.dot(x.astype(bf16), w1_ref[part, s], preferred_element_type=f32)
        hid = _gelu(acc + b1_ref[part:part + 1, :])
        out = jnp.dot(hid.astype(bf16), w2_ref[part], preferred_element_type=f32) + b2_ref[part:part + 1, :]
        if part == 0:
            sq = out * out
            ms = jnp.where(low, jnp.sum(jnp.where(low, sq, 0.0), axis=-1, keepdims=True),
                           jnp.sum(jnp.where(low, 0.0, sq), axis=-1, keepdims=True)) * (1.0 / DH)
            out = out * lax.rsqrt(ms + NORM_EPS) * kg_ref[...]
        out_ref = ck_ref if part == 0 else cv_ref
        for c in range(G // 2):
            out_ref[0, c] = out[c * N_SUB:(c + 1) * N_SUB].astype(bf16)


def sample_compress(cache3, page_table, pe, w1, b1, w2, b2, kg):
    bsz = page_table.shape[0]
    twice = lambda a: jnp.concatenate([a, a], axis=-1)
    w1 = w1.reshape(2, CMP_LEN, DH, CMP_HID)
    z1, z2 = jnp.zeros_like(w1), jnp.zeros_like(w2)
    w1d = jnp.concatenate([jnp.concatenate([w1, z1], -1), jnp.concatenate([z1, w1], -1)], axis=-2).astype(bf16)
    w2d = jnp.concatenate([jnp.concatenate([w2, z2], -1), jnp.concatenate([z2, w2], -1)], axis=-2).astype(bf16)
    pe, b1, b2, kg = twice(pe), twice(b1), twice(b2), twice(kg)

    def page_spec(j):
        return pl.BlockSpec((1, PAGE_SIZE, HALF), lambda b, pt: (pt[b, j], 0, 0))

    const = lambda a: pl.BlockSpec(a.shape, lambda b, pt: (0,) * a.ndim)
    out_spec = pl.BlockSpec((1, G // 2, N_SUB, LANES), lambda b, pt: (b, 0, 0, 0))
    grid_spec = pltpu.PrefetchScalarGridSpec(
        num_scalar_prefetch=1, grid=(bsz,),
        in_specs=[page_spec(j) for j in range(N_PAGES)] + [const(a) for a in (pe, w1d, b1, w2d, b2, kg)],
        out_specs=[out_spec, out_spec],
        scratch_shapes=[pltpu.VMEM((HALF // LANES, PAST_LEN + CMP_STRIDE, LANES), f32)])
    return pl.pallas_call(
        _sample_cmp_kernel, grid_spec=grid_spec,
        out_shape=[jax.ShapeDtypeStruct((bsz, G // 2, N_SUB, LANES), bf16)] * 2,
        compiler_params=pltpu.CompilerParams(dimension_semantics=("arbitrary",), vmem_limit_bytes=VMEM_LIMIT),
        name="sample_compress",
    )(page_table, *([cache3] * N_PAGES), pe, w1d, b1, w2d, b2, kg)


T_NEW = 4
NEW_PAD = 128
NB_SAMPLE = -(-(PAST_LEN + T_NEW) // SEL_BLOCK)
N_CMP_SAMPLE = (PAST_LEN + T_NEW) // CMP_STRIDE - 1
ROWS_S = HPG * T_NEW


def _sample_attn_kernel(pt_ref, *refs):
    pages = refs[:N_PAGES]
    (q_ref, gt_ref, ck_ref, cv_ref, new_ref, win_ref, wnew_ref, impt_ref, oh_ref, ohn_ref, o_ref) = refs[N_PAGES:]
    rows = 2 * ROWS_S
    t_of_row = lax.broadcasted_iota(jnp.int32, (rows, LANES), 0) & (T_NEW - 1)
    lane = lax.broadcasted_iota(jnp.int32, (rows, LANES), 1)
    q_pos = PAST_LEN + t_of_row
    hs_o = lax.broadcasted_iota(jnp.int32, (SUBLANES, rows), 0)
    hs_r = lax.broadcasted_iota(jnp.int32, (SUBLANES, rows), 1)
    hsum = ((hs_o >> 2 == hs_r >> 4) & ((hs_o & 3) == (hs_r & 3))).astype(bf16)
    sp_r = lax.broadcasted_iota(jnp.int32, (rows, SUBLANES), 0)
    sp_o = lax.broadcasted_iota(jnp.int32, (rows, SUBLANES), 1)
    spread = ((sp_o >> 2 == sp_r >> 4) & ((sp_o & 3) == (sp_r & 3))).astype(bf16)
    blk8 = lax.broadcasted_iota(jnp.int32, (SUBLANES, LANES), 1)
    qp8 = PAST_LEN + (lax.broadcasted_iota(jnp.int32, (SUBLANES, LANES), 0) & (T_NEW - 1))
    qb8 = qp8 >> 6
    forced = (blk8 == 0) | (blk8 == qb8) | (blk8 == qb8 - 1)
    causal = (blk8 * SEL_BLOCK <= qp8) & (blk8 < NB_SAMPLE)
    n_buf = win_ref.shape[1]

    for c in range(G // 2):
        lanes_k = pl.ds(c * LANES, LANES)
        lanes_v = pl.ds(G * DH + c * LANES, LANES)
        q = q_ref[0, c]
        gt = gt_ref[0, c]
        s_c = _nt(q, ck_ref[0, c])
        ok_c = (lane * CMP_STRIDE + (CMP_LEN - 1) <= q_pos) & (lane < N_CMP_SAMPLE)
        p_c = _masked_softmax(s_c, ok_c)
        o_c = jnp.dot(p_c.astype(bf16), cv_ref[0, c], preferred_element_type=f32)
        hi, lo = _split_bf16(p_c)
        psum = jnp.dot(hsum, hi, preferred_element_type=f32) + jnp.dot(hsum, lo, preferred_element_type=f32)
        hi, lo = _split_bf16(psum)
        imp = _nt(hi, impt_ref[...]) + _nt(lo, impt_ref[...])
        score = jnp.where(blk8 < NB_SAMPLE,
                          jnp.where(causal, jnp.where(forced, FORCE_SCORE, imp), MASK_SCORE), PICKED)
        sel = _topk_by_rank(score, blk8, NB_SAMPLE) & causal
        bias8 = jnp.where(sel, 0.0, MASK_SCORE).astype(bf16)
        bias = jnp.dot(spread, bias8, preferred_element_type=f32).astype(bf16)
        k_all = jnp.concatenate([pages[j][0, :, lanes_k] for j in range(N_PAGES)], axis=0).astype(bf16)
        s_past = _nt(q, k_all) + jnp.dot(bias, oh_ref[...], preferred_element_type=f32)
        s_new = _nt(q, new_ref[0, :, lanes_k].astype(bf16)) + jnp.dot(bias, ohn_ref[...],
                                                                      preferred_element_type=f32)
        s_new = jnp.where(lane <= t_of_row, s_new, 2.0 * MASK_SCORE)
        s_all = jnp.concatenate([s_past, s_new], axis=1)
        e = jnp.exp(s_all - jnp.max(s_all, axis=1, keepdims=True))
        den = jnp.maximum(jnp.sum(e, axis=1, keepdims=True), 1e-30)
        v_all = jnp.concatenate([pages[j][0, :, lanes_v] for j in range(N_PAGES)] + [new_ref[0, :, lanes_v]],
                                axis=0).astype(bf16)
        o_s = jnp.dot(e.astype(bf16), v_all, preferred_element_type=f32) / den
        kw_all = jnp.concatenate([win_ref[0, :, lanes_k], wnew_ref[0, :, lanes_k]], axis=0).astype(bf16)
        idx = lax.broadcasted_iota(jnp.int32, (rows, n_buf), 1)
        t_w = lax.broadcasted_iota(jnp.int32, (rows, n_buf), 0) & (T_NEW - 1)
        w_pos = PAST_LEN - n_buf + idx
        dist = PAST_LEN + t_w - w_pos
        ok_w = jnp.concatenate([(dist >= 0) & (dist < WINDOW) & (w_pos >= 0), lane <= t_of_row], axis=1)
        p_w = _masked_softmax(_nt(q, kw_all), ok_w).astype(bf16)
        vw_all = jnp.concatenate([win_ref[0, :, lanes_v], wnew_ref[0, :, lanes_v]], axis=0).astype(bf16)
        o_w = jnp.dot(p_w, vw_all, preferred_element_type=f32)
        o_ref[0, c] = gt[:, 0:1] * o_c + gt[:, 1:2] * o_s + gt[:, 2:3] * o_w


def sample_attention(cache3, page_table, q2, gt2, ck2, cv2, new_sel, win_cache, new_win):
    bsz = page_table.shape[0]
    n_buf = win_cache.shape[1]
    rows = 2 * ROWS_S
    impt = _imp_map_t(N_SUB, LANES, N_CMP_SAMPLE, NB_SAMPLE)
    key_blk = np.arange(PAST_LEN) // SEL_BLOCK
    onehot = jnp.asarray(key_blk[None, :] == np.arange(LANES)[:, None], bf16)
    onehot_new = jnp.asarray(np.broadcast_to(np.arange(LANES)[:, None] == PAST_LEN // SEL_BLOCK,
                                             (LANES, NEW_PAD)), bf16)

    def page_spec(j):
        return pl.BlockSpec((1, PAGE_SIZE, HALF), lambda b, pt: (pt[b, j], 0, 1))

    per_seq = lambda shape: pl.BlockSpec((1,) + shape, lambda b, pt: (b,) + (0,) * len(shape))
    const = lambda shape: pl.BlockSpec(shape, lambda b, pt: (0,) * len(shape))
    grid_spec = pltpu.PrefetchScalarGridSpec(
        num_scalar_prefetch=1, grid=(bsz,),
        in_specs=[page_spec(j) for j in range(N_PAGES)] + [
            per_seq((G // 2, rows, LANES)), per_seq((G // 2, rows, 3)),
            per_seq((G // 2, N_SUB, LANES)), per_seq((G // 2, N_SUB, LANES)),
            per_seq((NEW_PAD, HALF)), per_seq((n_buf, HALF)), per_seq((NEW_PAD, HALF)),
            const((LANES, N_SUB)), const((LANES, PAST_LEN)), const((LANES, NEW_PAD))],
        out_specs=per_seq((G // 2, rows, LANES)))
    return pl.pallas_call(
        _sample_attn_kernel, grid_spec=grid_spec,
        out_shape=jax.ShapeDtypeStruct((bsz, G // 2, rows, LANES), f32),
        compiler_params=pltpu.CompilerParams(dimension_semantics=("arbitrary",), vmem_limit_bytes=VMEM_LIMIT),
        name="sample_attention",
    )(page_table, *([cache3] * N_PAGES), q2, gt2, ck2, cv2, new_sel, win_cache, new_win, impt, onehot, onehot_new)


def rmsnorm(x, g):
    y = x * lax.rsqrt(jnp.mean(x * x, -1, keepdims=True) + NORM_EPS)
    return y * g


def _ffn_in_kernel(x_ref, g_ref, w_ref, o_ref):
    x = x_ref[...]
    xn = x * lax.rsqrt(jnp.mean(x * x, axis=-1, keepdims=True) + NORM_EPS) * g_ref[...]
    o_ref[...] = jnp.dot(xn.astype(bf16), w_ref[...], preferred_element_type=f32)


def _ffn_out_kernel(x_ref, h_ref, halo_ref, cs_ref, cw_ref, cb_ref, w_ref, o_ref, u_scr, *, tiles_per_seq):
    first = pl.program_id(0) % tiles_per_seq == 0
    tm = h_ref.shape[0]
    u_scr[pl.ds(0, SUBLANES), :] = halo_ref[...]
    u_scr[pl.ds(SUBLANES, tm), :] = h_ref[:, pl.ds(0, D_FF)]

    @pl.when(first)
    def _():
        u_scr[pl.ds(SUBLANES - (CONV_W - 1), CONV_W - 1), :] = cs_ref[0]

    uc = cb_ref[...] + cw_ref[CONV_W - 1:CONV_W, :] * u_scr[pl.ds(SUBLANES, tm), :]
    for j in range(CONV_W - 1):
        uc = uc + cw_ref[j:j + 1, :] * u_scr[pl.ds(SUBLANES - (CONV_W - 1) + j, tm), :]
    act = _gelu(uc) * h_ref[:, pl.ds(D_FF, D_FF)]
    o_ref[...] = x_ref[...] + jnp.dot(act.astype(bf16), w_ref[...], preferred_element_type=f32)


FFN_TM = 1024
FFN_TN = 1408
FFN_OUT_TM = 256


def conv_ffn_fused(x, g, conv_state, w_in, conv_w, conv_b, w_out):
    bsz, t_len, _ = x.shape
    m = bsz * t_len
    tiles_per_seq = t_len // FFN_OUT_TM
    x2 = x.reshape(m, D_MODEL)
    h = pl.pallas_call(
        _ffn_in_kernel,
        grid=(m // FFN_TM, 2 * D_FF // FFN_TN),
        in_specs=[pl.BlockSpec((FFN_TM, D_MODEL), lambda i, j: (i, 0)),
                  pl.BlockSpec((1, D_MODEL), lambda i, j: (0, 0)),
                  pl.BlockSpec((D_MODEL, FFN_TN), lambda i, j: (0, j))],
        out_specs=pl.BlockSpec((FFN_TM, FFN_TN), lambda i, j: (i, j)),
        out_shape=jax.ShapeDtypeStruct((m, 2 * D_FF), f32),
        compiler_params=pltpu.CompilerParams(dimension_semantics=("parallel", "parallel"),
                                             vmem_limit_bytes=VMEM_LIMIT),
        name="ffn_in",
    )(x2, g.reshape(1, D_MODEL), w_in.astype(bf16))
    halo_blocks = FFN_OUT_TM // SUBLANES
    y = pl.pallas_call(
        functools.partial(_ffn_out_kernel, tiles_per_seq=tiles_per_seq),
        grid=(m // FFN_OUT_TM,),
        in_specs=[pl.BlockSpec((FFN_OUT_TM, D_MODEL), lambda i: (i, 0)),
                  pl.BlockSpec((FFN_OUT_TM, 2 * D_FF), lambda i: (i, 0)),
                  pl.BlockSpec((SUBLANES, D_FF), lambda i: (jnp.maximum(i * halo_blocks - 1, 0), 0)),
                  pl.BlockSpec((1, CONV_W - 1, D_FF), lambda i: (i // tiles_per_seq, 0, 0)),
                  pl.BlockSpec((CONV_W, D_FF), lambda i: (0, 0)),
                  pl.BlockSpec((1, D_FF), lambda i: (0, 0)),
                  pl.BlockSpec((D_FF, D_MODEL), lambda i: (0, 0))],
        out_specs=pl.BlockSpec((FFN_OUT_TM, D_MODEL), lambda i: (i, 0)),
        out_shape=jax.ShapeDtypeStruct((m, D_MODEL), f32),
        scratch_shapes=[pltpu.VMEM((SUBLANES + FFN_OUT_TM, D_FF), f32)],
        compiler_params=pltpu.CompilerParams(dimension_semantics=("arbitrary",), vmem_limit_bytes=56 * 1024 * 1024),
        name="ffn_out",
    )(x2, h, h, conv_state, conv_w, conv_b.reshape(1, D_FF), w_out.astype(bf16))
    conv_new = h.reshape(bsz, t_len, 2 * D_FF)[:, t_len - (CONV_W - 1):, :D_FF]
    return y.reshape(bsz, t_len, D_MODEL), conv_new


def conv_ffn(xn, conv_state, w_in, conv_w, conv_b, w_out):
    t_len = xn.shape[1]
    u, gate = jnp.split(mm3(xn, w_in), 2, axis=-1)
    u_pad = jnp.concatenate([conv_state, u], axis=1)
    uc = conv_b + conv_w[0] * u_pad[:, 0:t_len]
    for j in range(1, CONV_W):
        uc = uc + conv_w[j] * u_pad[:, j:j + t_len]
    return mm3(jax.nn.gelu(uc) * gate, w_out), u_pad[:, t_len:]


RW_TM = 256
SEG_W = LANES


def _head_seg():
    seg = np.arange(D_MODEL)[:, None] // RW_HEAD_DIM == np.arange(SEG_W)[None, :]
    return jnp.asarray(seg, bf16), jnp.asarray(seg.T, bf16)


def _head_sum(x, seg_ref):
    hi, lo = _split_bf16(x)
    return jnp.dot(hi, seg_ref[...], preferred_element_type=f32) + jnp.dot(lo, seg_ref[...],
                                                                           preferred_element_type=f32)


def _head_expand(s, segt_ref):
    hi, lo = _split_bf16(s)
    return jnp.dot(hi, segt_ref[...], preferred_element_type=f32) + jnp.dot(lo, segt_ref[...],
                                                                            preferred_element_type=f32)


def _rwkv_pre_kernel(*refs, has_vfirst):
    it = iter(refs)
    xn_ref, xs_ref = next(it), next(it)
    vf_ref = next(it) if has_vfirst else None
    mu_ref, wr, wk, wv, w1, w2, a1, a2 = (next(it) for _ in range(8))
    v1, v2 = (next(it), next(it)) if has_vfirst else (None, None)
    g1, g2, w0, a0 = (next(it) for _ in range(4))
    v0 = next(it) if has_vfirst else None
    kk_ref, ka_ref, seg_ref, segt_ref = (next(it) for _ in range(4))
    r_o, w_o, k_o, v_o, nkk_o, kka_o, g_o = (next(it) for _ in range(7))

    dot = lambda x, w: jnp.dot(x.astype(bf16), w[...], preferred_element_type=f32)
    xn = xn_ref[...]
    xx = xs_ref[...] - xn
    xr, xw, xk, xv, xa, xg = (xn + xx * mu_ref[i:i + 1, :] for i in range(6))
    r = dot(xr, wr)
    k = dot(xk, wk)
    v = dot(xv, wv)
    z = -(w0[...] + dot(jnp.tanh(dot(xw, w1)), w2))
    softplus = jnp.maximum(z, 0.0) + jnp.log1p(jnp.exp(-jnp.abs(z)))
    decay = jnp.exp(-jnp.exp(-softplus - 0.5))
    a = jax.nn.sigmoid(a0[...] + dot(dot(xa, a1), a2))
    if has_vfirst:
        v = v + (vf_ref[...] - v) * jax.nn.sigmoid(v0[...] + dot(dot(xv, v1), v2))
    g = dot(jax.nn.sigmoid(dot(xg, g1)), g2)
    kk = k * kk_ref[...]
    norm = jnp.maximum(jnp.sqrt(_head_sum(kk * kk, seg_ref)), 1e-12)
    kk = kk * _head_expand(1.0 / norm, segt_ref)
    r_o[...] = r
    w_o[...] = decay
    k_o[...] = k * (1.0 + (a - 1.0) * ka_ref[...])
    v_o[...] = v
    nkk_o[...] = -kk
    kka_o[...] = kk * a
    g_o[...] = g


def rwkv_pre(xn, xs, v_first, li, p):
    m = xn.shape[0]
    tm = RW_TM if m % RW_TM == 0 else m
    has_vfirst = v_first is not None
    seg, segt = _head_seg()
    row = lambda a: a.reshape(1, -1)
    cast = lambda w: w.astype(bf16)
    ops = [xn, xs] + ([v_first] if has_vfirst else [])
    ops += [p['rw_mu'][li], cast(p['rw_w_r'][li]), cast(p['rw_w_k'][li]), cast(p['rw_w_v'][li]),
            cast(p['rw_w1'][li]), cast(p['rw_w2'][li]), cast(p['rw_a1'][li]), cast(p['rw_a2'][li])]
    if has_vfirst:
        ops += [cast(p['rw_v1'][li - 1]), cast(p['rw_v2'][li - 1])]
    ops += [cast(p['rw_g1'][li]), cast(p['rw_g2'][li]), row(p['rw_w0'][li]), row(p['rw_a0'][li])]
    if has_vfirst:
        ops += [row(p['rw_v0'][li - 1])]
    ops += [row(p['rw_k_k'][li]), row(p['rw_k_a'][li]), seg, segt]
    n_rows = 3 if has_vfirst else 2
    tile = pl.BlockSpec((tm, D_MODEL), lambda i: (i, 0))
    whole = lambda a: pl.BlockSpec(a.shape, lambda i: (0,) * a.ndim)
    return pl.pallas_call(
        functools.partial(_rwkv_pre_kernel, has_vfirst=has_vfirst),
        grid=(m // tm,),
        in_specs=[tile] * n_rows + [whole(a) for a in ops[n_rows:]],
        out_specs=[tile] * 7,
        out_shape=[jax.ShapeDtypeStruct((m, D_MODEL), f32)] * 7,
        compiler_params=pltpu.CompilerParams(dimension_semantics=("parallel",), vmem_limit_bytes=VMEM_LIMIT),
        name="rwkv_pre",
    )(*ops)


def _rwkv_post_kernel(y_ref, r_ref, k_ref, v_ref, g_ref, x_ref, lnw, lnb, rk, wo, seg_ref, segt_ref, o_ref):
    y = y_ref[...]
    inv_n = 1.0 / RW_HEAD_DIM
    yc = y - _head_expand(_head_sum(y, seg_ref) * inv_n, segt_ref)
    var = _head_expand(_head_sum(yc * yc, seg_ref) * inv_n, segt_ref)
    yn = yc * lax.rsqrt(var + RW_GN_EPS) * lnw[...] + lnb[...]
    bonus = _head_expand(_head_sum(r_ref[...] * k_ref[...] * rk[...], seg_ref), segt_ref) * v_ref[...]
    out = ((yn + bonus) * g_ref[...]).astype(bf16)
    o_ref[...] = x_ref[...] + jnp.dot(out, wo[...], preferred_element_type=f32)


def rwkv_post(y, r, k, v, g, x, li, p):
    m = y.shape[0]
    tm = RW_TM if m % RW_TM == 0 else m
    seg, segt = _head_seg()
    row = lambda a: a.reshape(1, -1)
    consts = [row(p['rw_ln_w'][li]), row(p['rw_ln_b'][li]), row(p['rw_r_k'][li]), p['rw_w_o'][li].astype(bf16),
              seg, segt]
    tile = pl.BlockSpec((tm, D_MODEL), lambda i: (i, 0))
    whole = lambda a: pl.BlockSpec(a.shape, lambda i: (0,) * a.ndim)
    return pl.pallas_call(
        _rwkv_post_kernel,
        grid=(m // tm,),
        in_specs=[tile] * 6 + [whole(a) for a in consts],
        out_specs=tile,
        out_shape=jax.ShapeDtypeStruct((m, D_MODEL), f32),
        compiler_params=pltpu.CompilerParams(dimension_semantics=("parallel",), vmem_limit_bytes=VMEM_LIMIT),
        name="rwkv_post",
    )(y, r, k, v, g, x, *consts)


def rwkv7_layer(x, shift_prev, s0, v_first, li, p, scan_cfg):
    bsz, t_len, _ = x.shape
    m = bsz * t_len
    xn = rmsnorm(x, p['norm_g'][li, 0])
    xs = jnp.concatenate([shift_prev[:, None], xn[:, :-1]], axis=1)
    flat = lambda a: a.reshape(m, D_MODEL)
    r, decay, k, v, nkk, kka, g = rwkv_pre(flat(xn), flat(xs), None if li == 0 else flat(v_first), li, p)
    if li == 0:
        v_first = v
    seq = lambda a: a.reshape(bsz, t_len, D_MODEL)
    y, s_new = wkv_scan(seq(r), seq(decay), seq(k), seq(v), seq(nkk), seq(kka), to_pairs(s0), **scan_cfg)
    x_new = rwkv_post(flat(y), r, k, v, g, flat(x), li, p)
    return seq(x_new), from_pairs(s_new, bsz), xn[:, -1], seq(v_first)


def kv_rows(h, p):
    bsz, t_len, _ = h.shape
    kv = mm3(rmsnorm(h, p['kv_norm_g']), p['w_kv']).reshape(bsz, t_len, 6, G, DH)
    k_sel = rmsnorm(kv[:, :, 2], p['k_norm_g'][1])
    k_win = rmsnorm(kv[:, :, 4], p['k_norm_g'][2])
    paged = jnp.stack([kv[:, :, 0], kv[:, :, 1], k_sel, kv[:, :, 3]], axis=2)
    win = jnp.stack([k_win, kv[:, :, 5]], axis=2)
    return paged, win


def nsa_query(xn, lb, p):
    bsz, t_len, _ = xn.shape
    n_q = NSA_HEADS * DH
    w = jnp.pad(p['nsa_w_q'][lb], ((0, 0), (0, LANES - 3 * NSA_HEADS)))
    proj = mm3(xn, w)
    q = proj[..., :n_q].reshape(bsz, t_len, G, HPG, DH)
    q = rmsnorm(q, p['q_norm_g'][lb]) * (DH ** -0.5)
    gates = jax.nn.sigmoid(proj[..., n_q:n_q + 3 * NSA_HEADS]).reshape(bsz, t_len, G, HPG, 3)
    return q, gates


def compress_rows_prompt(rows, pe, w1, b1, w2, b2):
    bsz, t_len = rows.shape[:2]
    n_sub = t_len // CMP_STRIDE
    sub = rows.reshape(bsz, n_sub, CMP_STRIDE, G, DH).transpose(0, 1, 3, 2, 4)
    half = CMP_STRIDE * DH
    x0 = (sub + pe[:CMP_STRIDE][None, None, None]).reshape(bsz * n_sub * G, half)
    x1 = (sub + pe[CMP_STRIDE:][None, None, None]).reshape(bsz * n_sub * G, half)
    p0 = mm(x0, w1[:half]).reshape(bsz, n_sub, G, CMP_HID)
    p1 = mm(x1, w1[half:]).reshape(bsz, n_sub, G, CMP_HID)
    hid = jax.nn.gelu(p0[:, :-1] + p1[:, 1:] + b1)
    out = mm(hid.reshape(bsz * (n_sub - 1) * G, CMP_HID), w2) + b2
    return out.reshape(bsz, n_sub - 1, G, DH)


def prompt_context(h, p):
    paged, win = kv_rows(h, p)
    bsz, t_len = h.shape[:2]
    ck = rmsnorm(compress_rows_prompt(paged[:, :, 0], p['cmp_pe'][0], p['cmp_w1'][0], p['cmp_b1'][0],
                                      p['cmp_w2'][0], p['cmp_b2'][0]), p['k_norm_g'][0])
    cv = compress_rows_prompt(paged[:, :, 1], p['cmp_pe'][1], p['cmp_w1'][1], p['cmp_b1'][1],
                              p['cmp_w2'][1], p['cmp_b2'][1])
    n_cmp = ck.shape[1]
    pad = lambda c: jnp.pad(c, ((0, 0), (0, NC_PAD - n_cmp), (0, 0), (0, 0)))
    heads = lambda x: x.transpose(0, 2, 1, 3).astype(bf16)
    heads_t = lambda x: x.transpose(0, 2, 3, 1).astype(bf16)
    onehot = jnp.asarray(np.arange(t_len)[:, None] // SEL_BLOCK == np.arange(NB_PROMPT)[None, :], bf16)
    ke = jnp.concatenate([jnp.broadcast_to(onehot, (bsz, G, t_len, NB_PROMPT)), heads(paged[:, :, 2])], axis=-1)
    ctx = dict(ck=heads(pad(ck)), cvt=heads_t(pad(cv)), n_cmp=n_cmp, ke=ke, vst=heads_t(paged[:, :, 3]),
               kw=heads(win[:, :, 0]), vwt=heads_t(win[:, :, 1]))
    keep = min(WINDOW, t_len)
    return ctx, (paged, win[:, t_len - keep:])


def attend_prompt(q, gates, ctx):
    bsz, t_len = q.shape[:2]
    gt = gates.reshape(bsz, t_len, G, 3 * HPG).transpose(0, 2, 1, 3)
    return nsa_prompt_attention(q.reshape(bsz, t_len, NSA_HEADS * DH).astype(bf16), gt, ctx['ck'], ctx['cvt'],
                                ctx['ke'], ctx['vst'], ctx['kw'], ctx['vwt'], ctx['n_cmp'])


def sample_context(h, cache3, page_table, cache_kv_win, p):
    paged, win = kv_rows(h, p)
    bsz, t_new = h.shape[:2]
    ck, cv = sample_compress(cache3, page_table, p['cmp_pe'], p['cmp_w1'], p['cmp_b1'], p['cmp_w2'], p['cmp_b2'],
                             p['k_norm_g'][0:1])
    n_buf = cache_kv_win.shape[1]
    pad_rows = lambda x: jnp.pad(x.reshape(bsz, t_new, HALF), ((0, 0), (0, NEW_PAD - t_new), (0, 0)))
    ctx = dict(ck=ck, cv=cv, new_sel=pad_rows(paged[:, :, 2:4]), new_win=pad_rows(win),
               win_cache=cache_kv_win.reshape(bsz, n_buf, HALF))
    win_full = jnp.concatenate([cache_kv_win, win], axis=1)
    return ctx, (paged, win_full[:, win_full.shape[1] - n_buf:])


def attend_sample(q, gates, ctx, cache3, page_table):
    bsz, t_new = q.shape[:2]
    rows = lambda x: x.transpose(0, 2, 3, 1, 4).reshape(bsz, G // 2, 2 * HPG * t_new, x.shape[-1])
    qr = rows(q)
    own = (jnp.arange(2 * HPG * t_new) // (HPG * t_new))[:, None] == (jnp.arange(LANES) // DH)[None, :]
    q2 = jnp.where(own, jnp.concatenate([qr, qr], axis=-1), 0.0).astype(bf16)
    o = sample_attention(cache3, page_table, q2, rows(gates), ctx['ck'], ctx['cv'],
                         ctx['new_sel'], ctx['win_cache'], ctx['new_win'])
    o = jnp.where(own, o, 0.0)
    o = o[..., :DH] + o[..., DH:]
    return o.reshape(bsz, G, HPG, t_new, DH).transpose(0, 3, 1, 2, 4).reshape(bsz, t_new, NSA_HEADS * DH)


def forward(x, wkv_in, shift_in, conv_in, make_ctx, attend, p, scan_cfg):
    wkv_out, shift_out, conv_out = [], [], []
    v_first = None
    ctx = None
    kv_state = None
    for layer in range(DEPTH):
        if layer == N_A_LAYERS:
            ctx, kv_state = make_ctx(x)
        if layer < N_A_LAYERS:
            x, s_new, sh_new, v_first = rwkv7_layer(x, shift_in[layer], wkv_in[layer], v_first, layer, p, scan_cfg)
            wkv_out.append(s_new)
            shift_out.append(sh_new)
        else:
            lb = layer - N_A_LAYERS
            q, gates = nsa_query(rmsnorm(x, p['norm_g'][layer, 0]), lb, p)
            x = x + mm3(attend(q, gates, ctx), p['nsa_w_o'][lb])
        ffn_w = (p['ffn_w_in'][layer], p['ffn_conv_w'][layer], p['ffn_conv_b'][layer], p['ffn_w_out'][layer])
        if x.shape[1] % FFN_TM == 0:
            x, conv_new = conv_ffn_fused(x, p['norm_g'][layer, 1], conv_in[layer], *ffn_w)
        else:
            f, conv_new = conv_ffn(rmsnorm(x, p['norm_g'][layer, 1]), conv_in[layer], *ffn_w)
            x = x + f
        conv_out.append(conv_new)
    return x, kv_state, jnp.stack(wkv_out), jnp.stack(shift_out), jnp.stack(conv_out)


def kernel(x_prompt, x_sample, cache_kv, cache_kv_win, state_wkv, state_shift, state_ffn_conv, page_table, norm_g, rw_mu, rw_w_r, rw_w_k, rw_w_v, rw_w_o, rw_w0, rw_w1, rw_w2, rw_a0, rw_a1, rw_a2, rw_v0, rw_v1, rw_v2, rw_g1, rw_g2, rw_k_k, rw_k_a, rw_r_k, rw_ln_w, rw_ln_b, kv_norm_g, w_kv, k_norm_g, cmp_pe, cmp_w1, cmp_b1, cmp_w2, cmp_b2, nsa_w_q, q_norm_g, nsa_w_o, ffn_w_in, ffn_conv_w, ffn_conv_b, ffn_w_out):
    p = dict(norm_g=norm_g, rw_mu=rw_mu, rw_w_r=rw_w_r, rw_w_k=rw_w_k, rw_w_v=rw_w_v, rw_w_o=rw_w_o,
             rw_w0=rw_w0, rw_w1=rw_w1, rw_w2=rw_w2, rw_a0=rw_a0, rw_a1=rw_a1, rw_a2=rw_a2,
             rw_v0=rw_v0, rw_v1=rw_v1, rw_v2=rw_v2, rw_g1=rw_g1, rw_g2=rw_g2, rw_k_k=rw_k_k, rw_k_a=rw_k_a,
             rw_r_k=rw_r_k, rw_ln_w=rw_ln_w, rw_ln_b=rw_ln_b, kv_norm_g=kv_norm_g, w_kv=w_kv,
             k_norm_g=k_norm_g, cmp_pe=cmp_pe, cmp_w1=cmp_w1, cmp_b1=cmp_b1, cmp_w2=cmp_w2, cmp_b2=cmp_b2,
             nsa_w_q=nsa_w_q, q_norm_g=q_norm_g, nsa_w_o=nsa_w_o, ffn_w_in=ffn_w_in,
             ffn_conv_w=ffn_conv_w, ffn_conv_b=ffn_conv_b, ffn_w_out=ffn_w_out)
    bp, t_p = x_prompt.shape[:2]
    bs = x_sample.shape[0]

    wkv0 = jnp.zeros((N_A_LAYERS, bp, RW_HEADS, RW_HEAD_DIM, RW_HEAD_DIM), f32)
    shift0 = jnp.zeros((N_A_LAYERS, bp, D_MODEL), f32)
    conv0 = jnp.zeros((DEPTH, bp, CONV_W - 1, D_FF), f32)
    y_prompt, (kv_prompt, win_prompt), wkv_prompt, shift_prompt, conv_prompt = forward(
        x_prompt, wkv0, shift0, conv0, lambda h: prompt_context(h, p), attend_prompt, p,
        dict(bb=bp, tc=128))

    cache3 = cache_kv.astype(bf16).reshape(cache_kv.shape[0], PAGE_SIZE, 4 * G * DH)
    y_sample, (kv_sample, win_sample), wkv_sample, shift_sample, conv_sample = forward(
        x_sample, state_wkv, state_shift, state_ffn_conv,
        lambda h: sample_context(h, cache3, page_table, cache_kv_win, p),
        lambda q, gates, ctx: attend_sample(q, gates, ctx, cache3, page_table), p,
        dict(bb=2, tc=x_sample.shape[1]))

    return (y_prompt, y_sample, kv_prompt, kv_sample, win_prompt, win_sample,
            wkv_prompt, wkv_sample, shift_prompt, shift_sample, conv_prompt, conv_sample)
```

```python
import functools

import numpy as np
import jax
import jax.numpy as jnp
from jax import lax
from jax.experimental import pallas as pl
from jax.experimental.pallas import tpu as pltpu

f32 = jnp.float32
bf16 = jnp.bfloat16

D_MODEL = 1024
DEPTH = 4
N_A_LAYERS = 2
PAST_LEN = 2048
PAGE_SIZE = 128
N_PAGES = PAST_LEN // PAGE_SIZE
RW_HEAD_DIM = 64
RW_HEADS = 16
RW_GN_EPS = 64e-5
DH = 64
NSA_HEADS = 16
G = 4
HPG = 4
CMP_LEN = 32
CMP_STRIDE = 16
CMP_HID = 128
SEL_BLOCK = 64
N_SELECT = 16
WINDOW = 512
Q_BLOCK = 128
D_FF = 2816
CONV_W = 3
NORM_EPS = 1e-6
NEG_INF = -1e30
FORCE_SCORE = 1e9
MASK_SCORE = -1e9
PICKED = -3e38

LANES = 128
SUBLANES = 8
VMEM_LIMIT = 48 * 1024 * 1024


def _nt(a, b):
    return lax.dot_general(a, b, (((1,), (1,)), ((), ())), preferred_element_type=f32)


def _split_bf16(x):
    hi = x.astype(bf16)
    lo = (x - hi.astype(f32)).astype(bf16)
    return hi, lo


def _mm_kernel(x_ref, w_ref, o_ref):
    o_ref[...] = jnp.dot(x_ref[...].astype(bf16), w_ref[...], preferred_element_type=f32)


MM_TM = 512


def _pick_tile(n, cands):
    for c in cands:
        if n % c == 0:
            return c
    return n


def mm(x, w):
    m, k = x.shape
    n = w.shape[1]
    w = w.astype(bf16)
    tm = MM_TM if m >= MM_TM else -(-m // SUBLANES) * SUBLANES
    mp = -(-m // tm) * tm
    if mp != m:
        x = jnp.pad(x, ((0, mp - m), (0, 0)))
    tn = n if n <= 1536 else _pick_tile(n, (1408, 1024, 512, 256, 128))
    out = pl.pallas_call(
        _mm_kernel,
        grid=(mp // tm, n // tn),
        in_specs=[pl.BlockSpec((tm, k), lambda i, j: (i, 0)),
                  pl.BlockSpec((k, tn), lambda i, j: (0, j))],
        out_specs=pl.BlockSpec((tm, tn), lambda i, j: (i, j)),
        out_shape=jax.ShapeDtypeStruct((mp, n), f32),
        compiler_params=pltpu.CompilerParams(dimension_semantics=("parallel", "parallel"),
                                             vmem_limit_bytes=VMEM_LIMIT),
        name="mm",
    )(x, w)
    return out[:m] if mp != m else out


def mm3(x, w):
    b, t, k = x.shape
    return mm(x.reshape(b * t, k), w).reshape(b, t, w.shape[1])


PAIR_W = 2 * RW_HEAD_DIM
N_PAIRS = RW_HEADS // 2


def _scan_kernel(r_ref, w_ref, k_ref, v_ref, a_ref, b_ref, s0_ref, y_ref, s_ref, *, bb, tc):
    n = RW_HEAD_DIM
    npairs = bb * N_PAIRS

    @pl.when(pl.program_id(1) == 0)
    def _():
        s_ref[...] = s0_ref[...]

    row = lax.broadcasted_iota(jnp.int32, (n, PAIR_W), 0)
    lane = lax.broadcasted_iota(jnp.int32, (n, PAIR_W), 1)
    diag = (lane % n) == row
    kk = lax.broadcasted_iota(jnp.int32, (PAIR_W, PAIR_W), 0) // n
    ll = lax.broadcasted_iota(jnp.int32, (PAIR_W, PAIR_W), 1) // n
    bones = (kk == ll).astype(bf16)
    group = min(tc, SUBLANES)

    def seg_sum_bcast(x):
        return jnp.dot(x.astype(bf16), bones, preferred_element_type=f32)

    def token(t0, s):
        def rowvec(ref, bi, hp):
            tile = ref[bi, pl.ds(t0, group), pl.ds(hp * PAIR_W, PAIR_W)]
            return jnp.broadcast_to(tile[s:s + 1], (n, PAIR_W))

        pairs = [(bi, hp) for bi in range(bb) for hp in range(N_PAIRS)]
        st = [s_ref[p] for p in range(npairs)]
        sa_in = jnp.concatenate([st[p] * rowvec(a_ref, *pairs[p]) for p in range(npairs)], axis=0)
        v_in = jnp.concatenate([jnp.where(diag, rowvec(v_ref, *pairs[p]), 0.0) for p in range(npairs)], axis=0)
        sa_col = seg_sum_bcast(sa_in)
        v_col = seg_sum_bcast(v_in)
        new = []
        for p in range(npairs):
            sl = slice(p * n, (p + 1) * n)
            sn = (st[p] * rowvec(w_ref, *pairs[p]) + sa_col[sl] * rowvec(b_ref, *pairs[p])
                  + v_col[sl] * rowvec(k_ref, *pairs[p]))
            s_ref[p] = sn
            new.append(sn)
        y_in = jnp.concatenate([new[p] * rowvec(r_ref, *pairs[p]) for p in range(npairs)], axis=0)
        y_col = seg_sum_bcast(y_in)
        rows = []
        for p in range(npairs):
            sl = slice(p * n, (p + 1) * n)
            rows.append(jnp.sum(jnp.where(diag, y_col[sl], 0.0), axis=0, keepdims=True))
        return rows

    def block(g, carry):
        t0 = pl.multiple_of(g * group, group)
        rows = [token(t0, s) for s in range(group)]
        for p in range(npairs):
            bi, hp = divmod(p, N_PAIRS)
            y_ref[bi, pl.ds(t0, group), pl.ds(hp * PAIR_W, PAIR_W)] = jnp.concatenate(
                [rows[s][p] for s in range(group)], axis=0)
        return carry

    lax.fori_loop(0, tc // group, block, 0)


def wkv_scan(r, w, k, v, a, b, s0_pairs, *, bb, tc):
    bsz, t_len, _ = r.shape
    assert bsz % bb == 0 and t_len % tc == 0
    vec_spec = pl.BlockSpec((bb, tc, D_MODEL), lambda i, j: (i, j, 0))
    st_spec = pl.BlockSpec((bb * N_PAIRS, RW_HEAD_DIM, PAIR_W), lambda i, j: (i, 0, 0))
    return pl.pallas_call(
        functools.partial(_scan_kernel, bb=bb, tc=tc),
        grid=(bsz // bb, t_len // tc),
        in_specs=[vec_spec] * 6 + [st_spec],
        out_specs=[vec_spec, st_spec],
        out_shape=[jax.ShapeDtypeStruct((bsz, t_len, D_MODEL), f32),
                   jax.ShapeDtypeStruct((bsz * N_PAIRS, RW_HEAD_DIM, PAIR_W), f32)],
        compiler_params=pltpu.CompilerParams(dimension_semantics=("arbitrary", "arbitrary"),
                                             vmem_limit_bytes=VMEM_LIMIT),
        name="wkv_scan",
    )(r, w, k, v, a, b, s0_pairs)


def to_pairs(s):
    bsz = s.shape[0]
    return s.reshape(bsz, N_PAIRS, 2, RW_HEAD_DIM, RW_HEAD_DIM).transpose(0, 1, 3, 2, 4).reshape(
        bsz * N_PAIRS, RW_HEAD_DIM, PAIR_W)


def from_pairs(s, bsz):
    return s.reshape(bsz, N_PAIRS, RW_HEAD_DIM, 2, RW_HEAD_DIM).transpose(0, 1, 3, 2, 4).reshape(
        bsz, RW_HEADS, RW_HEAD_DIM, RW_HEAD_DIM)


def _masked_softmax(s, mask):
    s = jnp.where(mask, s, NEG_INF)
    e = jnp.where(mask, jnp.exp(s - jnp.max(s, axis=-1, keepdims=True)), 0.0)
    return e / jnp.maximum(jnp.sum(e, axis=-1, keepdims=True), 1e-30)


def _topk_rounds(score, idx, axis):
    sel = jnp.zeros(score.shape, f32)
    idx = idx.astype(f32)
    for _ in range(N_SELECT):
        m = jnp.max(score, axis=axis, keepdims=True)
        first = jnp.min(jnp.where(score == m, idx, 1e6), axis=axis, keepdims=True)
        pick = idx == first
        sel = jnp.where(pick, 1.0, sel)
        score = jnp.where(pick, PICKED, score)
    return sel > 0.5


def _topk_by_rank(score, idx, n_valid):
    rank = jnp.zeros(score.shape, f32)
    for i in range(n_valid):
        ci = score[:, i:i + 1]
        beats = (ci > score) | ((ci == score) & (idx > i))
        rank = rank + jnp.where(beats, 1.0, 0.0)
    return rank < (N_SELECT - 0.5)


def _imp_map_t(n_cmp_pad, n_blk_pad, n_cmp, n_blk):
    ci = np.arange(n_cmp_pad)[None, :] * CMP_STRIDE
    sj = np.arange(n_blk_pad)[:, None] * SEL_BLOCK
    ov = np.clip(np.minimum(ci + CMP_LEN, sj + SEL_BLOCK) - np.maximum(ci, sj), 0, None) / CMP_STRIDE
    ov = ov * (np.arange(n_cmp_pad)[None, :] < n_cmp) * (np.arange(n_blk_pad)[:, None] < n_blk)
    return jnp.asarray(ov, bf16)


NC_PAD = 512
NB_PROMPT = 128
KEXT = NB_PROMPT + DH
SEL_KT = 4 * Q_BLOCK
QROWS = HPG * Q_BLOCK


def _prompt_attn_kernel(q_ref, gt_ref, ck_ref, cvt_ref, impt_ref, ke_ref, vst_ref, kw_ref, vwt_ref, o_ref,
                        s_scr, p_scr, *, n_cmp):
    qi = pl.program_id(2)
    rows = QROWS
    q0 = qi * Q_BLOCK
    q = jnp.concatenate([q_ref[0, :, h * DH:(h + 1) * DH] for h in range(HPG)], axis=0)

    def q_pos(n_keys):
        return q0 + (lax.broadcasted_iota(jnp.int32, (n_keys, rows), 1) & (Q_BLOCK - 1))

    def softmax0(s, ok):
        e = jnp.where(ok, jnp.exp(s - jnp.max(s, axis=0, keepdims=True)), 0.0)
        return e, jnp.maximum(jnp.sum(e, axis=0, keepdims=True), 1e-30)

    nn = lax.broadcasted_iota(jnp.int32, (NC_PAD, rows), 0)
    ok = (nn * CMP_STRIDE + (CMP_LEN - 1) <= q_pos(NC_PAD)) & (nn < n_cmp)
    e, l_c = softmax0(jnp.where(ok, _nt(ck_ref[0, 0], q), NEG_INF), ok)
    p_c = e / l_c
    o_c = jnp.dot(cvt_ref[0, 0], p_c.astype(bf16), preferred_element_type=f32).T
    psum = p_c[:, 0:Q_BLOCK]
    for h in range(1, HPG):
        psum = psum + p_c[:, h * Q_BLOCK:(h + 1) * Q_BLOCK]
    hi, lo = _split_bf16(psum)
    imp_t = (jnp.dot(impt_ref[...], hi, preferred_element_type=f32)
             + jnp.dot(impt_ref[...], lo, preferred_element_type=f32))
    blk = lax.broadcasted_iota(jnp.int32, (NB_PROMPT, Q_BLOCK), 0)
    qp = q0 + lax.broadcasted_iota(jnp.int32, (NB_PROMPT, Q_BLOCK), 1)
    qb = qp >> 6
    forced = (blk == 0) | (blk == qb) | (blk == qb - 1)
    causal = blk * SEL_BLOCK <= qp
    score = jnp.where(causal, jnp.where(forced, FORCE_SCORE, imp_t), MASK_SCORE)
    sel = _topk_rounds(score, blk, 0) & causal
    bias = jnp.where(sel, 0.0, MASK_SCORE).T.astype(bf16)
    qe = jnp.concatenate([jnp.concatenate([bias] * HPG, axis=0), q], axis=1)

    n_wk = WINDOW + Q_BLOCK
    w0 = pl.multiple_of(jnp.maximum(q0 - WINDOW, 0), Q_BLOCK)
    dist = q_pos(n_wk) - (w0 + lax.broadcasted_iota(jnp.int32, (n_wk, rows), 0))
    ok = (dist >= 0) & (dist < WINDOW)
    e, l_w = softmax0(jnp.where(ok, _nt(kw_ref[0, 0, pl.ds(w0, n_wk), :], q), NEG_INF), ok)
    o_w = (jnp.dot(vwt_ref[0, 0, :, pl.ds(w0, n_wk)], e.astype(bf16), preferred_element_type=f32) / l_w).T

    def sel_scores(off):
        return _nt(ke_ref[0, 0, pl.ds(off, SEL_KT), :], qe)

    def pv(j, p):
        off = pl.multiple_of(jnp.maximum(j, 0) * SEL_KT, SEL_KT)
        return jnp.dot(vst_ref[0, 0, :, pl.ds(off, SEL_KT)], p, preferred_element_type=f32)

    def softmax_step(s, m, l):
        m_new = jnp.maximum(m, jnp.max(s, axis=0, keepdims=True))
        alpha = jnp.exp(m - m_new)
        p = jnp.exp(s - m_new)
        return m_new, alpha * l + jnp.sum(p, axis=0, keepdims=True), alpha, p.astype(bf16)

    n_full = q0 // SEL_KT
    s_scr[...] = sel_scores(0)
    p_scr[...] = jnp.zeros((SEL_KT, rows), bf16)

    def sel_body(j, carry):
        m, l, acc, alpha_prev = carry
        s_next = sel_scores(pl.multiple_of((j + 1) * SEL_KT, SEL_KT))
        acc = alpha_prev * acc + pv(j - 1, p_scr[...])
        m, l, alpha, p = softmax_step(s_scr[...], m, l)
        p_scr[...] = p
        s_scr[...] = s_next
        return m, l, acc, alpha

    init = (jnp.full((1, rows), NEG_INF, f32), jnp.zeros((1, rows), f32), jnp.zeros((DH, rows), f32),
            jnp.ones((1, rows), f32))
    m, l, acc, alpha_prev = lax.fori_loop(0, n_full, sel_body, init)
    acc = alpha_prev * acc + pv(n_full - 1, p_scr[...])
    off = pl.multiple_of(n_full * SEL_KT, SEL_KT)
    key_pos = off + lax.broadcasted_iota(jnp.int32, (SEL_KT, rows), 0)
    s = jnp.where(key_pos <= q_pos(SEL_KT), s_scr[...], 2.0 * MASK_SCORE)
    _, l_s, alpha, p = softmax_step(s, m, l)
    acc = alpha * acc + jnp.dot(vst_ref[0, 0, :, pl.ds(off, SEL_KT)], p, preferred_element_type=f32)
    o_s = (acc / jnp.maximum(l_s, 1e-30)).T

    gt = gt_ref[0, 0]
    gate = lambda c: jnp.concatenate([gt[:, 3 * h + c:3 * h + c + 1] for h in range(HPG)], axis=0)
    o = gate(0) * o_c + gate(1) * o_s + gate(2) * o_w
    for h in range(HPG):
        o_ref[0, :, h * DH:(h + 1) * DH] = o[h * Q_BLOCK:(h + 1) * Q_BLOCK]


def nsa_prompt_attention(q, gt, ck, cvt, ke, vst, kw, vwt, n_cmp):
    bsz, t_len, _ = q.shape
    n_q = t_len // Q_BLOCK
    assert t_len % SEL_KT == 0 and t_len >= WINDOW + Q_BLOCK
    impt = _imp_map_t(NC_PAD, NB_PROMPT, n_cmp, NB_PROMPT)
    tile = pl.BlockSpec((1, Q_BLOCK, HPG * DH), lambda b, g, i: (b, i, g))
    per_bg = lambda r, w: pl.BlockSpec((1, 1, r, w), lambda b, g, i: (b, g, 0, 0))
    return pl.pallas_call(
        functools.partial(_prompt_attn_kernel, n_cmp=n_cmp),
        grid=(bsz, G, n_q),
        in_specs=[tile, pl.BlockSpec((1, 1, Q_BLOCK, 3 * HPG), lambda b, g, i: (b, g, i, 0)),
                  per_bg(NC_PAD, DH), per_bg(DH, NC_PAD),
                  pl.BlockSpec((NB_PROMPT, NC_PAD), lambda b, g, i: (0, 0)),
                  per_bg(t_len, KEXT), per_bg(DH, t_len), per_bg(t_len, DH), per_bg(DH, t_len)],
        out_specs=tile,
        out_shape=jax.ShapeDtypeStruct((bsz, t_len, G * HPG * DH), f32),
        scratch_shapes=[pltpu.VMEM((SEL_KT, QROWS), f32), pltpu.VMEM((SEL_KT, QROWS), bf16)],
        compiler_params=pltpu.CompilerParams(dimension_semantics=("parallel", "parallel", "arbitrary"),
                                             vmem_limit_bytes=VMEM_LIMIT),
        name="nsa_prompt_attention",
    )(q, gt, ck, cvt, impt, ke, vst, kw, vwt)


N_SUB = PAST_LEN // CMP_STRIDE
HALF = 2 * G * DH


def _gelu(x):
    return 0.5 * x * (1.0 + jnp.tanh(0.7978845608028654 * (x + 0.044715 * (x * x * x))))


def _sample_cmp_kernel(pt_ref, *refs):
    pages = refs[:N_PAGES]
    pe_ref, w1_ref, b1_ref, w2_ref, b2_ref, kg_ref, ck_ref, cv_ref, x_scr = refs[N_PAGES:]
    n_chunk = HALF // LANES
    for c in range(n_chunk):
        for j in range(N_PAGES):
            x_scr[c, pl.ds(j * PAGE_SIZE, PAGE_SIZE), :] = pages[j][0, :, pl.ds(c * LANES, LANES)]
        x_scr[c, pl.ds(PAST_LEN, CMP_STRIDE), :] = jnp.zeros((CMP_STRIDE, LANES), f32)
    low = lax.broadcasted_iota(jnp.int32, ((G // 2) * N_SUB, LANES), 1) < DH
    for part in range(2):
        acc = jnp.zeros(((G // 2) * N_SUB, 2 * CMP_HID), f32)
        for s in range(CMP_LEN):
            x = jnp.concatenate([x_scr[part * (G // 2) + c, pl.ds(s, N_SUB, stride=CMP_STRIDE), :]
                                 for c in range(G // 2)], axis=0)
            x = x + pe_ref[part, s:s + 1, :]
            acc = acc + jnp.dot(x.astype(bf16), w1_ref[part, s], preferred_element_type=f32)
        hid = _gelu(acc + b1_ref[part:part + 1, :])
        out = jnp.dot(hid.astype(bf16), w2_ref[part], preferred_element_type=f32) + b2_ref[part:part + 1, :]
        if part == 0:
            sq = out * out
            ms = jnp.where(low, jnp.sum(jnp.where(low, sq, 0.0), axis=-1, keepdims=True),
                           jnp.sum(jnp.where(low, 0.0, sq), axis=-1, keepdims=True)) * (1.0 / DH)
            out = out * lax.rsqrt(ms + NORM_EPS) * kg_ref[...]
        out_ref = ck_ref if part == 0 else cv_ref
        for c in range(G // 2):
            out_ref[0, c] = out[c * N_SUB:(c + 1) * N_SUB].astype(bf16)


def sample_compress(cache3, page_table, pe, w1, b1, w2, b2, kg):
    bsz = page_table.shape[0]
    twice = lambda a: jnp.concatenate([a, a], axis=-1)
    w1 = w1.reshape(2, CMP_LEN, DH, CMP_HID)
    z1, z2 = jnp.zeros_like(w1), jnp.zeros_like(w2)
    w1d = jnp.concatenate([jnp.concatenate([w1, z1], -1), jnp.concatenate([z1, w1], -1)], axis=-2).astype(bf16)
    w2d = jnp.concatenate([jnp.concatenate([w2, z2], -1), jnp.concatenate([z2, w2], -1)], axis=-2).astype(bf16)
    pe, b1, b2, kg = twice(pe), twice(b1), twice(b2), twice(kg)

    def page_spec(j):
        return pl.BlockSpec((1, PAGE_SIZE, HALF), lambda b, pt: (pt[b, j], 0, 0))

    const = lambda a: pl.BlockSpec(a.shape, lambda b, pt: (0,) * a.ndim)
    out_spec = pl.BlockSpec((1, G // 2, N_SUB, LANES), lambda b, pt: (b, 0, 0, 0))
    grid_spec = pltpu.PrefetchScalarGridSpec(
        num_scalar_prefetch=1, grid=(bsz,),
        in_specs=[page_spec(j) for j in range(N_PAGES)] + [const(a) for a in (pe, w1d, b1, w2d, b2, kg)],
        out_specs=[out_spec, out_spec],
        scratch_shapes=[pltpu.VMEM((HALF // LANES, PAST_LEN + CMP_STRIDE, LANES), f32)])
    return pl.pallas_call(
        _sample_cmp_kernel, grid_spec=grid_spec,
        out_shape=[jax.ShapeDtypeStruct((bsz, G // 2, N_SUB, LANES), bf16)] * 2,
        compiler_params=pltpu.CompilerParams(dimension_semantics=("arbitrary",), vmem_limit_bytes=VMEM_LIMIT),
        name="sample_compress",
    )(page_table, *([cache3] * N_PAGES), pe, w1d, b1, w2d, b2, kg)


T_NEW = 4
NEW_PAD = 128
NB_SAMPLE = -(-(PAST_LEN + T_NEW) // SEL_BLOCK)
N_CMP_SAMPLE = (PAST_LEN + T_NEW) // CMP_STRIDE - 1
ROWS_S = HPG * T_NEW


def _sample_attn_kernel(pt_ref, *refs):
    pages = refs[:N_PAGES]
    (q_ref, gt_ref, ck_ref, cv_ref, new_ref, win_ref, wnew_ref, impt_ref, oh_ref, ohn_ref, o_ref) = refs[N_PAGES:]
    rows = 2 * ROWS_S
    t_of_row = lax.broadcasted_iota(jnp.int32, (rows, LANES), 0) & (T_NEW - 1)
    lane = lax.broadcasted_iota(jnp.int32, (rows, LANES), 1)
    q_pos = PAST_LEN + t_of_row
    hs_o = lax.broadcasted_iota(jnp.int32, (SUBLANES, rows), 0)
    hs_r = lax.broadcasted_iota(jnp.int32, (SUBLANES, rows), 1)
    hsum = ((hs_o >> 2 == hs_r >> 4) & ((hs_o & 3) == (hs_r & 3))).astype(bf16)
    sp_r = lax.broadcasted_iota(jnp.int32, (rows, SUBLANES), 0)
    sp_o = lax.broadcasted_iota(jnp.int32, (rows, SUBLANES), 1)
    spread = ((sp_o >> 2 == sp_r >> 4) & ((sp_o & 3) == (sp_r & 3))).astype(bf16)
    blk8 = lax.broadcasted_iota(jnp.int32, (SUBLANES, LANES), 1)
    qp8 = PAST_LEN + (lax.broadcasted_iota(jnp.int32, (SUBLANES, LANES), 0) & (T_NEW - 1))
    qb8 = qp8 >> 6
    forced = (blk8 == 0) | (blk8 == qb8) | (blk8 == qb8 - 1)
    causal = (blk8 * SEL_BLOCK <= qp8) & (blk8 < NB_SAMPLE)
    n_buf = win_ref.shape[1]

    for c in range(G // 2):
        lanes_k = pl.ds(c * LANES, LANES)
        lanes_v = pl.ds(G * DH + c * LANES, LANES)
        q = q_ref[0, c]
        gt = gt_ref[0, c]
        s_c = _nt(q, ck_ref[0, c])
        ok_c = (lane * CMP_STRIDE + (CMP_LEN - 1) <= q_pos) & (lane < N_CMP_SAMPLE)
        p_c = _masked_softmax(s_c, ok_c)
        o_c = jnp.dot(p_c.astype(bf16), cv_ref[0, c], preferred_element_type=f32)
        hi, lo = _split_bf16(p_c)
        psum = jnp.dot(hsum, hi, preferred_element_type=f32) + jnp.dot(hsum, lo, preferred_element_type=f32)
        hi, lo = _split_bf16(psum)
        imp = _nt(hi, impt_ref[...]) + _nt(lo, impt_ref[...])
        score = jnp.where(blk8 < NB_SAMPLE,
                          jnp.where(causal, jnp.where(forced, FORCE_SCORE, imp), MASK_SCORE), PICKED)
        sel = _topk_by_rank(score, blk8, NB_SAMPLE) & causal
        bias8 = jnp.where(sel, 0.0, MASK_SCORE).astype(bf16)
        bias = jnp.dot(spread, bias8, preferred_element_type=f32).astype(bf16)
        k_all = jnp.concatenate([pages[j][0, :, lanes_k] for j in range(N_PAGES)], axis=0).astype(bf16)
        s_past = _nt(q, k_all) + jnp.dot(bias, oh_ref[...], preferred_element_type=f32)
        s_new = _nt(q, new_ref[0, :, lanes_k].astype(bf16)) + jnp.dot(bias, ohn_ref[...],
                                                                      preferred_element_type=f32)
        s_new = jnp.where(lane <= t_of_row, s_new, 2.0 * MASK_SCORE)
        s_all = jnp.concatenate([s_past, s_new], axis=1)
        e = jnp.exp(s_all - jnp.max(s_all, axis=1, keepdims=True))
        den = jnp.maximum(jnp.sum(e, axis=1, keepdims=True), 1e-30)
        v_all = jnp.concatenate([pages[j][0, :, lanes_v] for j in range(N_PAGES)] + [new_ref[0, :, lanes_v]],
                                axis=0).astype(bf16)
        o_s = jnp.dot(e.astype(bf16), v_all, preferred_element_type=f32) / den
        kw_all = jnp.concatenate([win_ref[0, :, lanes_k], wnew_ref[0, :, lanes_k]], axis=0).astype(bf16)
        idx = lax.broadcasted_iota(jnp.int32, (rows, n_buf), 1)
        t_w = lax.broadcasted_iota(jnp.int32, (rows, n_buf), 0) & (T_NEW - 1)
        w_pos = PAST_LEN - n_buf + idx
        dist = PAST_LEN + t_w - w_pos
        ok_w = jnp.concatenate([(dist >= 0) & (dist < WINDOW) & (w_pos >= 0), lane <= t_of_row], axis=1)
        p_w = _masked_softmax(_nt(q, kw_all), ok_w).astype(bf16)
        vw_all = jnp.concatenate([win_ref[0, :, lanes_v], wnew_ref[0, :, lanes_v]], axis=0).astype(bf16)
        o_w = jnp.dot(p_w, vw_all, preferred_element_type=f32)
        o_ref[0, c] = gt[:, 0:1] * o_c + gt[:, 1:2] * o_s + gt[:, 2:3] * o_w


def sample_attention(cache3, page_table, q2, gt2, ck2, cv2, new_sel, win_cache, new_win):
    bsz = page_table.shape[0]
    n_buf = win_cache.shape[1]
    rows = 2 * ROWS_S
    impt = _imp_map_t(N_SUB, LANES, N_CMP_SAMPLE, NB_SAMPLE)
    key_blk = np.arange(PAST_LEN) // SEL_BLOCK
    onehot = jnp.asarray(key_blk[None, :] == np.arange(LANES)[:, None], bf16)
    onehot_new = jnp.asarray(np.broadcast_to(np.arange(LANES)[:, None] == PAST_LEN // SEL_BLOCK,
                                             (LANES, NEW_PAD)), bf16)

    def page_spec(j):
        return pl.BlockSpec((1, PAGE_SIZE, HALF), lambda b, pt: (pt[b, j], 0, 1))

    per_seq = lambda shape: pl.BlockSpec((1,) + shape, lambda b, pt: (b,) + (0,) * len(shape))
    const = lambda shape: pl.BlockSpec(shape, lambda b, pt: (0,) * len(shape))
    grid_spec = pltpu.PrefetchScalarGridSpec(
        num_scalar_prefetch=1, grid=(bsz,),
        in_specs=[page_spec(j) for j in range(N_PAGES)] + [
            per_seq((G // 2, rows, LANES)), per_seq((G // 2, rows, 3)),
            per_seq((G // 2, N_SUB, LANES)), per_seq((G // 2, N_SUB, LANES)),
            per_seq((NEW_PAD, HALF)), per_seq((n_buf, HALF)), per_seq((NEW_PAD, HALF)),
            const((LANES, N_SUB)), const((LANES, PAST_LEN)), const((LANES, NEW_PAD))],
        out_specs=per_seq((G // 2, rows, LANES)))
    return pl.pallas_call(
        _sample_attn_kernel, grid_spec=grid_spec,
        out_shape=jax.ShapeDtypeStruct((bsz, G // 2, rows, LANES), f32),
        compiler_params=pltpu.CompilerParams(dimension_semantics=("arbitrary",), vmem_limit_bytes=VMEM_LIMIT),
        name="sample_attention",
    )(page_table, *([cache3] * N_PAGES), q2, gt2, ck2, cv2, new_sel, win_cache, new_win, impt, onehot, onehot_new)


def rmsnorm(x, g):
    y = x * lax.rsqrt(jnp.mean(x * x, -1, keepdims=True) + NORM_EPS)
    return y * g


def _ffn_in_kernel(x_ref, g_ref, w_ref, o_ref):
    x = x_ref[...]
    xn = x * lax.rsqrt(jnp.mean(x * x, axis=-1, keepdims=True) + NORM_EPS) * g_ref[...]
    o_ref[...] = jnp.dot(xn.astype(bf16), w_ref[...], preferred_element_type=f32)


def _ffn_out_kernel(x_ref, h_ref, halo_ref, cs_ref, cw_ref, cb_ref, w_ref, o_ref, u_scr, *, tiles_per_seq):
    first = pl.program_id(0) % tiles_per_seq == 0
    tm = h_ref.shape[0]
    u_scr[pl.ds(0, SUBLANES), :] = halo_ref[...]
    u_scr[pl.ds(SUBLANES, tm), :] = h_ref[:, pl.ds(0, D_FF)]

    @pl.when(first)
    def _():
        u_scr[pl.ds(SUBLANES - (CONV_W - 1), CONV_W - 1), :] = cs_ref[0]

    uc = cb_ref[...] + cw_ref[CONV_W - 1:CONV_W, :] * u_scr[pl.ds(SUBLANES, tm), :]
    for j in range(CONV_W - 1):
        uc = uc + cw_ref[j:j + 1, :] * u_scr[pl.ds(SUBLANES - (CONV_W - 1) + j, tm), :]
    act = _gelu(uc) * h_ref[:, pl.ds(D_FF, D_FF)]
    o_ref[...] = x_ref[...] + jnp.dot(act.astype(bf16), w_ref[...], preferred_element_type=f32)


FFN_TM = 1024
FFN_TN = 1408
FFN_OUT_TM = 256


def norm_mm(x, g, w, tn):
    m, n = x.shape[0], w.shape[1]
    tm = FFN_TM if m % FFN_TM == 0 else m
    return pl.pallas_call(
        _ffn_in_kernel,
        grid=(m // tm, n // tn),
        in_specs=[pl.BlockSpec((tm, D_MODEL), lambda i, j: (i, 0)),
                  pl.BlockSpec((1, D_MODEL), lambda i, j: (0, 0)),
                  pl.BlockSpec((D_MODEL, tn), lambda i, j: (0, j))],
        out_specs=pl.BlockSpec((tm, tn), lambda i, j: (i, j)),
        out_shape=jax.ShapeDtypeStruct((m, n), f32),
        compiler_params=pltpu.CompilerParams(dimension_semantics=("parallel", "parallel"),
                                             vmem_limit_bytes=VMEM_LIMIT),
        name="norm_mm",
    )(x, g.reshape(1, D_MODEL), w.astype(bf16))


def conv_ffn_fused(x, g, conv_state, w_in, conv_w, conv_b, w_out):
    bsz, t_len, _ = x.shape
    m = bsz * t_len
    tiles_per_seq = t_len // FFN_OUT_TM
    x2 = x.reshape(m, D_MODEL)
    h = norm_mm(x2, g, w_in, FFN_TN)
    halo_blocks = FFN_OUT_TM // SUBLANES
    y = pl.pallas_call(
        functools.partial(_ffn_out_kernel, tiles_per_seq=tiles_per_seq),
        grid=(m // FFN_OUT_TM,),
        in_specs=[pl.BlockSpec((FFN_OUT_TM, D_MODEL), lambda i: (i, 0)),
                  pl.BlockSpec((FFN_OUT_TM, 2 * D_FF), lambda i: (i, 0)),
                  pl.BlockSpec((SUBLANES, D_FF), lambda i: (jnp.maximum(i * halo_blocks - 1, 0), 0)),
                  pl.BlockSpec((1, CONV_W - 1, D_FF), lambda i: (i // tiles_per_seq, 0, 0)),
                  pl.BlockSpec((CONV_W, D_FF), lambda i: (0, 0)),
                  pl.BlockSpec((1, D_FF), lambda i: (0, 0)),
                  pl.BlockSpec((D_FF, D_MODEL), lambda i: (0, 0))],
        out_specs=pl.BlockSpec((FFN_OUT_TM, D_MODEL), lambda i: (i, 0)),
        out_shape=jax.ShapeDtypeStruct((m, D_MODEL), f32),
        scratch_shapes=[pltpu.VMEM((SUBLANES + FFN_OUT_TM, D_FF), f32)],
        compiler_params=pltpu.CompilerParams(dimension_semantics=("arbitrary",), vmem_limit_bytes=56 * 1024 * 1024),
        name="ffn_out",
    )(x2, h, h, conv_state, conv_w, conv_b.reshape(1, D_FF), w_out.astype(bf16))
    conv_new = h.reshape(bsz, t_len, 2 * D_FF)[:, t_len - (CONV_W - 1):, :D_FF]
    return y.reshape(bsz, t_len, D_MODEL), conv_new


def conv_ffn(xn, conv_state, w_in, conv_w, conv_b, w_out):
    t_len = xn.shape[1]
    u, gate = jnp.split(mm3(xn, w_in), 2, axis=-1)
    u_pad = jnp.concatenate([conv_state, u], axis=1)
    uc = conv_b + conv_w[0] * u_pad[:, 0:t_len]
    for j in range(1, CONV_W):
        uc = uc + conv_w[j] * u_pad[:, j:j + t_len]
    return mm3(jax.nn.gelu(uc) * gate, w_out), u_pad[:, t_len:]


RW_TM = 256
SEG_W = LANES


def _head_seg():
    seg = np.arange(D_MODEL)[:, None] // RW_HEAD_DIM == np.arange(SEG_W)[None, :]
    return jnp.asarray(seg, bf16), jnp.asarray(seg.T, bf16)


def _head_sum(x, seg_ref):
    hi, lo = _split_bf16(x)
    return jnp.dot(hi, seg_ref[...], preferred_element_type=f32) + jnp.dot(lo, seg_ref[...],
                                                                           preferred_element_type=f32)


def _head_expand(s, segt_ref):
    hi, lo = _split_bf16(s)
    return jnp.dot(hi, segt_ref[...], preferred_element_type=f32) + jnp.dot(lo, segt_ref[...],
                                                                            preferred_element_type=f32)


def _rwkv_pre_kernel(*refs, has_vfirst):
    it = iter(refs)
    xn_ref, xs_ref = next(it), next(it)
    vf_ref = next(it) if has_vfirst else None
    mu_ref, wr, wk, wv, w1, w2, a1, a2 = (next(it) for _ in range(8))
    v1, v2 = (next(it), next(it)) if has_vfirst else (None, None)
    g1, g2, w0, a0 = (next(it) for _ in range(4))
    v0 = next(it) if has_vfirst else None
    kk_ref, ka_ref, seg_ref, segt_ref = (next(it) for _ in range(4))
    r_o, w_o, k_o, v_o, nkk_o, kka_o, g_o = (next(it) for _ in range(7))

    dot = lambda x, w: jnp.dot(x.astype(bf16), w[...], preferred_element_type=f32)
    xn = xn_ref[...]
    xx = xs_ref[...] - xn
    xr, xw, xk, xv, xa, xg = (xn + xx * mu_ref[i:i + 1, :] for i in range(6))
    r = dot(xr, wr)
    k = dot(xk, wk)
    v = dot(xv, wv)
    z = -(w0[...] + dot(jnp.tanh(dot(xw, w1)), w2))
    softplus = jnp.maximum(z, 0.0) + jnp.log1p(jnp.exp(-jnp.abs(z)))
    decay = jnp.exp(-jnp.exp(-softplus - 0.5))
    a = jax.nn.sigmoid(a0[...] + dot(dot(xa, a1), a2))
    if has_vfirst:
        v = v + (vf_ref[...] - v) * jax.nn.sigmoid(v0[...] + dot(dot(xv, v1), v2))
    g = dot(jax.nn.sigmoid(dot(xg, g1)), g2)
    kk = k * kk_ref[...]
    norm = jnp.maximum(jnp.sqrt(_head_sum(kk * kk, seg_ref)), 1e-12)
    kk = kk * _head_expand(1.0 / norm, segt_ref)
    r_o[...] = r
    w_o[...] = decay
    k_o[...] = k * (1.0 + (a - 1.0) * ka_ref[...])
    v_o[...] = v
    nkk_o[...] = -kk
    kka_o[...] = kk * a
    g_o[...] = g


def rwkv_pre(xn, xs, v_first, li, p):
    m = xn.shape[0]
    tm = RW_TM if m % RW_TM == 0 else m
    has_vfirst = v_first is not None
    seg, segt = _head_seg()
    row = lambda a: a.reshape(1, -1)
    cast = lambda w: w.astype(bf16)
    ops = [xn, xs] + ([v_first] if has_vfirst else [])
    ops += [p['rw_mu'][li], cast(p['rw_w_r'][li]), cast(p['rw_w_k'][li]), cast(p['rw_w_v'][li]),
            cast(p['rw_w1'][li]), cast(p['rw_w2'][li]), cast(p['rw_a1'][li]), cast(p['rw_a2'][li])]
    if has_vfirst:
        ops += [cast(p['rw_v1'][li - 1]), cast(p['rw_v2'][li - 1])]
    ops += [cast(p['rw_g1'][li]), cast(p['rw_g2'][li]), row(p['rw_w0'][li]), row(p['rw_a0'][li])]
    if has_vfirst:
        ops += [row(p['rw_v0'][li - 1])]
    ops += [row(p['rw_k_k'][li]), row(p['rw_k_a'][li]), seg, segt]
    n_rows = 3 if has_vfirst else 2
    tile = pl.BlockSpec((tm, D_MODEL), lambda i: (i, 0))
    whole = lambda a: pl.BlockSpec(a.shape, lambda i: (0,) * a.ndim)
    return pl.pallas_call(
        functools.partial(_rwkv_pre_kernel, has_vfirst=has_vfirst),
        grid=(m // tm,),
        in_specs=[tile] * n_rows + [whole(a) for a in ops[n_rows:]],
        out_specs=[tile] * 7,
        out_shape=[jax.ShapeDtypeStruct((m, D_MODEL), f32)] * 7,
        compiler_params=pltpu.CompilerParams(dimension_semantics=("parallel",), vmem_limit_bytes=VMEM_LIMIT),
        name="rwkv_pre",
    )(*ops)


def _rwkv_post_kernel(y_ref, r_ref, k_ref, v_ref, g_ref, x_ref, lnw, lnb, rk, wo, seg_ref, segt_ref, o_ref):
    y = y_ref[...]
    inv_n = 1.0 / RW_HEAD_DIM
    yc = y - _head_expand(_head_sum(y, seg_ref) * inv_n, segt_ref)
    var = _head_expand(_head_sum(yc * yc, seg_ref) * inv_n, segt_ref)
    yn = yc * lax.rsqrt(var + RW_GN_EPS) * lnw[...] + lnb[...]
    bonus = _head_expand(_head_sum(r_ref[...] * k_ref[...] * rk[...], seg_ref), segt_ref) * v_ref[...]
    out = ((yn + bonus) * g_ref[...]).astype(bf16)
    o_ref[...] = x_ref[...] + jnp.dot(out, wo[...], preferred_element_type=f32)


def rwkv_post(y, r, k, v, g, x, li, p):
    m = y.shape[0]
    tm = RW_TM if m % RW_TM == 0 else m
    seg, segt = _head_seg()
    row = lambda a: a.reshape(1, -1)
    consts = [row(p['rw_ln_w'][li]), row(p['rw_ln_b'][li]), row(p['rw_r_k'][li]), p['rw_w_o'][li].astype(bf16),
              seg, segt]
    tile = pl.BlockSpec((tm, D_MODEL), lambda i: (i, 0))
    whole = lambda a: pl.BlockSpec(a.shape, lambda i: (0,) * a.ndim)
    return pl.pallas_call(
        _rwkv_post_kernel,
        grid=(m // tm,),
        in_specs=[tile] * 6 + [whole(a) for a in consts],
        out_specs=tile,
        out_shape=jax.ShapeDtypeStruct((m, D_MODEL), f32),
        compiler_params=pltpu.CompilerParams(dimension_semantics=("parallel",), vmem_limit_bytes=VMEM_LIMIT),
        name="rwkv_post",
    )(y, r, k, v, g, x, *consts)


def rwkv7_layer(x, shift_prev, s0, v_first, li, p, scan_cfg):
    bsz, t_len, _ = x.shape
    m = bsz * t_len
    xn = rmsnorm(x, p['norm_g'][li, 0])
    xs = jnp.concatenate([shift_prev[:, None], xn[:, :-1]], axis=1)
    flat = lambda a: a.reshape(m, D_MODEL)
    r, decay, k, v, nkk, kka, g = rwkv_pre(flat(xn), flat(xs), None if li == 0 else flat(v_first), li, p)
    if li == 0:
        v_first = v
    seq = lambda a: a.reshape(bsz, t_len, D_MODEL)
    y, s_new = wkv_scan(seq(r), seq(decay), seq(k), seq(v), seq(nkk), seq(kka), to_pairs(s0), **scan_cfg)
    x_new = rwkv_post(flat(y), r, k, v, g, flat(x), li, p)
    return seq(x_new), from_pairs(s_new, bsz), xn[:, -1], seq(v_first)


def kv_rows(h, p):
    bsz, t_len, _ = h.shape
    kv = norm_mm(h.reshape(bsz * t_len, D_MODEL), p['kv_norm_g'], p['w_kv'], p['w_kv'].shape[1] // 2)
    kv = kv.reshape(bsz, t_len, 6, G, DH)
    k_sel = rmsnorm(kv[:, :, 2], p['k_norm_g'][1])
    k_win = rmsnorm(kv[:, :, 4], p['k_norm_g'][2])
    paged = jnp.stack([kv[:, :, 0], kv[:, :, 1], k_sel, kv[:, :, 3]], axis=2)
    win = jnp.stack([k_win, kv[:, :, 5]], axis=2)
    return paged, win


def _nsa_q_kernel(x_ref, g_ref, w_ref, qg_ref, seg_ref, segt_ref, q_ref, gt_ref):
    n_q = NSA_HEADS * DH
    x = x_ref[...]
    xn = x * lax.rsqrt(jnp.mean(x * x, axis=-1, keepdims=True) + NORM_EPS) * g_ref[...]
    proj = jnp.dot(xn.astype(bf16), w_ref[...], preferred_element_type=f32)
    q = proj[:, :n_q]
    inv = lax.rsqrt(_head_sum(q * q, seg_ref) * (1.0 / DH) + NORM_EPS)
    q_ref[...] = (q * _head_expand(inv, segt_ref) * qg_ref[...]).astype(bf16)
    gt_ref[...] = jax.nn.sigmoid(proj[:, n_q:])


def nsa_query(x, g, lb, p):
    bsz, t_len, _ = x.shape
    m = bsz * t_len
    n_q = NSA_HEADS * DH
    tm = RW_TM if m % RW_TM == 0 else m
    w = jnp.pad(p['nsa_w_q'][lb], ((0, 0), (0, LANES - 3 * NSA_HEADS))).astype(bf16)
    qg = (jnp.tile(p['q_norm_g'][lb], NSA_HEADS) * (DH ** -0.5)).reshape(1, n_q)
    seg, segt = _head_seg()
    consts = [g.reshape(1, D_MODEL), w, qg, seg, segt]
    whole = lambda a: pl.BlockSpec(a.shape, lambda i: (0,) * a.ndim)
    q, gt = pl.pallas_call(
        _nsa_q_kernel,
        grid=(m // tm,),
        in_specs=[pl.BlockSpec((tm, D_MODEL), lambda i: (i, 0))] + [whole(a) for a in consts],
        out_specs=[pl.BlockSpec((tm, n_q), lambda i: (i, 0)), pl.BlockSpec((tm, LANES), lambda i: (i, 0))],
        out_shape=[jax.ShapeDtypeStruct((m, n_q), bf16), jax.ShapeDtypeStruct((m, LANES), f32)],
        compiler_params=pltpu.CompilerParams(dimension_semantics=("parallel",), vmem_limit_bytes=VMEM_LIMIT),
        name="nsa_q",
    )(x.reshape(m, D_MODEL), *consts)
    return (q.reshape(bsz, t_len, G, HPG, DH),
            gt[:, :3 * NSA_HEADS].reshape(bsz, t_len, G, HPG, 3))


def _res_mm_kernel(a_ref, x_ref, w_ref, o_ref):
    o_ref[...] = x_ref[...] + jnp.dot(a_ref[...].astype(bf16), w_ref[...], preferred_element_type=f32)


def residual_mm(a, x, w):
    bsz, t_len, k = a.shape
    n = w.shape[1]
    m = bsz * t_len
    tm = MM_TM if m % MM_TM == 0 else m
    out = pl.pallas_call(
        _res_mm_kernel,
        grid=(m // tm,),
        in_specs=[pl.BlockSpec((tm, k), lambda i: (i, 0)), pl.BlockSpec((tm, n), lambda i: (i, 0)),
                  pl.BlockSpec((k, n), lambda i: (0, 0))],
        out_specs=pl.BlockSpec((tm, n), lambda i: (i, 0)),
        out_shape=jax.ShapeDtypeStruct((m, n), f32),
        compiler_params=pltpu.CompilerParams(dimension_semantics=("parallel",), vmem_limit_bytes=VMEM_LIMIT),
        name="residual_mm",
    )(a.reshape(m, k), x.reshape(m, n), w.astype(bf16))
    return out.reshape(bsz, t_len, n)


def compress_rows_prompt(rows, pe, w1, b1, w2, b2):
    bsz, t_len = rows.shape[:2]
    n_sub = t_len // CMP_STRIDE
    sub = rows.reshape(bsz, n_sub, CMP_STRIDE, G, DH).transpose(0, 1, 3, 2, 4)
    half = CMP_STRIDE * DH
    x0 = (sub + pe[:CMP_STRIDE][None, None, None]).reshape(bsz * n_sub * G, half)
    x1 = (sub + pe[CMP_STRIDE:][None, None, None]).reshape(bsz * n_sub * G, half)
    p0 = mm(x0, w1[:half]).reshape(bsz, n_sub, G, CMP_HID)
    p1 = mm(x1, w1[half:]).reshape(bsz, n_sub, G, CMP_HID)
    hid = jax.nn.gelu(p0[:, :-1] + p1[:, 1:] + b1)
    out = mm(hid.reshape(bsz * (n_sub - 1) * G, CMP_HID), w2) + b2
    return out.reshape(bsz, n_sub - 1, G, DH)


def prompt_context(h, p):
    paged, win = kv_rows(h, p)
    bsz, t_len = h.shape[:2]
    ck = rmsnorm(compress_rows_prompt(paged[:, :, 0], p['cmp_pe'][0], p['cmp_w1'][0], p['cmp_b1'][0],
                                      p['cmp_w2'][0], p['cmp_b2'][0]), p['k_norm_g'][0])
    cv = compress_rows_prompt(paged[:, :, 1], p['cmp_pe'][1], p['cmp_w1'][1], p['cmp_b1'][1],
                              p['cmp_w2'][1], p['cmp_b2'][1])
    n_cmp = ck.shape[1]
    pad = lambda c: jnp.pad(c, ((0, 0), (0, NC_PAD - n_cmp), (0, 0), (0, 0)))
    heads = lambda x: x.transpose(0, 2, 1, 3).astype(bf16)
    heads_t = lambda x: x.transpose(0, 2, 3, 1).astype(bf16)
    onehot = jnp.asarray(np.arange(t_len)[:, None] // SEL_BLOCK == np.arange(NB_PROMPT)[None, :], bf16)
    ke = jnp.concatenate([jnp.broadcast_to(onehot, (bsz, G, t_len, NB_PROMPT)), heads(paged[:, :, 2])], axis=-1)
    ctx = dict(ck=heads(pad(ck)), cvt=heads_t(pad(cv)), n_cmp=n_cmp, ke=ke, vst=heads_t(paged[:, :, 3]),
               kw=heads(win[:, :, 0]), vwt=heads_t(win[:, :, 1]))
    keep = min(WINDOW, t_len)
    return ctx, (paged, win[:, t_len - keep:])


def attend_prompt(q, gates, ctx):
    bsz, t_len = q.shape[:2]
    gt = gates.reshape(bsz, t_len, G, 3 * HPG).transpose(0, 2, 1, 3)
    return nsa_prompt_attention(q.reshape(bsz, t_len, NSA_HEADS * DH).astype(bf16), gt, ctx['ck'], ctx['cvt'],
                                ctx['ke'], ctx['vst'], ctx['kw'], ctx['vwt'], ctx['n_cmp'])


def sample_context(h, cache3, page_table, cache_kv_win, p):
    paged, win = kv_rows(h, p)
    bsz, t_new = h.shape[:2]
    ck, cv = sample_compress(cache3, page_table, p['cmp_pe'], p['cmp_w1'], p['cmp_b1'], p['cmp_w2'], p['cmp_b2'],
                             p['k_norm_g'][0:1])
    n_buf = cache_kv_win.shape[1]
    pad_rows = lambda x: jnp.pad(x.reshape(bsz, t_new, HALF), ((0, 0), (0, NEW_PAD - t_new), (0, 0)))
    ctx = dict(ck=ck, cv=cv, new_sel=pad_rows(paged[:, :, 2:4]), new_win=pad_rows(win),
               win_cache=cache_kv_win.reshape(bsz, n_buf, HALF))
    win_full = jnp.concatenate([cache_kv_win, win], axis=1)
    return ctx, (paged, win_full[:, win_full.shape[1] - n_buf:])


def attend_sample(q, gates, ctx, cache3, page_table):
    bsz, t_new = q.shape[:2]
    rows = lambda x: x.transpose(0, 2, 3, 1, 4).reshape(bsz, G // 2, 2 * HPG * t_new, x.shape[-1])
    qr = rows(q)
    own = (jnp.arange(2 * HPG * t_new) // (HPG * t_new))[:, None] == (jnp.arange(LANES) // DH)[None, :]
    q2 = jnp.where(own, jnp.concatenate([qr, qr], axis=-1), 0.0).astype(bf16)
    o = sample_attention(cache3, page_table, q2, rows(gates), ctx['ck'], ctx['cv'],
                         ctx['new_sel'], ctx['win_cache'], ctx['new_win'])
    o = jnp.where(own, o, 0.0)
    o = o[..., :DH] + o[..., DH:]
    return o.reshape(bsz, G, HPG, t_new, DH).transpose(0, 3, 1, 2, 4).reshape(bsz, t_new, NSA_HEADS * DH)


def forward(x, wkv_in, shift_in, conv_in, make_ctx, attend, p, scan_cfg):
    wkv_out, shift_out, conv_out = [], [], []
    v_first = None
    ctx = None
    kv_state = None
    for layer in range(DEPTH):
        if layer == N_A_LAYERS:
            ctx, kv_state = make_ctx(x)
        if layer < N_A_LAYERS:
            x, s_new, sh_new, v_first = rwkv7_layer(x, shift_in[layer], wkv_in[layer], v_first, layer, p, scan_cfg)
            wkv_out.append(s_new)
            shift_out.append(sh_new)
        else:
            lb = layer - N_A_LAYERS
            q, gates = nsa_query(x, p['norm_g'][layer, 0], lb, p)
            x = residual_mm(attend(q, gates, ctx), x, p['nsa_w_o'][lb])
        ffn_w = (p['ffn_w_in'][layer], p['ffn_conv_w'][layer], p['ffn_conv_b'][layer], p['ffn_w_out'][layer])
        if x.shape[1] % FFN_TM == 0:
            x, conv_new = conv_ffn_fused(x, p['norm_g'][layer, 1], conv_in[layer], *ffn_w)
        else:
            f, conv_new = conv_ffn(rmsnorm(x, p['norm_g'][layer, 1]), conv_in[layer], *ffn_w)
            x = x + f
        conv_out.append(conv_new)
    return x, kv_state, jnp.stack(wkv_out), jnp.stack(shift_out), jnp.stack(conv_out)


def kernel(x_prompt, x_sample, cache_kv, cache_kv_win, state_wkv, state_shift, state_ffn_conv, page_table, norm_g, rw_mu, rw_w_r, rw_w_k, rw_w_v, rw_w_o, rw_w0, rw_w1, rw_w2, rw_a0, rw_a1, rw_a2, rw_v0, rw_v1, rw_v2, rw_g1, rw_g2, rw_k_k, rw_k_a, rw_r_k, rw_ln_w, rw_ln_b, kv_norm_g, w_kv, k_norm_g, cmp_pe, cmp_w1, cmp_b1, cmp_w2, cmp_b2, nsa_w_q, q_norm_g, nsa_w_o, ffn_w_in, ffn_conv_w, ffn_conv_b, ffn_w_out):
    p = dict(norm_g=norm_g, rw_mu=rw_mu, rw_w_r=rw_w_r, rw_w_k=rw_w_k, rw_w_v=rw_w_v, rw_w_o=rw_w_o,
             rw_w0=rw_w0, rw_w1=rw_w1, rw_w2=rw_w2, rw_a0=rw_a0, rw_a1=rw_a1, rw_a2=rw_a2,
             rw_v0=rw_v0, rw_v1=rw_v1, rw_v2=rw_v2, rw_g1=rw_g1, rw_g2=rw_g2, rw_k_k=rw_k_k, rw_k_a=rw_k_a,
             rw_r_k=rw_r_k, rw_ln_w=rw_ln_w, rw_ln_b=rw_ln_b, kv_norm_g=kv_norm_g, w_kv=w_kv,
             k_norm_g=k_norm_g, cmp_pe=cmp_pe, cmp_w1=cmp_w1, cmp_b1=cmp_b1, cmp_w2=cmp_w2, cmp_b2=cmp_b2,
             nsa_w_q=nsa_w_q, q_norm_g=q_norm_g, nsa_w_o=nsa_w_o, ffn_w_in=ffn_w_in,
             ffn_conv_w=ffn_conv_w, ffn_conv_b=ffn_conv_b, ffn_w_out=ffn_w_out)
    bp, t_p = x_prompt.shape[:2]
    bs = x_sample.shape[0]

    wkv0 = jnp.zeros((N_A_LAYERS, bp, RW_HEADS, RW_HEAD_DIM, RW_HEAD_DIM), f32)
    shift0 = jnp.zeros((N_A_LAYERS, bp, D_MODEL), f32)
    conv0 = jnp.zeros((DEPTH, bp, CONV_W - 1, D_FF), f32)
    y_prompt, (kv_prompt, win_prompt), wkv_prompt, shift_prompt, conv_prompt = forward(
        x_prompt, wkv0, shift0, conv0, lambda h: prompt_context(h, p), attend_prompt, p,
        dict(bb=bp, tc=128))

    cache3 = cache_kv.reshape(cache_kv.shape[0], PAGE_SIZE, 4 * G * DH)
    y_sample, (kv_sample, win_sample), wkv_sample, shift_sample, conv_sample = forward(
        x_sample, state_wkv, state_shift, state_ffn_conv,
        lambda h: sample_context(h, cache3, page_table, cache_kv_win, p),
        lambda q, gates, ctx: attend_sample(q, gates, ctx, cache3, page_table), p,
        dict(bb=2, tc=x_sample.shape[1]))

    return (y_prompt, y_sample, kv_prompt, kv_sample, win_prompt, win_sample,
            wkv_prompt, wkv_sample, shift_prompt, shift_sample, conv_prompt, conv_sample)
```

```python
import functools

import numpy as np
import jax
import jax.numpy as jnp
from jax import lax
from jax.experimental import pallas as pl
from jax.experimental.pallas import tpu as pltpu

f32 = jnp.float32
bf16 = jnp.bfloat16

D_MODEL = 1024
DEPTH = 4
N_A_LAYERS = 2
PAST_LEN = 2048
PAGE_SIZE = 128
N_PAGES = PAST_LEN // PAGE_SIZE
RW_HEAD_DIM = 64
RW_HEADS = 16
RW_GN_EPS = 64e-5
DH = 64
NSA_HEADS = 16
G = 4
HPG = 4
CMP_LEN = 32
CMP_STRIDE = 16
CMP_HID = 128
SEL_BLOCK = 64
N_SELECT = 16
WINDOW = 512
Q_BLOCK = 128
D_FF = 2816
CONV_W = 3
NORM_EPS = 1e-6
NEG_INF = -1e30
FORCE_SCORE = 1e9
MASK_SCORE = -1e9
PICKED = -3e38
LOG2E = 1.4426950408889634

LANES = 128
SUBLANES = 8
VMEM_LIMIT = 48 * 1024 * 1024


def _nt(a, b):
    return lax.dot_general(a, b, (((1,), (1,)), ((), ())), preferred_element_type=f32)


def _split_bf16(x):
    hi = x.astype(bf16)
    lo = (x - hi.astype(f32)).astype(bf16)
    return hi, lo


def _mm_kernel(x_ref, w_ref, o_ref):
    o_ref[...] = jnp.dot(x_ref[...].astype(bf16), w_ref[...], preferred_element_type=f32)


MM_TM = 512


def _pick_tile(n, cands):
    for c in cands:
        if n % c == 0:
            return c
    return n


def mm(x, w):
    m, k = x.shape
    n = w.shape[1]
    w = w.astype(bf16)
    tm = MM_TM if m >= MM_TM else -(-m // SUBLANES) * SUBLANES
    mp = -(-m // tm) * tm
    if mp != m:
        x = jnp.pad(x, ((0, mp - m), (0, 0)))
    tn = n if n <= 1536 else _pick_tile(n, (1408, 1024, 512, 256, 128))
    out = pl.pallas_call(
        _mm_kernel,
        grid=(mp // tm, n // tn),
        in_specs=[pl.BlockSpec((tm, k), lambda i, j: (i, 0)),
                  pl.BlockSpec((k, tn), lambda i, j: (0, j))],
        out_specs=pl.BlockSpec((tm, tn), lambda i, j: (i, j)),
        out_shape=jax.ShapeDtypeStruct((mp, n), f32),
        compiler_params=pltpu.CompilerParams(dimension_semantics=("parallel", "parallel"),
                                             vmem_limit_bytes=VMEM_LIMIT),
        name="mm",
    )(x, w)
    return out[:m] if mp != m else out


def mm3(x, w):
    b, t, k = x.shape
    return mm(x.reshape(b * t, k), w).reshape(b, t, w.shape[1])


PAIR_W = 2 * RW_HEAD_DIM
N_PAIRS = RW_HEADS // 2


def _scan_kernel(r_ref, w_ref, k_ref, v_ref, a_ref, b_ref, s0_ref, y_ref, s_ref, *, bb, tc):
    n = RW_HEAD_DIM
    npairs = bb * N_PAIRS

    @pl.when(pl.program_id(1) == 0)
    def _():
        s_ref[...] = s0_ref[...]

    row = lax.broadcasted_iota(jnp.int32, (n, PAIR_W), 0)
    lane = lax.broadcasted_iota(jnp.int32, (n, PAIR_W), 1)
    diag = (lane % n) == row
    kk = lax.broadcasted_iota(jnp.int32, (PAIR_W, PAIR_W), 0) // n
    ll = lax.broadcasted_iota(jnp.int32, (PAIR_W, PAIR_W), 1) // n
    bones = (kk == ll).astype(bf16)
    group = min(tc, SUBLANES)

    def seg_sum_bcast(x):
        return jnp.dot(x.astype(bf16), bones, preferred_element_type=f32)

    def token(t0, s):
        def rowvec(ref, bi, hp):
            tile = ref[bi, pl.ds(t0, group), pl.ds(hp * PAIR_W, PAIR_W)]
            return jnp.broadcast_to(tile[s:s + 1], (n, PAIR_W))

        pairs = [(bi, hp) for bi in range(bb) for hp in range(N_PAIRS)]
        st = [s_ref[p] for p in range(npairs)]
        sa_in = jnp.concatenate([st[p] * rowvec(a_ref, *pairs[p]) for p in range(npairs)], axis=0)
        v_in = jnp.concatenate([jnp.where(diag, rowvec(v_ref, *pairs[p]), 0.0) for p in range(npairs)], axis=0)
        sa_col = seg_sum_bcast(sa_in)
        v_col = seg_sum_bcast(v_in)
        new = []
        for p in range(npairs):
            sl = slice(p * n, (p + 1) * n)
            sn = (st[p] * rowvec(w_ref, *pairs[p]) + sa_col[sl] * rowvec(b_ref, *pairs[p])
                  + v_col[sl] * rowvec(k_ref, *pairs[p]))
            s_ref[p] = sn
            new.append(sn)
        y_in = jnp.concatenate([new[p] * rowvec(r_ref, *pairs[p]) for p in range(npairs)], axis=0)
        y_col = seg_sum_bcast(y_in)
        rows = []
        for p in range(npairs):
            sl = slice(p * n, (p + 1) * n)
            rows.append(jnp.sum(jnp.where(diag, y_col[sl], 0.0), axis=0, keepdims=True))
        return rows

    def block(g, carry):
        t0 = pl.multiple_of(g * group, group)
        rows = [token(t0, s) for s in range(group)]
        for p in range(npairs):
            bi, hp = divmod(p, N_PAIRS)
            y_ref[bi, pl.ds(t0, group), pl.ds(hp * PAIR_W, PAIR_W)] = jnp.concatenate(
                [rows[s][p] for s in range(group)], axis=0)
        return carry

    lax.fori_loop(0, tc // group, block, 0)


def wkv_scan(r, w, k, v, a, b, s0_pairs, *, bb, tc):
    bsz, t_len, _ = r.shape
    assert bsz % bb == 0 and t_len % tc == 0
    vec_spec = pl.BlockSpec((bb, tc, D_MODEL), lambda i, j: (i, j, 0))
    st_spec = pl.BlockSpec((bb * N_PAIRS, RW_HEAD_DIM, PAIR_W), lambda i, j: (i, 0, 0))
    return pl.pallas_call(
        functools.partial(_scan_kernel, bb=bb, tc=tc),
        grid=(bsz // bb, t_len // tc),
        in_specs=[vec_spec] * 6 + [st_spec],
        out_specs=[vec_spec, st_spec],
        out_shape=[jax.ShapeDtypeStruct((bsz, t_len, D_MODEL), f32),
                   jax.ShapeDtypeStruct((bsz * N_PAIRS, RW_HEAD_DIM, PAIR_W), f32)],
        compiler_params=pltpu.CompilerParams(dimension_semantics=("arbitrary", "arbitrary"),
                                             vmem_limit_bytes=VMEM_LIMIT),
        name="wkv_scan",
    )(r, w, k, v, a, b, s0_pairs)


def to_pairs(s):
    bsz = s.shape[0]
    return s.reshape(bsz, N_PAIRS, 2, RW_HEAD_DIM, RW_HEAD_DIM).transpose(0, 1, 3, 2, 4).reshape(
        bsz * N_PAIRS, RW_HEAD_DIM, PAIR_W)


def from_pairs(s, bsz):
    return s.reshape(bsz, N_PAIRS, RW_HEAD_DIM, 2, RW_HEAD_DIM).transpose(0, 1, 3, 2, 4).reshape(
        bsz, RW_HEADS, RW_HEAD_DIM, RW_HEAD_DIM)


def _masked_softmax(s, mask):
    s = jnp.where(mask, s, NEG_INF)
    e = jnp.where(mask, jnp.exp2(s - jnp.max(s, axis=-1, keepdims=True)), 0.0)
    return e / jnp.maximum(jnp.sum(e, axis=-1, keepdims=True), 1e-30)


def _topk_rounds(score, idx, axis):
    sel = jnp.zeros(score.shape, f32)
    idx = idx.astype(f32)
    for _ in range(N_SELECT):
        m = jnp.max(score, axis=axis, keepdims=True)
        first = jnp.min(jnp.where(score == m, idx, 1e6), axis=axis, keepdims=True)
        pick = idx == first
        sel = jnp.where(pick, 1.0, sel)
        score = jnp.where(pick, PICKED, score)
    return sel > 0.5


def _topk_by_rank(score, idx, n_valid):
    rank = jnp.zeros(score.shape, f32)
    for i in range(n_valid):
        ci = score[:, i:i + 1]
        beats = (ci > score) | ((ci == score) & (idx > i))
        rank = rank + jnp.where(beats, 1.0, 0.0)
    return rank < (N_SELECT - 0.5)


def _imp_map_t(n_cmp_pad, n_blk_pad, n_cmp, n_blk):
    ci = np.arange(n_cmp_pad)[None, :] * CMP_STRIDE
    sj = np.arange(n_blk_pad)[:, None] * SEL_BLOCK
    ov = np.clip(np.minimum(ci + CMP_LEN, sj + SEL_BLOCK) - np.maximum(ci, sj), 0, None) / CMP_STRIDE
    ov = ov * (np.arange(n_cmp_pad)[None, :] < n_cmp) * (np.arange(n_blk_pad)[:, None] < n_blk)
    return jnp.asarray(ov, bf16)


NC_PAD = 512
NB_PROMPT = 128
KEXT = NB_PROMPT + DH
SEL_KT = 4 * Q_BLOCK
QROWS = HPG * Q_BLOCK


def _prompt_attn_kernel(q_ref, gt_ref, ck_ref, cvt_ref, impt_ref, ke_ref, vst_ref, kw_ref, vwt_ref, o_ref,
                        s_scr, p_scr, *, n_cmp):
    qi = pl.program_id(2)
    rows = QROWS
    q0 = qi * Q_BLOCK
    q = jnp.concatenate([q_ref[0, :, h * DH:(h + 1) * DH] for h in range(HPG)], axis=0)

    def q_pos(n_keys):
        return q0 + (lax.broadcasted_iota(jnp.int32, (n_keys, rows), 1) & (Q_BLOCK - 1))

    def softmax0(s, ok):
        e = jnp.where(ok, jnp.exp2(s - jnp.max(s, axis=0, keepdims=True)), 0.0)
        return e, jnp.maximum(jnp.sum(e, axis=0, keepdims=True), 1e-30)

    nn = lax.broadcasted_iota(jnp.int32, (NC_PAD, rows), 0)
    ok = (nn * CMP_STRIDE + (CMP_LEN - 1) <= q_pos(NC_PAD)) & (nn < n_cmp)
    e, l_c = softmax0(jnp.where(ok, _nt(ck_ref[0, 0], q), NEG_INF), ok)
    p_c = e / l_c
    o_c = jnp.dot(cvt_ref[0, 0], p_c.astype(bf16), preferred_element_type=f32).T
    psum = p_c[:, 0:Q_BLOCK]
    for h in range(1, HPG):
        psum = psum + p_c[:, h * Q_BLOCK:(h + 1) * Q_BLOCK]
    hi, lo = _split_bf16(psum)
    imp_t = (jnp.dot(impt_ref[...], hi, preferred_element_type=f32)
             + jnp.dot(impt_ref[...], lo, preferred_element_type=f32))
    blk = lax.broadcasted_iota(jnp.int32, (NB_PROMPT, Q_BLOCK), 0)
    qp = q0 + lax.broadcasted_iota(jnp.int32, (NB_PROMPT, Q_BLOCK), 1)
    qb = qp >> 6
    forced = (blk == 0) | (blk == qb) | (blk == qb - 1)
    causal = blk * SEL_BLOCK <= qp
    score = jnp.where(causal, jnp.where(forced, FORCE_SCORE, imp_t), MASK_SCORE)
    sel = _topk_rounds(score, blk, 0) & causal
    bias = jnp.where(sel, 0.0, MASK_SCORE).T.astype(bf16)
    qe = jnp.concatenate([jnp.concatenate([bias] * HPG, axis=0), q], axis=1)

    n_wk = WINDOW + Q_BLOCK
    w0 = pl.multiple_of(jnp.maximum(q0 - WINDOW, 0), Q_BLOCK)
    dist = q_pos(n_wk) - (w0 + lax.broadcasted_iota(jnp.int32, (n_wk, rows), 0))
    ok = (dist >= 0) & (dist < WINDOW)
    e, l_w = softmax0(jnp.where(ok, _nt(kw_ref[0, 0, pl.ds(w0, n_wk), :], q), NEG_INF), ok)
    o_w = (jnp.dot(vwt_ref[0, 0, :, pl.ds(w0, n_wk)], e.astype(bf16), preferred_element_type=f32) / l_w).T

    def sel_scores(off):
        return _nt(ke_ref[0, 0, pl.ds(off, SEL_KT), :], qe)

    def pv(j, p):
        off = pl.multiple_of(jnp.maximum(j, 0) * SEL_KT, SEL_KT)
        return jnp.dot(vst_ref[0, 0, :, pl.ds(off, SEL_KT)], p, preferred_element_type=f32)

    def softmax_step(s, m, l):
        m_new = jnp.maximum(m, jnp.max(s, axis=0, keepdims=True))
        alpha = jnp.exp2(m - m_new)
        p = jnp.exp2(s - m_new)
        return m_new, alpha * l + jnp.sum(p, axis=0, keepdims=True), alpha, p.astype(bf16)

    n_full = q0 // SEL_KT
    s_scr[...] = sel_scores(0)
    p_scr[...] = jnp.zeros((SEL_KT, rows), bf16)

    def sel_body(j, carry):
        m, l, acc, alpha_prev = carry
        s_next = sel_scores(pl.multiple_of((j + 1) * SEL_KT, SEL_KT))
        acc = alpha_prev * acc + pv(j - 1, p_scr[...])
        m, l, alpha, p = softmax_step(s_scr[...], m, l)
        p_scr[...] = p
        s_scr[...] = s_next
        return m, l, acc, alpha

    init = (jnp.full((1, rows), NEG_INF, f32), jnp.zeros((1, rows), f32), jnp.zeros((DH, rows), f32),
            jnp.ones((1, rows), f32))
    m, l, acc, alpha_prev = lax.fori_loop(0, n_full, sel_body, init)
    acc = alpha_prev * acc + pv(n_full - 1, p_scr[...])
    off = pl.multiple_of(n_full * SEL_KT, SEL_KT)
    key_pos = off + lax.broadcasted_iota(jnp.int32, (SEL_KT, rows), 0)
    s = jnp.where(key_pos <= q_pos(SEL_KT), s_scr[...], 2.0 * MASK_SCORE)
    _, l_s, alpha, p = softmax_step(s, m, l)
    acc = alpha * acc + jnp.dot(vst_ref[0, 0, :, pl.ds(off, SEL_KT)], p, preferred_element_type=f32)
    o_s = (acc / jnp.maximum(l_s, 1e-30)).T

    gt = gt_ref[0, 0]
    gate = lambda c: jnp.concatenate([gt[:, 3 * h + c:3 * h + c + 1] for h in range(HPG)], axis=0)
    o = gate(0) * o_c + gate(1) * o_s + gate(2) * o_w
    for h in range(HPG):
        o_ref[0, :, h * DH:(h + 1) * DH] = o[h * Q_BLOCK:(h + 1) * Q_BLOCK]


def nsa_prompt_attention(q, gt, ck, cvt, ke, vst, kw, vwt, n_cmp):
    bsz, t_len, _ = q.shape
    n_q = t_len // Q_BLOCK
    assert t_len % SEL_KT == 0 and t_len >= WINDOW + Q_BLOCK
    impt = _imp_map_t(NC_PAD, NB_PROMPT, n_cmp, NB_PROMPT)
    tile = pl.BlockSpec((1, Q_BLOCK, HPG * DH), lambda b, g, i: (b, i, g))
    per_bg = lambda r, w: pl.BlockSpec((1, 1, r, w), lambda b, g, i: (b, g, 0, 0))
    return pl.pallas_call(
        functools.partial(_prompt_attn_kernel, n_cmp=n_cmp),
        grid=(bsz, G, n_q),
        in_specs=[tile, pl.BlockSpec((1, 1, Q_BLOCK, 3 * HPG), lambda b, g, i: (b, g, i, 0)),
                  per_bg(NC_PAD, DH), per_bg(DH, NC_PAD),
                  pl.BlockSpec((NB_PROMPT, NC_PAD), lambda b, g, i: (0, 0)),
                  per_bg(t_len, KEXT), per_bg(DH, t_len), per_bg(t_len, DH), per_bg(DH, t_len)],
        out_specs=tile,
        out_shape=jax.ShapeDtypeStruct((bsz, t_len, G * HPG * DH), f32),
        scratch_shapes=[pltpu.VMEM((SEL_KT, QROWS), f32), pltpu.VMEM((SEL_KT, QROWS), bf16)],
        compiler_params=pltpu.CompilerParams(dimension_semantics=("parallel", "parallel", "arbitrary"),
                                             vmem_limit_bytes=VMEM_LIMIT),
        name="nsa_prompt_attention",
    )(q, gt, ck, cvt, impt, ke, vst, kw, vwt)


N_SUB = PAST_LEN // CMP_STRIDE
HALF = 2 * G * DH


def _gelu(x):
    return 0.5 * x * (1.0 + jnp.tanh(0.7978845608028654 * (x + 0.044715 * (x * x * x))))


def _sample_cmp_kernel(pt_ref, *refs):
    pages = refs[:N_PAGES]
    pe_ref, w1_ref, b1_ref, w2_ref, b2_ref, kg_ref, ck_ref, cv_ref, x_scr = refs[N_PAGES:]
    n_chunk = HALF // LANES
    for c in range(n_chunk):
        for j in range(N_PAGES):
            x_scr[c, pl.ds(j * PAGE_SIZE, PAGE_SIZE), :] = pages[j][0, :, pl.ds(c * LANES, LANES)]
        x_scr[c, pl.ds(PAST_LEN, CMP_STRIDE), :] = jnp.zeros((CMP_STRIDE, LANES), f32)
    low = lax.broadcasted_iota(jnp.int32, ((G // 2) * N_SUB, LANES), 1) < DH
    for part in range(2):
        acc = jnp.zeros(((G // 2) * N_SUB, 2 * CMP_HID), f32)
        for s in range(CMP_LEN):
            x = jnp.concatenate([x_scr[part * (G // 2) + c, pl.ds(s, N_SUB, stride=CMP_STRIDE), :]
                                 for c in range(G // 2)], axis=0)
            x = x + pe_ref[part, s:s + 1, :]
            acc = acc + jnp.dot(x.astype(bf16), w1_ref[part, s], preferred_element_type=f32)
        hid = _gelu(acc + b1_ref[part:part + 1, :])
        out = jnp.dot(hid.astype(bf16), w2_ref[part], preferred_element_type=f32) + b2_ref[part:part + 1, :]
        if part == 0:
            sq = out * out
            ms = jnp.where(low, jnp.sum(jnp.where(low, sq, 0.0), axis=-1, keepdims=True),
                           jnp.sum(jnp.where(low, 0.0, sq), axis=-1, keepdims=True)) * (1.0 / DH)
            out = out * lax.rsqrt(ms + NORM_EPS) * kg_ref[...]
        out_ref = ck_ref if part == 0 else cv_ref
        for c in range(G // 2):
            out_ref[0, c] = out[c * N_SUB:(c + 1) * N_SUB].astype(bf16)


def sample_compress(cache3, page_table, pe, w1, b1, w2, b2, kg):
    bsz = page_table.shape[0]
    twice = lambda a: jnp.concatenate([a, a], axis=-1)
    w1 = w1.reshape(2, CMP_LEN, DH, CMP_HID)
    z1, z2 = jnp.zeros_like(w1), jnp.zeros_like(w2)
    w1d = jnp.concatenate([jnp.concatenate([w1, z1], -1), jnp.concatenate([z1, w1], -1)], axis=-2).astype(bf16)
    w2d = jnp.concatenate([jnp.concatenate([w2, z2], -1), jnp.concatenate([z2, w2], -1)], axis=-2).astype(bf16)
    pe, b1, b2, kg = twice(pe), twice(b1), twice(b2), twice(kg)

    def page_spec(j):
        return pl.BlockSpec((1, PAGE_SIZE, HALF), lambda b, pt: (pt[b, j], 0, 0))

    const = lambda a: pl.BlockSpec(a.shape, lambda b, pt: (0,) * a.ndim)
    out_spec = pl.BlockSpec((1, G // 2, N_SUB, LANES), lambda b, pt: (b, 0, 0, 0))
    grid_spec = pltpu.PrefetchScalarGridSpec(
        num_scalar_prefetch=1, grid=(bsz,),
        in_specs=[page_spec(j) for j in range(N_PAGES)] + [const(a) for a in (pe, w1d, b1, w2d, b2, kg)],
        out_specs=[out_spec, out_spec],
        scratch_shapes=[pltpu.VMEM((HALF // LANES, PAST_LEN + CMP_STRIDE, LANES), f32)])
    return pl.pallas_call(
        _sample_cmp_kernel, grid_spec=grid_spec,
        out_shape=[jax.ShapeDtypeStruct((bsz, G // 2, N_SUB, LANES), bf16)] * 2,
        compiler_params=pltpu.CompilerParams(dimension_semantics=("arbitrary",), vmem_limit_bytes=VMEM_LIMIT),
        name="sample_compress",
    )(page_table, *([cache3] * N_PAGES), pe, w1d, b1, w2d, b2, kg)


T_NEW = 4
NEW_PAD = 128
NB_SAMPLE = -(-(PAST_LEN + T_NEW) // SEL_BLOCK)
N_CMP_SAMPLE = (PAST_LEN + T_NEW) // CMP_STRIDE - 1
ROWS_S = HPG * T_NEW


def _sample_attn_kernel(pt_ref, *refs):
    pages = refs[:N_PAGES]
    (q_ref, gt_ref, ck_ref, cv_ref, new_ref, win_ref, wnew_ref, impt_ref, oh_ref, ohn_ref, o_ref) = refs[N_PAGES:]
    rows = 2 * ROWS_S
    t_of_row = lax.broadcasted_iota(jnp.int32, (rows, LANES), 0) & (T_NEW - 1)
    lane = lax.broadcasted_iota(jnp.int32, (rows, LANES), 1)
    q_pos = PAST_LEN + t_of_row
    hs_o = lax.broadcasted_iota(jnp.int32, (SUBLANES, rows), 0)
    hs_r = lax.broadcasted_iota(jnp.int32, (SUBLANES, rows), 1)
    hsum = ((hs_o >> 2 == hs_r >> 4) & ((hs_o & 3) == (hs_r & 3))).astype(bf16)
    sp_r = lax.broadcasted_iota(jnp.int32, (rows, SUBLANES), 0)
    sp_o = lax.broadcasted_iota(jnp.int32, (rows, SUBLANES), 1)
    spread = ((sp_o >> 2 == sp_r >> 4) & ((sp_o & 3) == (sp_r & 3))).astype(bf16)
    blk8 = lax.broadcasted_iota(jnp.int32, (SUBLANES, LANES), 1)
    qp8 = PAST_LEN + (lax.broadcasted_iota(jnp.int32, (SUBLANES, LANES), 0) & (T_NEW - 1))
    qb8 = qp8 >> 6
    forced = (blk8 == 0) | (blk8 == qb8) | (blk8 == qb8 - 1)
    causal = (blk8 * SEL_BLOCK <= qp8) & (blk8 < NB_SAMPLE)
    n_buf = win_ref.shape[1]

    for c in range(G // 2):
        lanes_k = pl.ds(c * LANES, LANES)
        lanes_v = pl.ds(G * DH + c * LANES, LANES)
        q = q_ref[0, c]
        gt = gt_ref[0, c]
        s_c = _nt(q, ck_ref[0, c])
        ok_c = (lane * CMP_STRIDE + (CMP_LEN - 1) <= q_pos) & (lane < N_CMP_SAMPLE)
        p_c = _masked_softmax(s_c, ok_c)
        o_c = jnp.dot(p_c.astype(bf16), cv_ref[0, c], preferred_element_type=f32)
        hi, lo = _split_bf16(p_c)
        psum = jnp.dot(hsum, hi, preferred_element_type=f32) + jnp.dot(hsum, lo, preferred_element_type=f32)
        hi, lo = _split_bf16(psum)
        imp = _nt(hi, impt_ref[...]) + _nt(lo, impt_ref[...])
        score = jnp.where(blk8 < NB_SAMPLE,
                          jnp.where(causal, jnp.where(forced, FORCE_SCORE, imp), MASK_SCORE), PICKED)
        sel = _topk_by_rank(score, blk8, NB_SAMPLE) & causal
        bias8 = jnp.where(sel, 0.0, MASK_SCORE).astype(bf16)
        bias = jnp.dot(spread, bias8, preferred_element_type=f32).astype(bf16)
        k_all = jnp.concatenate([pages[j][0, :, lanes_k] for j in range(N_PAGES)], axis=0).astype(bf16)
        s_past = _nt(q, k_all) + jnp.dot(bias, oh_ref[...], preferred_element_type=f32)
        s_new = _nt(q, new_ref[0, :, lanes_k].astype(bf16)) + jnp.dot(bias, ohn_ref[...],
                                                                      preferred_element_type=f32)
        s_new = jnp.where(lane <= t_of_row, s_new, 2.0 * MASK_SCORE)
        s_all = jnp.concatenate([s_past, s_new], axis=1)
        e = jnp.exp2(s_all - jnp.max(s_all, axis=1, keepdims=True))
        den = jnp.maximum(jnp.sum(e, axis=1, keepdims=True), 1e-30)
        v_all = jnp.concatenate([pages[j][0, :, lanes_v] for j in range(N_PAGES)] + [new_ref[0, :, lanes_v]],
                                axis=0).astype(bf16)
        o_s = jnp.dot(e.astype(bf16), v_all, preferred_element_type=f32) / den
        kw_all = jnp.concatenate([win_ref[0, :, lanes_k], wnew_ref[0, :, lanes_k]], axis=0).astype(bf16)
        idx = lax.broadcasted_iota(jnp.int32, (rows, n_buf), 1)
        t_w = lax.broadcasted_iota(jnp.int32, (rows, n_buf), 0) & (T_NEW - 1)
        w_pos = PAST_LEN - n_buf + idx
        dist = PAST_LEN + t_w - w_pos
        ok_w = jnp.concatenate([(dist >= 0) & (dist < WINDOW) & (w_pos >= 0), lane <= t_of_row], axis=1)
        p_w = _masked_softmax(_nt(q, kw_all), ok_w).astype(bf16)
        vw_all = jnp.concatenate([win_ref[0, :, lanes_v], wnew_ref[0, :, lanes_v]], axis=0).astype(bf16)
        o_w = jnp.dot(p_w, vw_all, preferred_element_type=f32)
        o_ref[0, c] = gt[:, 0:1] * o_c + gt[:, 1:2] * o_s + gt[:, 2:3] * o_w


def sample_attention(cache3, page_table, q2, gt2, ck2, cv2, new_sel, win_cache, new_win):
    bsz = page_table.shape[0]
    n_buf = win_cache.shape[1]
    rows = 2 * ROWS_S
    impt = _imp_map_t(N_SUB, LANES, N_CMP_SAMPLE, NB_SAMPLE)
    key_blk = np.arange(PAST_LEN) // SEL_BLOCK
    onehot = jnp.asarray(key_blk[None, :] == np.arange(LANES)[:, None], bf16)
    onehot_new = jnp.asarray(np.broadcast_to(np.arange(LANES)[:, None] == PAST_LEN // SEL_BLOCK,
                                             (LANES, NEW_PAD)), bf16)

    def page_spec(j):
        return pl.BlockSpec((1, PAGE_SIZE, HALF), lambda b, pt: (pt[b, j], 0, 1))

    per_seq = lambda shape: pl.BlockSpec((1,) + shape, lambda b, pt: (b,) + (0,) * len(shape))
    const = lambda shape: pl.BlockSpec(shape, lambda b, pt: (0,) * len(shape))
    grid_spec = pltpu.PrefetchScalarGridSpec(
        num_scalar_prefetch=1, grid=(bsz,),
        in_specs=[page_spec(j) for j in range(N_PAGES)] + [
            per_seq((G // 2, rows, LANES)), per_seq((G // 2, rows, 3)),
            per_seq((G // 2, N_SUB, LANES)), per_seq((G // 2, N_SUB, LANES)),
            per_seq((NEW_PAD, HALF)), per_seq((n_buf, HALF)), per_seq((NEW_PAD, HALF)),
            const((LANES, N_SUB)), const((LANES, PAST_LEN)), const((LANES, NEW_PAD))],
        out_specs=per_seq((G // 2, rows, LANES)))
    return pl.pallas_call(
        _sample_attn_kernel, grid_spec=grid_spec,
        out_shape=jax.ShapeDtypeStruct((bsz, G // 2, rows, LANES), f32),
        compiler_params=pltpu.CompilerParams(dimension_semantics=("arbitrary",), vmem_limit_bytes=VMEM_LIMIT),
        name="sample_attention",
    )(page_table, *([cache3] * N_PAGES), q2, gt2, ck2, cv2, new_sel, win_cache, new_win, impt, onehot, onehot_new)


def rmsnorm(x, g):
    y = x * lax.rsqrt(jnp.mean(x * x, -1, keepdims=True) + NORM_EPS)
    return y * g


def _ffn_in_kernel(x_ref, g_ref, w_ref, o_ref):
    x = x_ref[...]
    xn = x * lax.rsqrt(jnp.mean(x * x, axis=-1, keepdims=True) + NORM_EPS) * g_ref[...]
    o_ref[...] = jnp.dot(xn.astype(bf16), w_ref[...], preferred_element_type=f32)


def _ffn_out_kernel(x_ref, h_ref, halo_ref, cs_ref, cw_ref, cb_ref, w_ref, o_ref, u_scr, *, tiles_per_seq):
    first = pl.program_id(0) % tiles_per_seq == 0
    tm = h_ref.shape[0]
    u_scr[pl.ds(0, SUBLANES), :] = halo_ref[...]
    u_scr[pl.ds(SUBLANES, tm), :] = h_ref[:, pl.ds(0, D_FF)]

    @pl.when(first)
    def _():
        u_scr[pl.ds(SUBLANES - (CONV_W - 1), CONV_W - 1), :] = cs_ref[0]

    uc = cb_ref[...] + cw_ref[CONV_W - 1:CONV_W, :] * u_scr[pl.ds(SUBLANES, tm), :]
    for j in range(CONV_W - 1):
        uc = uc + cw_ref[j:j + 1, :] * u_scr[pl.ds(SUBLANES - (CONV_W - 1) + j, tm), :]
    act = _gelu(uc) * h_ref[:, pl.ds(D_FF, D_FF)]
    o_ref[...] = x_ref[...] + jnp.dot(act.astype(bf16), w_ref[...], preferred_element_type=f32)


FFN_TM = 1024
FFN_TN = 1408
FFN_OUT_TM = 256


def norm_mm(x, g, w, tn):
    m, n = x.shape[0], w.shape[1]
    tm = FFN_TM if m % FFN_TM == 0 else m
    return pl.pallas_call(
        _ffn_in_kernel,
        grid=(m // tm, n // tn),
        in_specs=[pl.BlockSpec((tm, D_MODEL), lambda i, j: (i, 0)),
                  pl.BlockSpec((1, D_MODEL), lambda i, j: (0, 0)),
                  pl.BlockSpec((D_MODEL, tn), lambda i, j: (0, j))],
        out_specs=pl.BlockSpec((tm, tn), lambda i, j: (i, j)),
        out_shape=jax.ShapeDtypeStruct((m, n), f32),
        compiler_params=pltpu.CompilerParams(dimension_semantics=("parallel", "parallel"),
                                             vmem_limit_bytes=VMEM_LIMIT),
        name="norm_mm",
    )(x, g.reshape(1, D_MODEL), w.astype(bf16))


def conv_ffn_fused(x, g, conv_state, w_in, conv_w, conv_b, w_out):
    bsz, t_len, _ = x.shape
    m = bsz * t_len
    tiles_per_seq = t_len // FFN_OUT_TM
    x2 = x.reshape(m, D_MODEL)
    h = norm_mm(x2, g, w_in, FFN_TN)
    halo_blocks = FFN_OUT_TM // SUBLANES
    y = pl.pallas_call(
        functools.partial(_ffn_out_kernel, tiles_per_seq=tiles_per_seq),
        grid=(m // FFN_OUT_TM,),
        in_specs=[pl.BlockSpec((FFN_OUT_TM, D_MODEL), lambda i: (i, 0)),
                  pl.BlockSpec((FFN_OUT_TM, 2 * D_FF), lambda i: (i, 0)),
                  pl.BlockSpec((SUBLANES, D_FF), lambda i: (jnp.maximum(i * halo_blocks - 1, 0), 0)),
                  pl.BlockSpec((1, CONV_W - 1, D_FF), lambda i: (i // tiles_per_seq, 0, 0)),
                  pl.BlockSpec((CONV_W, D_FF), lambda i: (0, 0)),
                  pl.BlockSpec((1, D_FF), lambda i: (0, 0)),
                  pl.BlockSpec((D_FF, D_MODEL), lambda i: (0, 0))],
        out_specs=pl.BlockSpec((FFN_OUT_TM, D_MODEL), lambda i: (i, 0)),
        out_shape=jax.ShapeDtypeStruct((m, D_MODEL), f32),
        scratch_shapes=[pltpu.VMEM((SUBLANES + FFN_OUT_TM, D_FF), f32)],
        compiler_params=pltpu.CompilerParams(dimension_semantics=("arbitrary",), vmem_limit_bytes=56 * 1024 * 1024),
        name="ffn_out",
    )(x2, h, h, conv_state, conv_w, conv_b.reshape(1, D_FF), w_out.astype(bf16))
    conv_new = h.reshape(bsz, t_len, 2 * D_FF)[:, t_len - (CONV_W - 1):, :D_FF]
    return y.reshape(bsz, t_len, D_MODEL), conv_new


def conv_ffn(xn, conv_state, w_in, conv_w, conv_b, w_out):
    t_len = xn.shape[1]
    u, gate = jnp.split(mm3(xn, w_in), 2, axis=-1)
    u_pad = jnp.concatenate([conv_state, u], axis=1)
    uc = conv_b + conv_w[0] * u_pad[:, 0:t_len]
    for j in range(1, CONV_W):
        uc = uc + conv_w[j] * u_pad[:, j:j + t_len]
    return mm3(jax.nn.gelu(uc) * gate, w_out), u_pad[:, t_len:]


RW_TM = 256
SEG_W = LANES


def _head_seg():
    seg = np.arange(D_MODEL)[:, None] // RW_HEAD_DIM == np.arange(SEG_W)[None, :]
    return jnp.asarray(seg, bf16), jnp.asarray(seg.T, bf16)


def _head_sum(x, seg_ref):
    hi, lo = _split_bf16(x)
    return jnp.dot(hi, seg_ref[...], preferred_element_type=f32) + jnp.dot(lo, seg_ref[...],
                                                                           preferred_element_type=f32)


def _head_expand(s, segt_ref):
    hi, lo = _split_bf16(s)
    return jnp.dot(hi, segt_ref[...], preferred_element_type=f32) + jnp.dot(lo, segt_ref[...],
                                                                            preferred_element_type=f32)


def _rwkv_pre_kernel(*refs, has_vfirst):
    it = iter(refs)
    xn_ref, xs_ref = next(it), next(it)
    vf_ref = next(it) if has_vfirst else None
    mu_ref, wr, wk, wv, w1, w2, a1, a2 = (next(it) for _ in range(8))
    v1, v2 = (next(it), next(it)) if has_vfirst else (None, None)
    g1, g2, w0, a0 = (next(it) for _ in range(4))
    v0 = next(it) if has_vfirst else None
    kk_ref, ka_ref, seg_ref, segt_ref = (next(it) for _ in range(4))
    r_o, w_o, k_o, v_o, nkk_o, kka_o, g_o = (next(it) for _ in range(7))

    dot = lambda x, w: jnp.dot(x.astype(bf16), w[...], preferred_element_type=f32)
    xn = xn_ref[...]
    xx = xs_ref[...] - xn
    xr, xw, xk, xv, xa, xg = (xn + xx * mu_ref[i:i + 1, :] for i in range(6))
    r = dot(xr, wr)
    k = dot(xk, wk)
    v = dot(xv, wv)
    z = -(w0[...] + dot(jnp.tanh(dot(xw, w1)), w2))
    softplus = jnp.maximum(z, 0.0) + jnp.log1p(jnp.exp(-jnp.abs(z)))
    decay = jnp.exp(-jnp.exp(-softplus - 0.5))
    a = jax.nn.sigmoid(a0[...] + dot(dot(xa, a1), a2))
    if has_vfirst:
        v = v + (vf_ref[...] - v) * jax.nn.sigmoid(v0[...] + dot(dot(xv, v1), v2))
    g = dot(jax.nn.sigmoid(dot(xg, g1)), g2)
    kk = k * kk_ref[...]
    norm = jnp.maximum(jnp.sqrt(_head_sum(kk * kk, seg_ref)), 1e-12)
    kk = kk * _head_expand(1.0 / norm, segt_ref)
    r_o[...] = r
    w_o[...] = decay
    k_o[...] = k * (1.0 + (a - 1.0) * ka_ref[...])
    v_o[...] = v
    nkk_o[...] = -kk
    kka_o[...] = kk * a
    g_o[...] = g


def rwkv_pre(xn, xs, v_first, li, p):
    m = xn.shape[0]
    tm = RW_TM if m % RW_TM == 0 else m
    has_vfirst = v_first is not None
    seg, segt = _head_seg()
    row = lambda a: a.reshape(1, -1)
    cast = lambda w: w.astype(bf16)
    ops = [xn, xs] + ([v_first] if has_vfirst else [])
    ops += [p['rw_mu'][li], cast(p['rw_w_r'][li]), cast(p['rw_w_k'][li]), cast(p['rw_w_v'][li]),
            cast(p['rw_w1'][li]), cast(p['rw_w2'][li]), cast(p['rw_a1'][li]), cast(p['rw_a2'][li])]
    if has_vfirst:
        ops += [cast(p['rw_v1'][li - 1]), cast(p['rw_v2'][li - 1])]
    ops += [cast(p['rw_g1'][li]), cast(p['rw_g2'][li]), row(p['rw_w0'][li]), row(p['rw_a0'][li])]
    if has_vfirst:
        ops += [row(p['rw_v0'][li - 1])]
    ops += [row(p['rw_k_k'][li]), row(p['rw_k_a'][li]), seg, segt]
    n_rows = 3 if has_vfirst else 2
    tile = pl.BlockSpec((tm, D_MODEL), lambda i: (i, 0))
    whole = lambda a: pl.BlockSpec(a.shape, lambda i: (0,) * a.ndim)
    return pl.pallas_call(
        functools.partial(_rwkv_pre_kernel, has_vfirst=has_vfirst),
        grid=(m // tm,),
        in_specs=[tile] * n_rows + [whole(a) for a in ops[n_rows:]],
        out_specs=[tile] * 7,
        out_shape=[jax.ShapeDtypeStruct((m, D_MODEL), f32)] * 7,
        compiler_params=pltpu.CompilerParams(dimension_semantics=("parallel",), vmem_limit_bytes=VMEM_LIMIT),
        name="rwkv_pre",
    )(*ops)


def _rwkv_post_kernel(y_ref, r_ref, k_ref, v_ref, g_ref, x_ref, lnw, lnb, rk, wo, seg_ref, segt_ref, o_ref):
    y = y_ref[...]
    inv_n = 1.0 / RW_HEAD_DIM
    yc = y - _head_expand(_head_sum(y, seg_ref) * inv_n, segt_ref)
    var = _head_expand(_head_sum(yc * yc, seg_ref) * inv_n, segt_ref)
    yn = yc * lax.rsqrt(var + RW_GN_EPS) * lnw[...] + lnb[...]
    bonus = _head_expand(_head_sum(r_ref[...] * k_ref[...] * rk[...], seg_ref), segt_ref) * v_ref[...]
    out = ((yn + bonus) * g_ref[...]).astype(bf16)
    o_ref[...] = x_ref[...] + jnp.dot(out, wo[...], preferred_element_type=f32)


def rwkv_post(y, r, k, v, g, x, li, p):
    m = y.shape[0]
    tm = RW_TM if m % RW_TM == 0 else m
    seg, segt = _head_seg()
    row = lambda a: a.reshape(1, -1)
    consts = [row(p['rw_ln_w'][li]), row(p['rw_ln_b'][li]), row(p['rw_r_k'][li]), p['rw_w_o'][li].astype(bf16),
              seg, segt]
    tile = pl.BlockSpec((tm, D_MODEL), lambda i: (i, 0))
    whole = lambda a: pl.BlockSpec(a.shape, lambda i: (0,) * a.ndim)
    return pl.pallas_call(
        _rwkv_post_kernel,
        grid=(m // tm,),
        in_specs=[tile] * 6 + [whole(a) for a in consts],
        out_specs=tile,
        out_shape=jax.ShapeDtypeStruct((m, D_MODEL), f32),
        compiler_params=pltpu.CompilerParams(dimension_semantics=("parallel",), vmem_limit_bytes=VMEM_LIMIT),
        name="rwkv_post",
    )(y, r, k, v, g, x, *consts)


def rwkv7_layer(x, shift_prev, s0, v_first, li, p, scan_cfg):
    bsz, t_len, _ = x.shape
    m = bsz * t_len
    xn = rmsnorm(x, p['norm_g'][li, 0])
    xs = jnp.concatenate([shift_prev[:, None], xn[:, :-1]], axis=1)
    flat = lambda a: a.reshape(m, D_MODEL)
    r, decay, k, v, nkk, kka, g = rwkv_pre(flat(xn), flat(xs), None if li == 0 else flat(v_first), li, p)
    if li == 0:
        v_first = v
    seq = lambda a: a.reshape(bsz, t_len, D_MODEL)
    y, s_new = wkv_scan(seq(r), seq(decay), seq(k), seq(v), seq(nkk), seq(kka), to_pairs(s0), **scan_cfg)
    x_new = rwkv_post(flat(y), r, k, v, g, flat(x), li, p)
    return seq(x_new), from_pairs(s_new, bsz), xn[:, -1], seq(v_first)


def kv_rows(h, p):
    bsz, t_len, _ = h.shape
    kv = norm_mm(h.reshape(bsz * t_len, D_MODEL), p['kv_norm_g'], p['w_kv'], p['w_kv'].shape[1] // 2)
    kv = kv.reshape(bsz, t_len, 6, G, DH)
    k_sel = rmsnorm(kv[:, :, 2], p['k_norm_g'][1])
    k_win = rmsnorm(kv[:, :, 4], p['k_norm_g'][2])
    paged = jnp.stack([kv[:, :, 0], kv[:, :, 1], k_sel, kv[:, :, 3]], axis=2)
    win = jnp.stack([k_win, kv[:, :, 5]], axis=2)
    return paged, win


def _nsa_q_kernel(x_ref, g_ref, w_ref, qg_ref, seg_ref, segt_ref, q_ref, gt_ref):
    n_q = NSA_HEADS * DH
    x = x_ref[...]
    xn = x * lax.rsqrt(jnp.mean(x * x, axis=-1, keepdims=True) + NORM_EPS) * g_ref[...]
    proj = jnp.dot(xn.astype(bf16), w_ref[...], preferred_element_type=f32)
    q = proj[:, :n_q]
    inv = lax.rsqrt(_head_sum(q * q, seg_ref) * (1.0 / DH) + NORM_EPS)
    q_ref[...] = (q * _head_expand(inv, segt_ref) * qg_ref[...]).astype(bf16)
    gt_ref[...] = jax.nn.sigmoid(proj[:, n_q:])


def nsa_query(x, g, lb, p):
    bsz, t_len, _ = x.shape
    m = bsz * t_len
    n_q = NSA_HEADS * DH
    tm = RW_TM if m % RW_TM == 0 else m
    w = jnp.pad(p['nsa_w_q'][lb], ((0, 0), (0, LANES - 3 * NSA_HEADS))).astype(bf16)
    qg = (jnp.tile(p['q_norm_g'][lb], NSA_HEADS) * (DH ** -0.5 * LOG2E)).reshape(1, n_q)
    seg, segt = _head_seg()
    consts = [g.reshape(1, D_MODEL), w, qg, seg, segt]
    whole = lambda a: pl.BlockSpec(a.shape, lambda i: (0,) * a.ndim)
    q, gt = pl.pallas_call(
        _nsa_q_kernel,
        grid=(m // tm,),
        in_specs=[pl.BlockSpec((tm, D_MODEL), lambda i: (i, 0))] + [whole(a) for a in consts],
        out_specs=[pl.BlockSpec((tm, n_q), lambda i: (i, 0)), pl.BlockSpec((tm, LANES), lambda i: (i, 0))],
        out_shape=[jax.ShapeDtypeStruct((m, n_q), bf16), jax.ShapeDtypeStruct((m, LANES), f32)],
        compiler_params=pltpu.CompilerParams(dimension_semantics=("parallel",), vmem_limit_bytes=VMEM_LIMIT),
        name="nsa_q",
    )(x.reshape(m, D_MODEL), *consts)
    return (q.reshape(bsz, t_len, G, HPG, DH),
            gt[:, :3 * NSA_HEADS].reshape(bsz, t_len, G, HPG, 3))


def _res_mm_kernel(a_ref, x_ref, w_ref, o_ref):
    o_ref[...] = x_ref[...] + jnp.dot(a_ref[...].astype(bf16), w_ref[...], preferred_element_type=f32)


def residual_mm(a, x, w):
    bsz, t_len, k = a.shape
    n = w.shape[1]
    m = bsz * t_len
    tm = MM_TM if m % MM_TM == 0 else m
    out = pl.pallas_call(
        _res_mm_kernel,
        grid=(m // tm,),
        in_specs=[pl.BlockSpec((tm, k), lambda i: (i, 0)), pl.BlockSpec((tm, n), lambda i: (i, 0)),
                  pl.BlockSpec((k, n), lambda i: (0, 0))],
        out_specs=pl.BlockSpec((tm, n), lambda i: (i, 0)),
        out_shape=jax.ShapeDtypeStruct((m, n), f32),
        compiler_params=pltpu.CompilerParams(dimension_semantics=("parallel",), vmem_limit_bytes=VMEM_LIMIT),
        name="residual_mm",
    )(a.reshape(m, k), x.reshape(m, n), w.astype(bf16))
    return out.reshape(bsz, t_len, n)


def compress_rows_prompt(rows, pe, w1, b1, w2, b2):
    bsz, t_len = rows.shape[:2]
    n_sub = t_len // CMP_STRIDE
    sub = rows.reshape(bsz, n_sub, CMP_STRIDE, G, DH).transpose(0, 1, 3, 2, 4)
    half = CMP_STRIDE * DH
    x0 = (sub + pe[:CMP_STRIDE][None, None, None]).reshape(bsz * n_sub * G, half)
    x1 = (sub + pe[CMP_STRIDE:][None, None, None]).reshape(bsz * n_sub * G, half)
    p0 = mm(x0, w1[:half]).reshape(bsz, n_sub, G, CMP_HID)
    p1 = mm(x1, w1[half:]).reshape(bsz, n_sub, G, CMP_HID)
    hid = jax.nn.gelu(p0[:, :-1] + p1[:, 1:] + b1)
    out = mm(hid.reshape(bsz * (n_sub - 1) * G, CMP_HID), w2) + b2
    return out.reshape(bsz, n_sub - 1, G, DH)


def prompt_context(h, p):
    paged, win = kv_rows(h, p)
    bsz, t_len = h.shape[:2]
    ck = rmsnorm(compress_rows_prompt(paged[:, :, 0], p['cmp_pe'][0], p['cmp_w1'][0], p['cmp_b1'][0],
                                      p['cmp_w2'][0], p['cmp_b2'][0]), p['k_norm_g'][0])
    cv = compress_rows_prompt(paged[:, :, 1], p['cmp_pe'][1], p['cmp_w1'][1], p['cmp_b1'][1],
                              p['cmp_w2'][1], p['cmp_b2'][1])
    n_cmp = ck.shape[1]
    pad = lambda c: jnp.pad(c, ((0, 0), (0, NC_PAD - n_cmp), (0, 0), (0, 0)))
    heads = lambda x: x.transpose(0, 2, 1, 3).astype(bf16)
    heads_t = lambda x: x.transpose(0, 2, 3, 1).astype(bf16)
    onehot = jnp.asarray(np.arange(t_len)[:, None] // SEL_BLOCK == np.arange(NB_PROMPT)[None, :], bf16)
    ke = jnp.concatenate([jnp.broadcast_to(onehot, (bsz, G, t_len, NB_PROMPT)), heads(paged[:, :, 2])], axis=-1)
    ctx = dict(ck=heads(pad(ck)), cvt=heads_t(pad(cv)), n_cmp=n_cmp, ke=ke, vst=heads_t(paged[:, :, 3]),
               kw=heads(win[:, :, 0]), vwt=heads_t(win[:, :, 1]))
    keep = min(WINDOW, t_len)
    return ctx, (paged, win[:, t_len - keep:])


def attend_prompt(q, gates, ctx):
    bsz, t_len = q.shape[:2]
    gt = gates.reshape(bsz, t_len, G, 3 * HPG).transpose(0, 2, 1, 3)
    return nsa_prompt_attention(q.reshape(bsz, t_len, NSA_HEADS * DH).astype(bf16), gt, ctx['ck'], ctx['cvt'],
                                ctx['ke'], ctx['vst'], ctx['kw'], ctx['vwt'], ctx['n_cmp'])


def sample_context(h, cache3, page_table, cache_kv_win, p):
    paged, win = kv_rows(h, p)
    bsz, t_new = h.shape[:2]
    ck, cv = sample_compress(cache3, page_table, p['cmp_pe'], p['cmp_w1'], p['cmp_b1'], p['cmp_w2'], p['cmp_b2'],
                             p['k_norm_g'][0:1])
    n_buf = cache_kv_win.shape[1]
    pad_rows = lambda x: jnp.pad(x.reshape(bsz, t_new, HALF), ((0, 0), (0, NEW_PAD - t_new), (0, 0)))
    ctx = dict(ck=ck, cv=cv, new_sel=pad_rows(paged[:, :, 2:4]), new_win=pad_rows(win),
               win_cache=cache_kv_win.reshape(bsz, n_buf, HALF))
    win_full = jnp.concatenate([cache_kv_win, win], axis=1)
    return ctx, (paged, win_full[:, win_full.shape[1] - n_buf:])


def attend_sample(q, gates, ctx, cache3, page_table):
    bsz, t_new = q.shape[:2]
    rows = lambda x: x.transpose(0, 2, 3, 1, 4).reshape(bsz, G // 2, 2 * HPG * t_new, x.shape[-1])
    qr = rows(q)
    own = (jnp.arange(2 * HPG * t_new) // (HPG * t_new))[:, None] == (jnp.arange(LANES) // DH)[None, :]
    q2 = jnp.where(own, jnp.concatenate([qr, qr], axis=-1), 0.0).astype(bf16)
    o = sample_attention(cache3, page_table, q2, rows(gates), ctx['ck'], ctx['cv'],
                         ctx['new_sel'], ctx['win_cache'], ctx['new_win'])
    o = jnp.where(own, o, 0.0)
    o = o[..., :DH] + o[..., DH:]
    return o.reshape(bsz, G, HPG, t_new, DH).transpose(0, 3, 1, 2, 4).reshape(bsz, t_new, NSA_HEADS * DH)


def forward(x, wkv_in, shift_in, conv_in, make_ctx, attend, p, scan_cfg):
    wkv_out, shift_out, conv_out = [], [], []
    v_first = None
    ctx = None
    kv_state = None
    for layer in range(DEPTH):
        if layer == N_A_LAYERS:
            ctx, kv_state = make_ctx(x)
        if layer < N_A_LAYERS:
            x, s_new, sh_new, v_first = rwkv7_layer(x, shift_in[layer], wkv_in[layer], v_first, layer, p, scan_cfg)
            wkv_out.append(s_new)
            shift_out.append(sh_new)
        else:
            lb = layer - N_A_LAYERS
            q, gates = nsa_query(x, p['norm_g'][layer, 0], lb, p)
            x = residual_mm(attend(q, gates, ctx), x, p['nsa_w_o'][lb])
        ffn_w = (p['ffn_w_in'][layer], p['ffn_conv_w'][layer], p['ffn_conv_b'][layer], p['ffn_w_out'][layer])
        if x.shape[1] % FFN_TM == 0:
            x, conv_new = conv_ffn_fused(x, p['norm_g'][layer, 1], conv_in[layer], *ffn_w)
        else:
            f, conv_new = conv_ffn(rmsnorm(x, p['norm_g'][layer, 1]), conv_in[layer], *ffn_w)
            x = x + f
        conv_out.append(conv_new)
    return x, kv_state, jnp.stack(wkv_out), jnp.stack(shift_out), jnp.stack(conv_out)


def kernel(x_prompt, x_sample, cache_kv, cache_kv_win, state_wkv, state_shift, state_ffn_conv, page_table, norm_g, rw_mu, rw_w_r, rw_w_k, rw_w_v, rw_w_o, rw_w0, rw_w1, rw_w2, rw_a0, rw_a1, rw_a2, rw_v0, rw_v1, rw_v2, rw_g1, rw_g2, rw_k_k, rw_k_a, rw_r_k, rw_ln_w, rw_ln_b, kv_norm_g, w_kv, k_norm_g, cmp_pe, cmp_w1, cmp_b1, cmp_w2, cmp_b2, nsa_w_q, q_norm_g, nsa_w_o, ffn_w_in, ffn_conv_w, ffn_conv_b, ffn_w_out):
    p = dict(norm_g=norm_g, rw_mu=rw_mu, rw_w_r=rw_w_r, rw_w_k=rw_w_k, rw_w_v=rw_w_v, rw_w_o=rw_w_o,
             rw_w0=rw_w0, rw_w1=rw_w1, rw_w2=rw_w2, rw_a0=rw_a0, rw_a1=rw_a1, rw_a2=rw_a2,
             rw_v0=rw_v0, rw_v1=rw_v1, rw_v2=rw_v2, rw_g1=rw_g1, rw_g2=rw_g2, rw_k_k=rw_k_k, rw_k_a=rw_k_a,
             rw_r_k=rw_r_k, rw_ln_w=rw_ln_w, rw_ln_b=rw_ln_b, kv_norm_g=kv_norm_g, w_kv=w_kv,
             k_norm_g=k_norm_g, cmp_pe=cmp_pe, cmp_w1=cmp_w1, cmp_b1=cmp_b1, cmp_w2=cmp_w2, cmp_b2=cmp_b2,
             nsa_w_q=nsa_w_q, q_norm_g=q_norm_g, nsa_w_o=nsa_w_o, ffn_w_in=ffn_w_in,
             ffn_conv_w=ffn_conv_w, ffn_conv_b=ffn_conv_b, ffn_w_out=ffn_w_out)
    bp, t_p = x_prompt.shape[:2]
    bs = x_sample.shape[0]

    wkv0 = jnp.zeros((N_A_LAYERS, bp, RW_HEADS, RW_HEAD_DIM, RW_HEAD_DIM), f32)
    shift0 = jnp.zeros((N_A_LAYERS, bp, D_MODEL), f32)
    conv0 = jnp.zeros((DEPTH, bp, CONV_W - 1, D_FF), f32)
    y_prompt, (kv_prompt, win_prompt), wkv_prompt, shift_prompt, conv_prompt = forward(
        x_prompt, wkv0, shift0, conv0, lambda h: prompt_context(h, p), attend_prompt, p,
        dict(bb=bp, tc=128))

    cache3 = cache_kv.reshape(cache_kv.shape[0], PAGE_SIZE, 4 * G * DH)
    y_sample, (kv_sample, win_sample), wkv_sample, shift_sample, conv_sample = forward(
        x_sample, state_wkv, state_shift, state_ffn_conv,
        lambda h: sample_context(h, cache3, page_table, cache_kv_win, p),
        lambda q, gates, ctx: attend_sample(q, gates, ctx, cache3, page_table), p,
        dict(bb=2, tc=x_sample.shape[1]))

    return (y_prompt, y_sample, kv_prompt, kv_sample, win_prompt, win_sample,
            wkv_prompt, wkv_sample, shift_prompt, shift_sample, conv_prompt, conv_sample)
```
